```python
import math
import jax
import jax.numpy as jnp
from jax import lax
import numpy as np

D_MODEL = 1024
BATCH = 8
SEQ = 2048
DEPTH = 4
DEC_BATCH = 128
DEC_SEQ = 1
PAST_LEN = 16384
PAGE_SIZE = 128

BRANCH_WIDTH = D_MODEL // 2
SSM_WIDTH = BRANCH_WIDTH
SSM_GROUP = 16
SSM_GROUPS = SSM_WIDTH // SSM_GROUP
SSM_STATE = 64
POOL_WIDTH = BRANCH_WIDTH
POOL_WINDOWS = (2, 4, 8, 16)
POOL_GROUPS = len(POOL_WINDOWS)
POOL_GROUP_WIDTH = POOL_WIDTH // POOL_GROUPS
POOL_HIST = max(POOL_WINDOWS) - 1
N_MEM = 256
MEM_HEADS = 4
MEM_HEAD_DIM = BRANCH_WIDTH // MEM_HEADS
MEM_WIDTH = MEM_HEADS * MEM_HEAD_DIM
N_BRANCH = 3
IN_WIDTH = SSM_WIDTH + POOL_WIDTH + MEM_WIDTH + N_BRANCH * D_MODEL
D_FF = -(-(8 * D_MODEL) // (3 * 256)) * 256
RMS_EPS = 1e-6
DT_MIN = 1e-3
DT_MAX = 1e-1

kernel_name = "hybrid_s5_pool_memxattn_step"


def rmsnorm(x, g):
    xf = x.astype(jnp.float32)
    y = xf * lax.rsqrt(jnp.mean(xf * xf, axis=-1, keepdims=True) + RMS_EPS)
    return (y * g.astype(jnp.float32)).astype(x.dtype)


def s5_discretise(lam_re, lam_im, log_dt, b_re, b_im):
    dt = jnp.exp(log_dt.astype(jnp.float32))[:, None]
    lr = lam_re.astype(jnp.float32)
    li = lam_im.astype(jnp.float32)
    zr, zi = lr * dt, li * dt
    mag = jnp.exp(zr)
    ar, ai = mag * jnp.cos(zi), mag * jnp.sin(zi)
    den = lr * lr + li * li
    fr = ((ar - 1.0) * lr + ai * li) / den
    fi = (ai * lr - (ar - 1.0) * li) / den
    br, bi = b_re.astype(jnp.float32), b_im.astype(jnp.float32)
    bbr = fr[..., None] * br - fi[..., None] * bi
    bbi = fr[..., None] * bi + fi[..., None] * br
    return zr, zi, ar, ai, bbr, bbi


def _complex_affine_combine(e1, e2):
    a1r, a1i, b1r, b1i = e1
    a2r, a2i, b2r, b2i = e2
    return (a2r * a1r - a2i * a1i, a2r * a1i + a2i * a1r,
            a2r * b1r - a2i * b1i + b2r, a2r * b1i + a2i * b1r + b2i)


def s5_branch(u, h0, lam_re, lam_im, log_dt, b_re, b_im, c_re, c_im, d_skip, w_glu, b_glu):
    n, t, _ = u.shape
    uf = u.astype(jnp.float32).reshape(n, t, SSM_GROUPS, SSM_GROUP)
    zr, zi, ar, ai, bbr, bbi = s5_discretise(lam_re, lam_im, log_dt, b_re, b_im)
    bu_re = jnp.einsum('ntgh,gph->ntgp', uf, bbr)
    bu_im = jnp.einsum('ntgh,gph->ntgp', uf, bbi)
    a_re = jnp.broadcast_to(ar, bu_re.shape)
    a_im = jnp.broadcast_to(ai, bu_im.shape)
    _, _, h_re, h_im = lax.associative_scan(
        _complex_affine_combine, (a_re, a_im, bu_re, bu_im), axis=1)
    if h0 is not None:
        h0r = h0[0].astype(jnp.float32)[:, None]
        h0i = h0[1].astype(jnp.float32)[:, None]
        steps = jnp.arange(1, t + 1, dtype=jnp.float32)[:, None, None]
        mag = jnp.exp(zr * steps)
        pr, pim = mag * jnp.cos(zi * steps), mag * jnp.sin(zi * steps)
        h_re, h_im = h_re + pr * h0r - pim * h0i, h_im + pr * h0i + pim * h0r
    y = (jnp.einsum('ntgp,ghp->ntgh', h_re, c_re.astype(jnp.float32))
         - jnp.einsum('ntgp,ghp->ntgh', h_im, c_im.astype(jnp.float32)))
    y = y.reshape(n, t, SSM_WIDTH) + d_skip.astype(jnp.float32) * uf.reshape(n, t, SSM_WIDTH)
    y = jax.nn.gelu(y)
    y = y * jax.nn.sigmoid(y @ w_glu.astype(jnp.float32) + b_glu.astype(jnp.float32))
    return y.astype(u.dtype), h_re[:, -1].astype(u.dtype), h_im[:, -1].astype(u.dtype)


def pool_branch(u, hist, pos0, pool_w, pool_scale):
    n, t, _ = u.shape
    uf = u.astype(jnp.float32)
    if hist is None:
        hist_f = jnp.zeros((n, POOL_HIST, POOL_WIDTH), jnp.float32)
    else:
        hist_f = hist.astype(jnp.float32)
    full = jnp.concatenate([hist_f, uf], axis=1)
    cs = jnp.concatenate([jnp.zeros((n, 1, POOL_WIDTH), jnp.float32),
                          jnp.cumsum(full, axis=1)], axis=1)
    pos = pos0 + jnp.arange(t, dtype=jnp.int32)
    outs = []
    for gi, w in enumerate(POOL_WINDOWS):
        sl = slice(gi * POOL_GROUP_WIDTH, (gi + 1) * POOL_GROUP_WIDTH)
        end = cs[:, POOL_HIST + 1:POOL_HIST + 1 + t, sl]
        start = cs[:, POOL_HIST + 1 - w:POOL_HIST + 1 - w + t, sl]
        cnt = jnp.minimum(pos + 1, w).astype(jnp.float32)[None, :, None]
        outs.append((end - start) / cnt)
    pooled = jnp.concatenate(outs, axis=-1) - uf
    mixed = jnp.einsum('ntgc,gcd->ntgd',
                       pooled.reshape(n, t, POOL_GROUPS, POOL_GROUP_WIDTH),
                       pool_w.astype(jnp.float32)).reshape(n, t, POOL_WIDTH)
    mixed = mixed * pool_scale.astype(jnp.float32)
    return mixed.astype(u.dtype), full[:, -POOL_HIST:].astype(u.dtype)


def mem_kv(mem, g_mem, w_kv):
    kv = rmsnorm(mem, g_mem) @ w_kv
    n = mem.shape[0]
    k = kv[..., :MEM_WIDTH].reshape(n, N_MEM, MEM_HEADS, MEM_HEAD_DIM)
    v = kv[..., MEM_WIDTH:].reshape(n, N_MEM, MEM_HEADS, MEM_HEAD_DIM)
    return k, v


def mem_attention(q, k, v):
    n, t, _ = q.shape
    qh = q.reshape(n, t, MEM_HEADS, MEM_HEAD_DIM)
    s = jnp.einsum('nthd,nmhd->nhtm', qh, k).astype(jnp.float32) * (MEM_HEAD_DIM ** -0.5)
    p = jax.nn.softmax(s, axis=-1).astype(v.dtype)
    o = jnp.einsum('nhtm,nmhd->nthd', p, v)
    return o.reshape(n, t, MEM_WIDTH)


def trunk_layer(x, p, carried, pos0, k_mem, v_mem):
    n, t, _ = x.shape
    h = rmsnorm(x, p['g_mix_pre'])
    proj = h @ p['w_in']
    o1 = SSM_WIDTH
    o2 = o1 + POOL_WIDTH
    o3 = o2 + MEM_WIDTH
    u_ssm, u_pool, q_mem, gate_logits = proj[..., :o1], proj[..., o1:o2], proj[..., o2:o3], proj[..., o3:]
    h0 = None if carried is None else (carried[0], carried[1])
    hist = None if carried is None else carried[2]
    o_ssm, h_re, h_im = s5_branch(u_ssm, h0, p['lam_re'], p['lam_im'], p['log_dt'], p['b_re'], p['b_im'],
                                  p['c_re'], p['c_im'], p['d_skip'], p['w_glu'], p['b_glu'])
    o_pool, new_hist = pool_branch(u_pool, hist, pos0, p['pool_w'], p['pool_scale'])
    o_mem = mem_attention(q_mem, k_mem, v_mem)
    branches = jnp.stack([o_ssm, o_pool, o_mem], axis=2)
    up = jnp.einsum('ntbc,bcd->ntbd', branches, p['w_branch_up'])
    gates = jax.nn.sigmoid(gate_logits.astype(jnp.float32).reshape(n, t, N_BRANCH, D_MODEL))
    merged = jnp.sum(gates * up.astype(jnp.float32), axis=2).astype(x.dtype)
    x = x + rmsnorm(merged @ p['w_out'], p['g_mix_post'])
    hf = rmsnorm(x, p['g_ffn_pre']) @ p['w_ffn_in']
    f = (jax.nn.silu(hf[..., :D_FF]) * hf[..., D_FF:]) @ p['w_ffn_out']
    x = x + rmsnorm(f, p['g_ffn_post'])
    return x, h_re, h_im, new_hist


def setup_inputs(seed: int = 0) -> dict:
    key = jax.random.key(seed)
    ks = iter(jax.random.split(key, 40))
    f32 = jnp.float32

    def nrm(shape, scale):
        return jax.random.normal(next(ks), shape, f32) * scale

    def gain(shape):
        return 1.0 + nrm(shape, 0.05)

    lam_re = -0.5 + nrm((DEPTH, SSM_GROUPS, SSM_STATE), 0.01)
    lam_im = (jnp.pi * jnp.arange(SSM_STATE, dtype=f32))[None, None, :] + nrm((DEPTH, SSM_GROUPS, SSM_STATE), 0.01)
    log_dt = jax.random.uniform(next(ks), (DEPTH, SSM_GROUPS), f32, math.log(DT_MIN), math.log(DT_MAX))
    return {
        'x_prompt': nrm((BATCH, SEQ, D_MODEL), 1.0),
        'x_sample': nrm((DEC_BATCH, DEC_SEQ, D_MODEL), 1.0),
        'mem_prompt': nrm((BATCH, N_MEM, D_MODEL), 1.0),
        'cache_mem_k': nrm((DEPTH, DEC_BATCH, N_MEM, MEM_HEADS, MEM_HEAD_DIM), 1.0),
        'cache_mem_v': nrm((DEPTH, DEC_BATCH, N_MEM, MEM_HEADS, MEM_HEAD_DIM), 1.0),
        'state_ssm_re': nrm((DEPTH, DEC_BATCH, SSM_GROUPS, SSM_STATE), 0.5),
        'state_ssm_im': nrm((DEPTH, DEC_BATCH, SSM_GROUPS, SSM_STATE), 0.5),
        'state_pool': nrm((DEPTH, DEC_BATCH, POOL_HIST, POOL_WIDTH), 1.0),
        'g_mix_pre': gain((DEPTH, D_MODEL)),
        'g_mix_post': gain((DEPTH, D_MODEL)),
        'g_ffn_pre': gain((DEPTH, D_MODEL)),
        'g_ffn_post': gain((DEPTH, D_MODEL)),
        'g_mem': gain((DEPTH, D_MODEL)),
        'w_in': nrm((DEPTH, D_MODEL, IN_WIDTH), D_MODEL ** -0.5),
        'w_kv': nrm((DEPTH, D_MODEL, 2 * MEM_WIDTH), D_MODEL ** -0.5),
        'ssm_lam_re': lam_re,
        'ssm_lam_im': lam_im,
        'ssm_log_dt': log_dt,
        'ssm_b_re': nrm((DEPTH, SSM_GROUPS, SSM_STATE, SSM_GROUP), (2 * SSM_GROUP) ** -0.5),
        'ssm_b_im': nrm((DEPTH, SSM_GROUPS, SSM_STATE, SSM_GROUP), (2 * SSM_GROUP) ** -0.5),
        'ssm_c_re': nrm((DEPTH, SSM_GROUPS, SSM_GROUP, SSM_STATE), (2 * SSM_STATE) ** -0.5),
        'ssm_c_im': nrm((DEPTH, SSM_GROUPS, SSM_GROUP, SSM_STATE), (2 * SSM_STATE) ** -0.5),
        'ssm_d': nrm((DEPTH, SSM_WIDTH), 1.0),
        'ssm_w_glu': nrm((DEPTH, SSM_WIDTH, SSM_WIDTH), SSM_WIDTH ** -0.5),
        'ssm_b_glu': nrm((DEPTH, SSM_WIDTH), 0.02),
        'pool_w': nrm((DEPTH, POOL_GROUPS, POOL_GROUP_WIDTH, POOL_GROUP_WIDTH), POOL_GROUP_WIDTH ** -0.5),
        'pool_scale': 1.0 + nrm((DEPTH, POOL_WIDTH), 0.1),
        'w_branch_up': nrm((DEPTH, N_BRANCH, BRANCH_WIDTH, D_MODEL), BRANCH_WIDTH ** -0.5),
        'w_out': nrm((DEPTH, D_MODEL, D_MODEL), D_MODEL ** -0.5),
        'w_ffn_in': nrm((DEPTH, D_MODEL, 2 * D_FF), D_MODEL ** -0.5),
        'w_ffn_out': nrm((DEPTH, D_FF, D_MODEL), D_FF ** -0.5),
    }


def reference(x_prompt, x_sample, mem_prompt, cache_mem_k, cache_mem_v, state_ssm_re, state_ssm_im, state_pool,
              g_mix_pre, g_mix_post, g_ffn_pre, g_ffn_post, g_mem, w_in, w_kv,
              ssm_lam_re, ssm_lam_im, ssm_log_dt, ssm_b_re, ssm_b_im, ssm_c_re, ssm_c_im, ssm_d,
              ssm_w_glu, ssm_b_glu, pool_w, pool_scale, w_branch_up, w_out, w_ffn_in, w_ffn_out):
    yp, ys = x_prompt, x_sample
    re_p, im_p, pool_p, mk_p, mv_p = [], [], [], [], []
    re_s, im_s, pool_s = [], [], []
    for l in range(DEPTH):
        p = {
            'g_mix_pre': g_mix_pre[l], 'g_mix_post': g_mix_post[l],
            'g_ffn_pre': g_ffn_pre[l], 'g_ffn_post': g_ffn_post[l],
            'w_in': w_in[l], 'lam_re': ssm_lam_re[l], 'lam_im': ssm_lam_im[l], 'log_dt': ssm_log_dt[l],
            'b_re': ssm_b_re[l], 'b_im': ssm_b_im[l], 'c_re': ssm_c_re[l], 'c_im': ssm_c_im[l],
            'd_skip': ssm_d[l], 'w_glu': ssm_w_glu[l], 'b_glu': ssm_b_glu[l],
            'pool_w': pool_w[l], 'pool_scale': pool_scale[l],
            'w_branch_up': w_branch_up[l], 'w_out': w_out[l],
            'w_ffn_in': w_ffn_in[l], 'w_ffn_out': w_ffn_out[l],
        }
        k_m, v_m = mem_kv(mem_prompt, g_mem[l], w_kv[l])
        yp, hr, hi, hist = trunk_layer(yp, p, None, 0, k_m, v_m)
        re_p.append(hr)
        im_p.append(hi)
        pool_p.append(hist)
        mk_p.append(k_m)
        mv_p.append(v_m)
        carried = (state_ssm_re[l], state_ssm_im[l], state_pool[l])
        ys, hr, hi, hist = trunk_layer(ys, p, carried, PAST_LEN, cache_mem_k[l], cache_mem_v[l])
        re_s.append(hr)
        im_s.append(hi)
        pool_s.append(hist)
    return (yp, ys,
            jnp.stack(re_p), jnp.stack(im_p), jnp.stack(pool_p), jnp.stack(mk_p), jnp.stack(mv_p),
            jnp.stack(re_s), jnp.stack(im_s), jnp.stack(pool_s))
```

```python
import functools
import math

import jax
import jax.numpy as jnp
from jax import lax
from jax.experimental import pallas as pl
from jax.experimental.pallas import tpu as pltpu

D_MODEL = 1024
BATCH = 8
SEQ = 2048
DEPTH = 4
DEC_BATCH = 128
PAST_LEN = 16384

BRANCH = D_MODEL // 2
SSM_GROUP = 16
SSM_GROUPS = BRANCH // SSM_GROUP
SSM_STATE = 64
N_STATE = SSM_GROUPS * SSM_STATE
POOL_WINDOWS = (2, 4, 8, 16)
POOL_GROUP_WIDTH = BRANCH // len(POOL_WINDOWS)
POOL_HIST = max(POOL_WINDOWS) - 1
N_MEM = 256
MEM_HEADS = 4
MEM_HEAD_DIM = BRANCH // MEM_HEADS
N_BRANCH = 3
D_FF = 2816
RMS_EPS = 1e-6

LANES = 128
SUBLANES = 8
GROUPS_PER_BLOCK = LANES // SSM_GROUP
N_SSM_BLOCKS = BRANCH // LANES
BLOCK_STATE = GROUPS_PER_BLOCK * SSM_STATE

ROWS_A = 512
STEPS_A = ROWS_A // BATCH
ROWS_C = 512
TQ_ATT = 1024
SAMPLE_BLOCK = 8
SCAN_LANES = 1024
FF_CHUNKS = ((0, 1536), (1536, 2816))
VMEM_LIMIT = 56 * 1024 * 1024

BF16 = jnp.bfloat16
F32 = jnp.float32


def _rms(x, g):
    ms = jnp.mean(x * x, axis=-1, keepdims=True)
    return x * lax.rsqrt(ms + RMS_EPS) * g


def _dot(a, b):
    return jnp.dot(a, b, preferred_element_type=F32)


def _const_spec(shape):
    nd = len(shape)
    return pl.BlockSpec(shape, lambda *_: (0,) * nd, pipeline_mode=pl.Buffered(1))


def _layer_spec(shape, layer):
    nd = len(shape)
    return pl.BlockSpec((None,) + tuple(shape), lambda *_: (layer,) + (0,) * nd,
                        pipeline_mode=pl.Buffered(1))


def _discretise_kernel(lr_ref, li_ref, ldt_ref, br_ref, bi_ref,
                       ar_ref, ai_ref, bbr_ref, bbi_ref):
    lr = lr_ref[...]
    li = li_ref[...]
    dt = jnp.exp(ldt_ref[...])
    zr = lr * dt
    zi = li * dt
    mag = jnp.exp(zr)
    ar = mag * jnp.cos(zi)
    ai = mag * jnp.sin(zi)
    den = lr * lr + li * li
    fr = ((ar - 1.0) * lr + ai * li) / den
    fi = (ai * lr - (ar - 1.0) * li) / den
    br = br_ref[...]
    bi = bi_ref[...]
    ar_ref[...] = ar
    ai_ref[...] = ai
    bbr_ref[...] = fr * br - fi * bi
    bbi_ref[...] = fr * bi + fi * br


def _discretise(lam_re, lam_im, log_dt, b_re, b_im):
    rows = DEPTH * SSM_GROUPS
    width = SSM_GROUP * SSM_STATE

    def tile_p(a):
        return jnp.tile(a.reshape(rows, 1, SSM_STATE), (1, SSM_GROUP, 1)).reshape(rows, width)

    def b_t(a):
        return jnp.swapaxes(a, -1, -2).reshape(rows, width)

    ldt = jnp.broadcast_to(log_dt.reshape(rows, 1), (rows, width))
    out = jax.ShapeDtypeStruct((rows, width), F32)
    ar, ai, bbr, bbi = pl.pallas_call(
        _discretise_kernel, out_shape=(out, out, out, out), name="s5_discretise",
    )(tile_p(lam_re), tile_p(lam_im), ldt, b_t(b_re), b_t(b_im))
    ar = ar[:, :SSM_STATE].reshape(DEPTH, 1, N_STATE)
    ai = ai[:, :SSM_STATE].reshape(DEPTH, 1, N_STATE)
    shape5 = (DEPTH, N_SSM_BLOCKS, GROUPS_PER_BLOCK, SSM_GROUP, SSM_STATE)
    eye = jnp.eye(GROUPS_PER_BLOCK, dtype=F32)

    def b_blocks(bb):
        m = jnp.einsum('lkghp,gG->lkghGp', bb.reshape(shape5), eye)
        return m.reshape(DEPTH, N_SSM_BLOCKS, LANES, BLOCK_STATE)

    bblk = jnp.concatenate([b_blocks(bbr), b_blocks(bbi)], axis=-1).astype(BF16)
    return ar, ai, bblk


def _c_blocks(c):
    shape5 = (DEPTH, N_SSM_BLOCKS, GROUPS_PER_BLOCK, SSM_GROUP, SSM_STATE)
    eye = jnp.eye(GROUPS_PER_BLOCK, dtype=F32)
    m = jnp.einsum('lkghp,gG->lkgpGh', c.reshape(shape5), eye)
    return m.reshape(DEPTH, N_SSM_BLOCKS, BLOCK_STATE, LANES)


def _ssm_input(u_ssm, bblk_ref, bu_re_ref, bu_im_ref):
    for k in range(N_SSM_BLOCKS):
        uk = u_ssm[:, k * LANES:(k + 1) * LANES].astype(BF16)
        bu = _dot(uk, bblk_ref[k])
        sl = slice(k * BLOCK_STATE, (k + 1) * BLOCK_STATE)
        bu_re_ref[:, sl] = bu[:, :BLOCK_STATE]
        bu_im_ref[:, sl] = bu[:, BLOCK_STATE:]


def _ssm_output(h_re_ref, h_im_ref, u_ssm, cre_ref, cimn_ref, d_ref, wglu_ref, bglu_ref):
    ys = []
    for k in range(N_SSM_BLOCKS):
        sl = slice(k * BLOCK_STATE, (k + 1) * BLOCK_STATE)
        ys.append(_dot(h_re_ref[:, sl].astype(BF16), cre_ref[k])
                  + _dot(h_im_ref[:, sl].astype(BF16), cimn_ref[k]))
    y = jnp.concatenate(ys, axis=-1) + d_ref[...] * u_ssm
    y = jax.nn.gelu(y)
    return y * jax.nn.sigmoid(_dot(y.astype(BF16), wglu_ref[...]) + bglu_ref[...])


def _pool_mix(pooled, poolw_ref, pscale_ref):
    outs = []
    for gi in range(len(POOL_WINDOWS)):
        sl = slice(gi * POOL_GROUP_WIDTH, (gi + 1) * POOL_GROUP_WIDTH)
        outs.append(_dot(pooled[gi].astype(BF16), poolw_ref[gi]) * pscale_ref[:, sl])
    return jnp.concatenate(outs, axis=-1)


def _mixer_a_kernel(x_ref, g_ref, win_ref, bblk_ref, cre_ref, cimn_ref, ar_ref, ai_ref,
                    d_ref, wglu_ref, bglu_ref, poolw_ref, pscale_ref,
                    q_ref, ossm_ref, opool_ref, hre_ref, him_ref, hist_ref,
                    bu_re_ref, bu_im_ref, pool_buf):
    i = pl.program_id(0)
    hist_rows = POOL_HIST * BATCH

    @pl.when(i == 0)
    def _():
        hre_ref[...] = jnp.zeros_like(hre_ref)
        him_ref[...] = jnp.zeros_like(him_ref)
        pool_buf[0:hist_rows, :] = jnp.zeros((hist_rows, BRANCH), F32)

    h = _rms(x_ref[...], g_ref[...]).astype(BF16)
    proj = _dot(h, win_ref[...])
    u_ssm = proj[:, :BRANCH]
    u_pool = proj[:, BRANCH:2 * BRANCH]
    q_ref[...] = proj[:, 2 * BRANCH:].astype(BF16)

    _ssm_input(u_ssm, bblk_ref, bu_re_ref, bu_im_ref)
    for c in range(N_STATE // SCAN_LANES):
        sl = slice(c * SCAN_LANES, (c + 1) * SCAN_LANES)
        a_r = jnp.broadcast_to(ar_ref[:, sl], (BATCH, SCAN_LANES))
        a_i = jnp.broadcast_to(ai_ref[:, sl], (BATCH, SCAN_LANES))

        def step(t, carry, sl=sl, a_r=a_r, a_i=a_i):
            hr, hi = carry
            rows = pl.ds(pl.multiple_of(t * BATCH, BATCH), BATCH)
            nhr = a_r * hr - a_i * hi + bu_re_ref[rows, sl]
            nhi = a_r * hi + a_i * hr + bu_im_ref[rows, sl]
            bu_re_ref[rows, sl] = nhr
            bu_im_ref[rows, sl] = nhi
            return nhr, nhi

        hr, hi = lax.fori_loop(0, STEPS_A, step, (hre_ref[:, sl], him_ref[:, sl]), unroll=4)
        hre_ref[:, sl] = hr
        him_ref[:, sl] = hi
    y = _ssm_output(bu_re_ref, bu_im_ref, u_ssm, cre_ref, cimn_ref, d_ref, wglu_ref, bglu_ref)
    ossm_ref[...] = y.astype(BF16)

    pool_buf[hist_rows:hist_rows + ROWS_A, :] = u_pool
    t_pos = i * STEPS_A + lax.broadcasted_iota(jnp.int32, (ROWS_A, POOL_GROUP_WIDTH), 0) // BATCH
    pooled = []
    for gi, w in enumerate(POOL_WINDOWS):
        sl = slice(gi * POOL_GROUP_WIDTH, (gi + 1) * POOL_GROUP_WIDTH)
        acc = u_pool[:, sl]
        for j in range(1, w):
            start = hist_rows - j * BATCH
            acc = acc + pool_buf[start:start + ROWS_A, sl]
        cnt = jnp.minimum(t_pos + 1, w).astype(F32)
        pooled.append(acc / cnt - u_pool[:, sl])
    opool_ref[...] = _pool_mix(pooled, poolw_ref, pscale_ref).astype(BF16)
    pool_buf[0:hist_rows, :] = pool_buf[ROWS_A:ROWS_A + hist_rows, :]

    @pl.when(i == pl.num_programs(0) - 1)
    def _():
        hist_ref[...] = pool_buf[0:hist_rows, :]


def _mixer_a(layer, x_tm, p):
    n_rows = x_tm.shape[0]
    hist_rows = POOL_HIST * BATCH
    row_spec = lambda w: pl.BlockSpec((ROWS_A, w), lambda i: (i, 0))
    in_specs = [
        row_spec(D_MODEL),
        _layer_spec((1, D_MODEL), layer),
        _layer_spec((D_MODEL, 3 * BRANCH), layer),
        _layer_spec((N_SSM_BLOCKS, LANES, 2 * BLOCK_STATE), layer),
        _layer_spec((N_SSM_BLOCKS, BLOCK_STATE, LANES), layer),
        _layer_spec((N_SSM_BLOCKS, BLOCK_STATE, LANES), layer),
        _layer_spec((1, N_STATE), layer),
        _layer_spec((1, N_STATE), layer),
        _layer_spec((1, BRANCH), layer),
        _layer_spec((BRANCH, BRANCH), layer),
        _layer_spec((1, BRANCH), layer),
        _layer_spec((len(POOL_WINDOWS), POOL_GROUP_WIDTH, POOL_GROUP_WIDTH), layer),
        _layer_spec((1, BRANCH), layer),
    ]
    out_shape = (
        jax.ShapeDtypeStruct((n_rows, BRANCH), BF16),
        jax.ShapeDtypeStruct((n_rows, BRANCH), BF16),
        jax.ShapeDtypeStruct((n_rows, BRANCH), BF16),
        jax.ShapeDtypeStruct((BATCH, N_STATE), F32),
        jax.ShapeDtypeStruct((BATCH, N_STATE), F32),
        jax.ShapeDtypeStruct((hist_rows, BRANCH), F32),
    )
    out_specs = (
        row_spec(BRANCH), row_spec(BRANCH), row_spec(BRANCH),
        pl.BlockSpec((BATCH, N_STATE), lambda i: (0, 0)),
        pl.BlockSpec((BATCH, N_STATE), lambda i: (0, 0)),
        pl.BlockSpec((hist_rows, BRANCH), lambda i: (0, 0)),
    )
    return pl.pallas_call(
        _mixer_a_kernel,
        grid=(n_rows // ROWS_A,),
        in_specs=in_specs, out_specs=out_specs, out_shape=out_shape,
        scratch_shapes=[
            pltpu.VMEM((ROWS_A, N_STATE), F32),
            pltpu.VMEM((ROWS_A, N_STATE), F32),
            pltpu.VMEM((hist_rows + ROWS_A, BRANCH), F32),
        ],
        compiler_params=pltpu.CompilerParams(
            dimension_semantics=("arbitrary",), vmem_limit_bytes=VMEM_LIMIT),
        name="prompt_mixer_a",
    )(x_tm, p['g_mix_pre'], p['w_in_a'], p['bblk'], p['cre'], p['cimn'], p['ar'], p['ai'],
      p['ssm_d'], p['w_glu'], p['b_glu'], p['pool_w'], p['pool_scale'])


def _mem_kv_kernel(mem_ref, g_ref, wkv_ref, k_ref, v_ref):
    kv = _dot(_rms(mem_ref[...], g_ref[...]).astype(BF16), wkv_ref[...])
    k_ref[...] = kv[:, :BRANCH]
    v_ref[...] = kv[:, BRANCH:]


def _mem_kv(mem, g_mem, w_kv):
    out = jax.ShapeDtypeStruct((DEPTH, BATCH, N_MEM, BRANCH), F32)
    out_spec = pl.BlockSpec((None, None, N_MEM, BRANCH), lambda l, b: (l, b, 0, 0))
    return pl.pallas_call(
        _mem_kv_kernel,
        grid=(DEPTH, BATCH),
        in_specs=[
            pl.BlockSpec((None, N_MEM, D_MODEL), lambda l, b: (b, 0, 0)),
            pl.BlockSpec((None, 1, D_MODEL), lambda l, b: (l, 0, 0)),
            pl.BlockSpec((None, D_MODEL, 2 * BRANCH), lambda l, b: (l, 0, 0)),
        ],
        out_specs=(out_spec, out_spec), out_shape=(out, out),
        compiler_params=pltpu.CompilerParams(dimension_semantics=("arbitrary", "arbitrary")),
        name="mem_kv",
    )(mem, g_mem, w_kv)


def _attention_kernel(q_ref, k_ref, v_ref, o_ref):
    scale = MEM_HEAD_DIM ** -0.5
    for hd in range(MEM_HEADS):
        sl = slice(hd * MEM_HEAD_DIM, (hd + 1) * MEM_HEAD_DIM)
        s = lax.dot_general(q_ref[:, sl], k_ref[:, sl].astype(BF16),
                            (((1,), (1,)), ((), ())), preferred_element_type=F32) * scale
        e = jnp.exp(s - jnp.max(s, axis=-1, keepdims=True))
        prob = e / jnp.sum(e, axis=-1, keepdims=True)
        o_ref[:, sl] = _dot(prob.astype(BF16), v_ref[:, sl].astype(BF16)).astype(BF16)


def _attention(layer, q_bm, k_all, v_all):
    kv_spec = pl.BlockSpec((None, None, N_MEM, BRANCH), lambda b, t: (layer, b, 0, 0))
    q_spec = pl.BlockSpec((None, TQ_ATT, BRANCH), lambda b, t: (b, t, 0))
    return pl.pallas_call(
        _attention_kernel,
        grid=(BATCH, SEQ // TQ_ATT),
        in_specs=[q_spec, kv_spec, kv_spec],
        out_specs=q_spec,
        out_shape=jax.ShapeDtypeStruct((BATCH, SEQ, BRANCH), BF16),
        compiler_params=pltpu.CompilerParams(dimension_semantics=("parallel", "parallel")),
        name="prompt_attention",
    )(q_bm, k_all, v_all)


def _merge_ffn_kernel(x_ref, ossm_ref, opool_ref, omem_ref, gpre_ref, wgate_ref, wup_ref,
                      wout_ref, gpost_ref, gfpre_ref, wfin_ref, wfout_ref, gfpost_ref, y_ref):
    x = x_ref[...]
    h = _rms(x, gpre_ref[...]).astype(BF16)
    merged = None
    for b, o_ref in enumerate((ossm_ref, opool_ref, omem_ref)):
        gate = jax.nn.sigmoid(_dot(h, wgate_ref[:, b * D_MODEL:(b + 1) * D_MODEL]))
        term = gate * _dot(o_ref[...], wup_ref[b])
        merged = term if merged is None else merged + term
    x = x + _rms(_dot(merged.astype(BF16), wout_ref[...]), gpost_ref[...])
    hf = _rms(x, gfpre_ref[...]).astype(BF16)
    f = None
    for lo, hi in FF_CHUNKS:
        hg = _dot(hf, wfin_ref[:, lo:hi])
        hu = _dot(hf, wfin_ref[:, D_FF + lo:D_FF + hi])
        part = _dot((jax.nn.silu(hg) * hu).astype(BF16), wfout_ref[lo:hi, :])
        f = part if f is None else f + part
    y_ref[...] = x + _rms(f, gfpost_ref[...])


def _merge_ffn(layer, x, o_ssm, o_pool, o_mem, p):
    n_rows = x.shape[0]
    rows = min(ROWS_C, n_rows)
    row_spec = lambda w: pl.BlockSpec((rows, w), lambda i: (i, 0))
    in_specs = [
        row_spec(D_MODEL), row_spec(BRANCH), row_spec(BRANCH), row_spec(BRANCH),
        _layer_spec((1, D_MODEL), layer),
        _layer_spec((D_MODEL, N_BRANCH * D_MODEL), layer),
        _layer_spec((N_BRANCH, BRANCH, D_MODEL), layer),
        _layer_spec((D_MODEL, D_MODEL), layer),
        _layer_spec((1, D_MODEL), layer),
        _layer_spec((1, D_MODEL), layer),
        _layer_spec((D_MODEL, 2 * D_FF), layer),
        _layer_spec((D_FF, D_MODEL), layer),
        _layer_spec((1, D_MODEL), layer),
    ]
    return pl.pallas_call(
        _merge_ffn_kernel,
        grid=(n_rows // rows,),
        in_specs=in_specs, out_specs=row_spec(D_MODEL),
        out_shape=jax.ShapeDtypeStruct((n_rows, D_MODEL), F32),
        compiler_params=pltpu.CompilerParams(
            dimension_semantics=("parallel",), vmem_limit_bytes=VMEM_LIMIT),
        name="merge_ffn",
    )(x, o_ssm, o_pool, o_mem, p['g_mix_pre'], p['w_gate'], p['w_up'], p['w_out'],
      p['g_mix_post'], p['g_ffn_pre'], p['w_ffn_in'], p['w_ffn_out'], p['g_ffn_post'])


def _sample_mixer_kernel(x_ref, g_ref, win_ref, bblk_ref, cre_ref, cimn_ref, ar_ref, ai_ref,
                         d_ref, wglu_ref, bglu_ref, poolw_ref, pscale_ref,
                         h0re_ref, h0im_ref, hist_ref,
                         q_ref, ossm_ref, opool_ref, hre_ref, him_ref, nhist_ref,
                         bu_re_ref, bu_im_ref):
    h = _rms(x_ref[...], g_ref[...]).astype(BF16)
    proj = _dot(h, win_ref[...])
    u_ssm = proj[:, :BRANCH]
    u_pool = proj[:, BRANCH:2 * BRANCH]
    q_ref[...] = proj[:, 2 * BRANCH:]

    _ssm_input(u_ssm, bblk_ref, bu_re_ref, bu_im_ref)
    a_r = ar_ref[...]
    a_i = ai_ref[...]
    h0r = h0re_ref[...]
    h0i = h0im_ref[...]
    hre_ref[...] = bu_re_ref[...] + a_r * h0r - a_i * h0i
    him_ref[...] = bu_im_ref[...] + a_r * h0i + a_i * h0r
    y = _ssm_output(hre_ref, him_ref, u_ssm, cre_ref, cimn_ref, d_ref, wglu_ref, bglu_ref)
    ossm_ref[...] = y.astype(BF16)

    pooled = []
    for gi, w in enumerate(POOL_WINDOWS):
        sl = slice(gi * POOL_GROUP_WIDTH, (gi + 1) * POOL_GROUP_WIDTH)
        acc = u_pool[:, sl]
        for j in range(1, w):
            acc = acc + hist_ref[POOL_HIST - j, :, sl]
        cnt = float(min(PAST_LEN + 1, w))
        pooled.append(acc / cnt - u_pool[:, sl])
    opool_ref[...] = _pool_mix(pooled, poolw_ref, pscale_ref).astype(BF16)
    for j in range(POOL_HIST - 1):
        nhist_ref[j] = hist_ref[j + 1]
    nhist_ref[POOL_HIST - 1] = u_pool


def _sample_mixer(layer, x_s, p, h0_re, h0_im, hist_t):
    n = x_s.shape[0]
    full = lambda shape: pl.BlockSpec(shape, lambda i: (0,) * len(shape))
    in_specs = [
        full((n, D_MODEL)),
        _layer_spec((1, D_MODEL), layer),
        _layer_spec((D_MODEL, 3 * BRANCH), layer),
        _layer_spec((N_SSM_BLOCKS, LANES, 2 * BLOCK_STATE), layer),
        _layer_spec((N_SSM_BLOCKS, BLOCK_STATE, LANES), layer),
        _layer_spec((N_SSM_BLOCKS, BLOCK_STATE, LANES), layer),
        _layer_spec((1, N_STATE), layer),
        _layer_spec((1, N_STATE), layer),
        _layer_spec((1, BRANCH), layer),
        _layer_spec((BRANCH, BRANCH), layer),
        _layer_spec((1, BRANCH), layer),
        _layer_spec((len(POOL_WINDOWS), POOL_GROUP_WIDTH, POOL_GROUP_WIDTH), layer),
        _layer_spec((1, BRANCH), layer),
        _layer_spec((n, N_STATE), layer),
        _layer_spec((n, N_STATE), layer),
        _layer_spec((POOL_HIST, n, BRANCH), layer),
    ]
    out_shape = (
        jax.ShapeDtypeStruct((n, BRANCH), F32),
        jax.ShapeDtypeStruct((n, BRANCH), BF16),
        jax.ShapeDtypeStruct((n, BRANCH), BF16),
        jax.ShapeDtypeStruct((n, N_STATE), F32),
        jax.ShapeDtypeStruct((n, N_STATE), F32),
        jax.ShapeDtypeStruct((POOL_HIST, n, BRANCH), F32),
    )
    out_specs = tuple(full(s.shape) for s in out_shape)
    return pl.pallas_call(
        _sample_mixer_kernel,
        grid=(1,),
        in_specs=in_specs, out_specs=out_specs, out_shape=out_shape,
        scratch_shapes=[pltpu.VMEM((n, N_STATE), F32), pltpu.VMEM((n, N_STATE), F32)],
        compiler_params=pltpu.CompilerParams(
            dimension_semantics=("arbitrary",), vmem_limit_bytes=VMEM_LIMIT),
        name="sample_mixer",
    )(x_s, p['g_mix_pre'], p['w_in_a'], p['bblk'], p['cre'], p['cimn'], p['ar'], p['ai'],
      p['ssm_d'], p['w_glu'], p['b_glu'], p['pool_w'], p['pool_scale'], h0_re, h0_im, hist_t)


def _sample_attention_kernel(q_ref, k_ref, v_ref, ones_ref, o_ref):
    scale = MEM_HEAD_DIM ** -0.5
    for b in range(SAMPLE_BLOCK):
        qb = q_ref[pl.ds(b, 1), :]
        prod = (k_ref[b] * qb).astype(BF16)
        s = _dot(prod, ones_ref[...]) * scale
        e = jnp.exp(s - jnp.max(s, axis=0, keepdims=True))
        prob = e / jnp.sum(e, axis=0, keepdims=True)
        o_ref[pl.ds(b, 1), :] = jnp.sum(prob * v_ref[b], axis=0, keepdims=True).astype(BF16)


def _sample_attention(layer, q, k_cache, v_cache, head_ones):
    n = q.shape[0]
    kv_spec = pl.BlockSpec((None, SAMPLE_BLOCK, N_MEM, BRANCH), lambda i: (layer, i, 0, 0))
    q_spec = pl.BlockSpec((SAMPLE_BLOCK, BRANCH), lambda i: (i, 0))
    return pl.pallas_call(
        _sample_attention_kernel,
        grid=(n // SAMPLE_BLOCK,),
        in_specs=[q_spec, kv_spec, kv_spec, _const_spec((BRANCH, BRANCH))],
        out_specs=q_spec,
        out_shape=jax.ShapeDtypeStruct((n, BRANCH), BF16),
        compiler_params=pltpu.CompilerParams(
            dimension_semantics=("parallel",), vmem_limit_bytes=VMEM_LIMIT),
        name="sample_attention",
    )(q, k_cache, v_cache, head_ones)


def kernel(x_prompt, x_sample, mem_prompt, cache_mem_k, cache_mem_v, state_ssm_re, state_ssm_im, state_pool, g_mix_pre, g_mix_post, g_ffn_pre, g_ffn_post, g_mem, w_in, w_kv, ssm_lam_re, ssm_lam_im, ssm_log_dt, ssm_b_re, ssm_b_im, ssm_c_re, ssm_c_im, ssm_d, ssm_w_glu, ssm_b_glu, pool_w, pool_scale, w_branch_up, w_out, w_ffn_in, w_ffn_out):
    ar, ai, bblk = _discretise(ssm_lam_re, ssm_lam_im, ssm_log_dt, ssm_b_re, ssm_b_im)
    vec = lambda a: a.reshape(DEPTH, 1, a.shape[-1])
    p = {
        'g_mix_pre': vec(g_mix_pre), 'g_mix_post': vec(g_mix_post),
        'g_ffn_pre': vec(g_ffn_pre), 'g_ffn_post': vec(g_ffn_post),
        'w_in_a': w_in[:, :, :3 * BRANCH].astype(BF16),
        'w_gate': w_in[:, :, 3 * BRANCH:].astype(BF16),
        'bblk': bblk, 'ar': ar, 'ai': ai,
        'cre': _c_blocks(ssm_c_re).astype(BF16),
        'cimn': _c_blocks(-ssm_c_im).astype(BF16),
        'ssm_d': vec(ssm_d), 'w_glu': ssm_w_glu.astype(BF16), 'b_glu': vec(ssm_b_glu),
        'pool_w': pool_w.astype(BF16), 'pool_scale': vec(pool_scale),
        'w_up': w_branch_up.astype(BF16), 'w_out': w_out.astype(BF16),
        'w_ffn_in': w_ffn_in.astype(BF16), 'w_ffn_out': w_ffn_out.astype(BF16),
    }
    k_mem, v_mem = _mem_kv(mem_prompt, vec(g_mem), w_kv.astype(BF16))

    head = jnp.arange(BRANCH, dtype=jnp.int32) // MEM_HEAD_DIM
    head_ones = (head[:, None] == head[None, :]).astype(BF16)
    k_cache = cache_mem_k.reshape(DEPTH, DEC_BATCH, N_MEM, BRANCH)
    v_cache = cache_mem_v.reshape(DEPTH, DEC_BATCH, N_MEM, BRANCH)
    h0_re = state_ssm_re.reshape(DEPTH, DEC_BATCH, N_STATE)
    h0_im = state_ssm_im.reshape(DEPTH, DEC_BATCH, N_STATE)
    hist_t = jnp.swapaxes(state_pool, 1, 2)

    xp = jnp.swapaxes(x_prompt, 0, 1).reshape(SEQ * BATCH, D_MODEL)
    xs = x_sample.reshape(DEC_BATCH, D_MODEL)
    re_p, im_p, pool_p, re_s, im_s, pool_s = [], [], [], [], [], []
    for layer in range(DEPTH):
        q_tm, o_ssm, o_pool, hre, him, hist = _mixer_a(layer, xp, p)
        q_bm = jnp.swapaxes(q_tm.reshape(SEQ, BATCH, BRANCH), 0, 1)
        o_mem = _attention(layer, q_bm, k_mem, v_mem)
        o_mem = jnp.swapaxes(o_mem, 0, 1).reshape(SEQ * BATCH, BRANCH)
        xp = _merge_ffn(layer, xp, o_ssm, o_pool, o_mem, p)
        re_p.append(hre.reshape(BATCH, SSM_GROUPS, SSM_STATE))
        im_p.append(him.reshape(BATCH, SSM_GROUPS, SSM_STATE))
        pool_p.append(jnp.swapaxes(hist.reshape(POOL_HIST, BATCH, BRANCH), 0, 1))

        qs, os_ssm, os_pool, hre, him, nhist = _sample_mixer(layer, xs, p, h0_re, h0_im, hist_t)
        os_mem = _sample_attention(layer, qs, k_cache, v_cache, head_ones)
        xs = _merge_ffn(layer, xs, os_ssm, os_pool, os_mem, p)
        re_s.append(hre.reshape(DEC_BATCH, SSM_GROUPS, SSM_STATE))
        im_s.append(him.reshape(DEC_BATCH, SSM_GROUPS, SSM_STATE))
        pool_s.append(jnp.swapaxes(nhist, 0, 1))

    y_prompt = jnp.swapaxes(xp.reshape(SEQ, BATCH, D_MODEL), 0, 1)
    y_sample = xs.reshape(DEC_BATCH, 1, D_MODEL)
    kv_shape = (DEPTH, BATCH, N_MEM, MEM_HEADS, MEM_HEAD_DIM)
    return (y_prompt, y_sample,
            jnp.stack(re_p), jnp.stack(im_p), jnp.stack(pool_p),
            k_mem.reshape(kv_shape), v_mem.reshape(kv_shape),
            jnp.stack(re_s), jnp.stack(im_s), jnp.stack(pool_s))
```

```python
import functools
import math

import jax
import jax.numpy as jnp
from jax import lax
from jax.experimental import pallas as pl
from jax.experimental.pallas import tpu as pltpu

D_MODEL = 1024
BATCH = 8
SEQ = 2048
DEPTH = 4
DEC_BATCH = 128
PAST_LEN = 16384

BRANCH = D_MODEL // 2
SSM_GROUP = 16
SSM_GROUPS = BRANCH // SSM_GROUP
SSM_STATE = 64
N_STATE = SSM_GROUPS * SSM_STATE
POOL_WINDOWS = (2, 4, 8, 16)
POOL_GROUP_WIDTH = BRANCH // len(POOL_WINDOWS)
POOL_HIST = max(POOL_WINDOWS) - 1
N_MEM = 256
MEM_HEADS = 4
MEM_HEAD_DIM = BRANCH // MEM_HEADS
KV_ROWS = N_MEM * MEM_HEADS
N_BRANCH = 3
D_FF = 2816
RMS_EPS = 1e-6

LANES = 128
SUBLANES = 8
GROUPS_PER_BLOCK = LANES // SSM_GROUP
N_SSM_BLOCKS = BRANCH // LANES
BLOCK_STATE = GROUPS_PER_BLOCK * SSM_STATE

ROWS_A = 512
STEPS_A = ROWS_A // BATCH
ROWS_C = 512
TQ_ATT = 1024
SAMPLE_BLOCK = 8
SCAN_LANES = 1024
FF_CHUNKS = ((0, 1536), (1536, 2816))
VMEM_LIMIT = 56 * 1024 * 1024

BF16 = jnp.bfloat16
F32 = jnp.float32


def _rms(x, g):
    ms = jnp.mean(x * x, axis=-1, keepdims=True)
    return x * lax.rsqrt(ms + RMS_EPS) * g


def _dot(a, b):
    return jnp.dot(a, b, preferred_element_type=F32)


def _const_spec(shape):
    nd = len(shape)
    return pl.BlockSpec(shape, lambda *_: (0,) * nd, pipeline_mode=pl.Buffered(1))


def _layer_spec(shape, layer):
    nd = len(shape)
    return pl.BlockSpec((None,) + tuple(shape), lambda *_: (layer,) + (0,) * nd,
                        pipeline_mode=pl.Buffered(1))


def _discretise_kernel(lr_ref, li_ref, ldt_ref, br_ref, bi_ref,
                       ar_ref, ai_ref, bbr_ref, bbi_ref):
    lr = lr_ref[...]
    li = li_ref[...]
    dt = jnp.exp(ldt_ref[...])
    zr = lr * dt
    zi = li * dt
    mag = jnp.exp(zr)
    ar = mag * jnp.cos(zi)
    ai = mag * jnp.sin(zi)
    den = lr * lr + li * li
    fr = ((ar - 1.0) * lr + ai * li) / den
    fi = (ai * lr - (ar - 1.0) * li) / den
    br = br_ref[...]
    bi = bi_ref[...]
    ar_ref[...] = ar
    ai_ref[...] = ai
    bbr_ref[...] = fr * br - fi * bi
    bbi_ref[...] = fr * bi + fi * br


def _discretise(lam_re, lam_im, log_dt, b_re, b_im):
    rows = DEPTH * SSM_GROUPS
    width = SSM_GROUP * SSM_STATE

    def tile_p(a):
        return jnp.tile(a.reshape(rows, 1, SSM_STATE), (1, SSM_GROUP, 1)).reshape(rows, width)

    def b_t(a):
        return jnp.swapaxes(a, -1, -2).reshape(rows, width)

    ldt = jnp.broadcast_to(log_dt.reshape(rows, 1), (rows, width))
    out = jax.ShapeDtypeStruct((rows, width), F32)
    ar, ai, bbr, bbi = pl.pallas_call(
        _discretise_kernel, out_shape=(out, out, out, out), name="s5_discretise",
    )(tile_p(lam_re), tile_p(lam_im), ldt, b_t(b_re), b_t(b_im))
    ar = ar[:, :SSM_STATE].reshape(DEPTH, 1, N_STATE)
    ai = ai[:, :SSM_STATE].reshape(DEPTH, 1, N_STATE)
    shape5 = (DEPTH, N_SSM_BLOCKS, GROUPS_PER_BLOCK, SSM_GROUP, SSM_STATE)
    eye = jnp.eye(GROUPS_PER_BLOCK, dtype=F32)

    def b_blocks(bb):
        m = jnp.einsum('lkghp,gG->lkghGp', bb.reshape(shape5), eye)
        return m.reshape(DEPTH, N_SSM_BLOCKS, LANES, BLOCK_STATE)

    bblk = jnp.concatenate([b_blocks(bbr), b_blocks(bbi)], axis=-1).astype(BF16)
    return ar, ai, bblk


def _c_blocks(c):
    shape5 = (DEPTH, N_SSM_BLOCKS, GROUPS_PER_BLOCK, SSM_GROUP, SSM_STATE)
    eye = jnp.eye(GROUPS_PER_BLOCK, dtype=F32)
    m = jnp.einsum('lkghp,gG->lkgpGh', c.reshape(shape5), eye)
    return m.reshape(DEPTH, N_SSM_BLOCKS, BLOCK_STATE, LANES)


def _ssm_input(u_ssm, bblk_ref, bu_re_ref, bu_im_ref):
    for k in range(N_SSM_BLOCKS):
        uk = u_ssm[:, k * LANES:(k + 1) * LANES].astype(BF16)
        bu = _dot(uk, bblk_ref[k])
        sl = slice(k * BLOCK_STATE, (k + 1) * BLOCK_STATE)
        bu_re_ref[:, sl] = bu[:, :BLOCK_STATE]
        bu_im_ref[:, sl] = bu[:, BLOCK_STATE:]


def _ssm_output(h_re_ref, h_im_ref, u_ssm, cre_ref, cimn_ref, d_ref, wglu_ref, bglu_ref):
    ys = []
    for k in range(N_SSM_BLOCKS):
        sl = slice(k * BLOCK_STATE, (k + 1) * BLOCK_STATE)
        ys.append(_dot(h_re_ref[:, sl].astype(BF16), cre_ref[k])
                  + _dot(h_im_ref[:, sl].astype(BF16), cimn_ref[k]))
    y = jnp.concatenate(ys, axis=-1) + d_ref[...] * u_ssm
    y = jax.nn.gelu(y)
    return y * jax.nn.sigmoid(_dot(y.astype(BF16), wglu_ref[...]) + bglu_ref[...])


def _pool_mix(pooled, poolw_ref, pscale_ref):
    outs = []
    for gi in range(len(POOL_WINDOWS)):
        sl = slice(gi * POOL_GROUP_WIDTH, (gi + 1) * POOL_GROUP_WIDTH)
        outs.append(_dot(pooled[gi].astype(BF16), poolw_ref[gi]) * pscale_ref[:, sl])
    return jnp.concatenate(outs, axis=-1)


def _mixer_a_kernel(x_ref, g_ref, win_ref, bblk_ref, cre_ref, cimn_ref, ar_ref, ai_ref,
                    d_ref, wglu_ref, bglu_ref, poolw_ref, pscale_ref,
                    q_ref, ossm_ref, opool_ref, hre_ref, him_ref, hist_ref,
                    bu_re_ref, bu_im_ref, pool_buf, ussm_tm, ossm_tm, opool_tm):
    i = pl.program_id(0)
    hist_rows = POOL_HIST * BATCH

    @pl.when(i == 0)
    def _():
        hre_ref[...] = jnp.zeros_like(hre_ref)
        him_ref[...] = jnp.zeros_like(him_ref)
        pool_buf[:, 0:hist_rows, :] = jnp.zeros((N_SSM_BLOCKS, hist_rows, LANES), F32)

    x = x_ref[...].reshape(ROWS_A, D_MODEL)
    h = _rms(x, g_ref[...]).astype(BF16)
    proj = _dot(h, win_ref[...])
    for b in range(BATCH):
        rows = slice(b * STEPS_A, (b + 1) * STEPS_A)
        q_ref[b] = proj[rows, 2 * BRANCH:].astype(BF16)
        for k in range(N_SSM_BLOCKS):
            ussm_tm[k, pl.ds(b, STEPS_A, stride=BATCH), :] = proj[rows, k * LANES:(k + 1) * LANES]
            pool_buf[k, pl.ds(hist_rows + b, STEPS_A, stride=BATCH), :] = (
                proj[rows, BRANCH + k * LANES:BRANCH + (k + 1) * LANES])
    u_ssm = jnp.concatenate([ussm_tm[k] for k in range(N_SSM_BLOCKS)], axis=-1)

    _ssm_input(u_ssm, bblk_ref, bu_re_ref, bu_im_ref)
    for c in range(N_STATE // SCAN_LANES):
        sl = slice(c * SCAN_LANES, (c + 1) * SCAN_LANES)
        a_r = jnp.broadcast_to(ar_ref[:, sl], (BATCH, SCAN_LANES))
        a_i = jnp.broadcast_to(ai_ref[:, sl], (BATCH, SCAN_LANES))

        def step(t, carry, sl=sl, a_r=a_r, a_i=a_i):
            hr, hi = carry
            rows = pl.ds(pl.multiple_of(t * BATCH, BATCH), BATCH)
            nhr = a_r * hr - a_i * hi + bu_re_ref[rows, sl]
            nhi = a_r * hi + a_i * hr + bu_im_ref[rows, sl]
            bu_re_ref[rows, sl] = nhr
            bu_im_ref[rows, sl] = nhi
            return nhr, nhi

        hr, hi = lax.fori_loop(0, STEPS_A, step, (hre_ref[:, sl], him_ref[:, sl]), unroll=4)
        hre_ref[:, sl] = hr
        him_ref[:, sl] = hi
    o_ssm = _ssm_output(bu_re_ref, bu_im_ref, u_ssm, cre_ref, cimn_ref, d_ref, wglu_ref, bglu_ref)

    t_pos = i * STEPS_A + lax.broadcasted_iota(jnp.int32, (ROWS_A, POOL_GROUP_WIDTH), 0) // BATCH
    pooled = []
    for gi, w in enumerate(POOL_WINDOWS):
        u_g = pool_buf[gi, hist_rows:hist_rows + ROWS_A, :]
        acc = u_g
        for j in range(1, w):
            start = hist_rows - j * BATCH
            acc = acc + pool_buf[gi, start:start + ROWS_A, :]
        cnt = jnp.minimum(t_pos + 1, w).astype(F32)
        pooled.append(acc / cnt - u_g)
    o_pool = _pool_mix(pooled, poolw_ref, pscale_ref)
    pool_buf[:, 0:hist_rows, :] = pool_buf[:, ROWS_A:ROWS_A + hist_rows, :]

    for k in range(N_SSM_BLOCKS):
        ossm_tm[k] = o_ssm[:, k * LANES:(k + 1) * LANES]
        opool_tm[k] = o_pool[:, k * LANES:(k + 1) * LANES]
    for b in range(BATCH):
        rows = pl.ds(b, STEPS_A, stride=BATCH)
        ossm_ref[b] = jnp.concatenate(
            [ossm_tm[k, rows, :] for k in range(N_SSM_BLOCKS)], axis=-1).astype(BF16)
        opool_ref[b] = jnp.concatenate(
            [opool_tm[k, rows, :] for k in range(N_SSM_BLOCKS)], axis=-1).astype(BF16)

    @pl.when(i == pl.num_programs(0) - 1)
    def _():
        for k in range(N_SSM_BLOCKS):
            hist_ref[:, k * LANES:(k + 1) * LANES] = pool_buf[k, 0:hist_rows, :]


def _mixer_a(layer, x, p):
    hist_rows = POOL_HIST * BATCH
    seq_spec = lambda w: pl.BlockSpec((BATCH, STEPS_A, w), lambda i: (0, i, 0))
    in_specs = [
        seq_spec(D_MODEL),
        _layer_spec((1, D_MODEL), layer),
        _layer_spec((D_MODEL, 3 * BRANCH), layer),
        _layer_spec((N_SSM_BLOCKS, LANES, 2 * BLOCK_STATE), layer),
        _layer_spec((N_SSM_BLOCKS, BLOCK_STATE, LANES), layer),
        _layer_spec((N_SSM_BLOCKS, BLOCK_STATE, LANES), layer),
        _layer_spec((1, N_STATE), layer),
        _layer_spec((1, N_STATE), layer),
        _layer_spec((1, BRANCH), layer),
        _layer_spec((BRANCH, BRANCH), layer),
        _layer_spec((1, BRANCH), layer),
        _layer_spec((len(POOL_WINDOWS), POOL_GROUP_WIDTH, POOL_GROUP_WIDTH), layer),
        _layer_spec((1, BRANCH), layer),
    ]
    branch_out = jax.ShapeDtypeStruct((BATCH, SEQ, BRANCH), BF16)
    out_shape = (
        branch_out, branch_out, branch_out,
        jax.ShapeDtypeStruct((BATCH, N_STATE), F32),
        jax.ShapeDtypeStruct((BATCH, N_STATE), F32),
        jax.ShapeDtypeStruct((hist_rows, BRANCH), F32),
    )
    out_specs = (
        seq_spec(BRANCH), seq_spec(BRANCH), seq_spec(BRANCH),
        pl.BlockSpec((BATCH, N_STATE), lambda i: (0, 0)),
        pl.BlockSpec((BATCH, N_STATE), lambda i: (0, 0)),
        pl.BlockSpec((hist_rows, BRANCH), lambda i: (0, 0)),
    )
    return pl.pallas_call(
        _mixer_a_kernel,
        grid=(SEQ // STEPS_A,),
        in_specs=in_specs, out_specs=out_specs, out_shape=out_shape,
        scratch_shapes=[
            pltpu.VMEM((ROWS_A, N_STATE), F32),
            pltpu.VMEM((ROWS_A, N_STATE), F32),
            pltpu.VMEM((N_SSM_BLOCKS, hist_rows + ROWS_A, LANES), F32),
            pltpu.VMEM((N_SSM_BLOCKS, ROWS_A, LANES), F32),
            pltpu.VMEM((N_SSM_BLOCKS, ROWS_A, LANES), F32),
            pltpu.VMEM((N_SSM_BLOCKS, ROWS_A, LANES), F32),
        ],
        compiler_params=pltpu.CompilerParams(
            dimension_semantics=("arbitrary",), vmem_limit_bytes=VMEM_LIMIT),
        name="prompt_mixer_a",
    )(x, p['g_mix_pre'], p['w_in_a'], p['bblk'], p['cre'], p['cimn'], p['ar'], p['ai'],
      p['ssm_d'], p['w_glu'], p['b_glu'], p['pool_w'], p['pool_scale'])


def _mem_kv_kernel(mem_ref, g_ref, wkv_ref, k_ref, v_ref):
    kv = _dot(_rms(mem_ref[...], g_ref[...]).astype(BF16), wkv_ref[...])
    k_ref[...] = kv[:, :BRANCH]
    v_ref[...] = kv[:, BRANCH:]


def _mem_kv(mem, g_mem, w_kv):
    out = jax.ShapeDtypeStruct((DEPTH, BATCH, N_MEM, BRANCH), F32)
    out_spec = pl.BlockSpec((None, None, N_MEM, BRANCH), lambda l, b: (l, b, 0, 0))
    return pl.pallas_call(
        _mem_kv_kernel,
        grid=(DEPTH, BATCH),
        in_specs=[
            pl.BlockSpec((None, N_MEM, D_MODEL), lambda l, b: (b, 0, 0)),
            pl.BlockSpec((None, 1, D_MODEL), lambda l, b: (l, 0, 0)),
            pl.BlockSpec((None, D_MODEL, 2 * BRANCH), lambda l, b: (l, 0, 0)),
        ],
        out_specs=(out_spec, out_spec), out_shape=(out, out),
        compiler_params=pltpu.CompilerParams(dimension_semantics=("arbitrary", "arbitrary")),
        name="mem_kv",
    )(mem, g_mem, w_kv)


def _attention_kernel(q_ref, k_ref, v_ref, o_ref):
    scale = MEM_HEAD_DIM ** -0.5
    for hd in range(MEM_HEADS):
        sl = slice(hd * MEM_HEAD_DIM, (hd + 1) * MEM_HEAD_DIM)
        s = lax.dot_general(q_ref[:, sl], k_ref[:, sl].astype(BF16),
                            (((1,), (1,)), ((), ())), preferred_element_type=F32) * scale
        e = jnp.exp(s - jnp.max(s, axis=-1, keepdims=True))
        prob = e / jnp.sum(e, axis=-1, keepdims=True)
        o_ref[:, sl] = _dot(prob.astype(BF16), v_ref[:, sl].astype(BF16)).astype(BF16)


def _attention(layer, q_bm, k_all, v_all):
    kv_spec = pl.BlockSpec((None, None, N_MEM, BRANCH), lambda b, t: (layer, b, 0, 0))
    q_spec = pl.BlockSpec((None, TQ_ATT, BRANCH), lambda b, t: (b, t, 0))
    return pl.pallas_call(
        _attention_kernel,
        grid=(BATCH, SEQ // TQ_ATT),
        in_specs=[q_spec, kv_spec, kv_spec],
        out_specs=q_spec,
        out_shape=jax.ShapeDtypeStruct((BATCH, SEQ, BRANCH), BF16),
        compiler_params=pltpu.CompilerParams(dimension_semantics=("parallel", "parallel")),
        name="prompt_attention",
    )(q_bm, k_all, v_all)


def _merge_ffn_kernel(x_ref, ossm_ref, opool_ref, omem_ref, gpre_ref, wgate_ref, wup_ref,
                      wout_ref, gpost_ref, gfpre_ref, wfin_ref, wfout_ref, gfpost_ref, y_ref):
    x = x_ref[...]
    h = _rms(x, gpre_ref[...]).astype(BF16)
    merged = None
    for b, o_ref in enumerate((ossm_ref, opool_ref, omem_ref)):
        gate = jax.nn.sigmoid(_dot(h, wgate_ref[:, b * D_MODEL:(b + 1) * D_MODEL]))
        term = gate * _dot(o_ref[...], wup_ref[b])
        merged = term if merged is None else merged + term
    x = x + _rms(_dot(merged.astype(BF16), wout_ref[...]), gpost_ref[...])
    hf = _rms(x, gfpre_ref[...]).astype(BF16)
    f = None
    for lo, hi in FF_CHUNKS:
        hg = _dot(hf, wfin_ref[:, lo:hi])
        hu = _dot(hf, wfin_ref[:, D_FF + lo:D_FF + hi])
        part = _dot((jax.nn.silu(hg) * hu).astype(BF16), wfout_ref[lo:hi, :])
        f = part if f is None else f + part
    y_ref[...] = x + _rms(f, gfpost_ref[...])


def _merge_ffn(layer, x, o_ssm, o_pool, o_mem, p):
    n_rows = x.shape[0]
    rows = min(ROWS_C, n_rows)
    row_spec = lambda w: pl.BlockSpec((rows, w), lambda i: (i, 0))
    in_specs = [
        row_spec(D_MODEL), row_spec(BRANCH), row_spec(BRANCH), row_spec(BRANCH),
        _layer_spec((1, D_MODEL), layer),
        _layer_spec((D_MODEL, N_BRANCH * D_MODEL), layer),
        _layer_spec((N_BRANCH, BRANCH, D_MODEL), layer),
        _layer_spec((D_MODEL, D_MODEL), layer),
        _layer_spec((1, D_MODEL), layer),
        _layer_spec((1, D_MODEL), layer),
        _layer_spec((D_MODEL, 2 * D_FF), layer),
        _layer_spec((D_FF, D_MODEL), layer),
        _layer_spec((1, D_MODEL), layer),
    ]
    return pl.pallas_call(
        _merge_ffn_kernel,
        grid=(n_rows // rows,),
        in_specs=in_specs, out_specs=row_spec(D_MODEL),
        out_shape=jax.ShapeDtypeStruct((n_rows, D_MODEL), F32),
        compiler_params=pltpu.CompilerParams(
            dimension_semantics=("parallel",), vmem_limit_bytes=VMEM_LIMIT),
        name="merge_ffn",
    )(x, o_ssm, o_pool, o_mem, p['g_mix_pre'], p['w_gate'], p['w_up'], p['w_out'],
      p['g_mix_post'], p['g_ffn_pre'], p['w_ffn_in'], p['w_ffn_out'], p['g_ffn_post'])


def _sample_mixer_kernel(x_ref, g_ref, win_ref, bblk_ref, cre_ref, cimn_ref, ar_ref, ai_ref,
                         d_ref, wglu_ref, bglu_ref, poolw_ref, pscale_ref,
                         h0re_ref, h0im_ref, hist_ref,
                         q_ref, ossm_ref, opool_ref, hre_ref, him_ref, nhist_ref,
                         bu_re_ref, bu_im_ref):
    h = _rms(x_ref[...], g_ref[...]).astype(BF16)
    proj = _dot(h, win_ref[...])
    u_ssm = proj[:, :BRANCH]
    u_pool = proj[:, BRANCH:2 * BRANCH]
    q_ref[...] = proj[:, 2 * BRANCH:]

    _ssm_input(u_ssm, bblk_ref, bu_re_ref, bu_im_ref)
    a_r = ar_ref[...]
    a_i = ai_ref[...]
    h0r = h0re_ref[...]
    h0i = h0im_ref[...]
    hre_ref[...] = bu_re_ref[...] + a_r * h0r - a_i * h0i
    him_ref[...] = bu_im_ref[...] + a_r * h0i + a_i * h0r
    y = _ssm_output(hre_ref, him_ref, u_ssm, cre_ref, cimn_ref, d_ref, wglu_ref, bglu_ref)
    ossm_ref[...] = y.astype(BF16)

    pooled = []
    for gi, w in enumerate(POOL_WINDOWS):
        sl = slice(gi * POOL_GROUP_WIDTH, (gi + 1) * POOL_GROUP_WIDTH)
        acc = u_pool[:, sl]
        for j in range(1, w):
            acc = acc + hist_ref[POOL_HIST - j, :, sl]
        cnt = float(min(PAST_LEN + 1, w))
        pooled.append(acc / cnt - u_pool[:, sl])
    opool_ref[...] = _pool_mix(pooled, poolw_ref, pscale_ref).astype(BF16)
    for j in range(POOL_HIST - 1):
        nhist_ref[j] = hist_ref[j + 1]
    nhist_ref[POOL_HIST - 1] = u_pool


def _sample_mixer(layer, x_s, p, h0_re, h0_im, hist_t):
    n = x_s.shape[0]
    full = lambda shape: pl.BlockSpec(shape, lambda i: (0,) * len(shape))
    in_specs = [
        full((n, D_MODEL)),
        _layer_spec((1, D_MODEL), layer),
        _layer_spec((D_MODEL, 3 * BRANCH), layer),
        _layer_spec((N_SSM_BLOCKS, LANES, 2 * BLOCK_STATE), layer),
        _layer_spec((N_SSM_BLOCKS, BLOCK_STATE, LANES), layer),
        _layer_spec((N_SSM_BLOCKS, BLOCK_STATE, LANES), layer),
        _layer_spec((1, N_STATE), layer),
        _layer_spec((1, N_STATE), layer),
        _layer_spec((1, BRANCH), layer),
        _layer_spec((BRANCH, BRANCH), layer),
        _layer_spec((1, BRANCH), layer),
        _layer_spec((len(POOL_WINDOWS), POOL_GROUP_WIDTH, POOL_GROUP_WIDTH), layer),
        _layer_spec((1, BRANCH), layer),
        _layer_spec((n, N_STATE), layer),
        _layer_spec((n, N_STATE), layer),
        _layer_spec((POOL_HIST, n, BRANCH), layer),
    ]
    out_shape = (
        jax.ShapeDtypeStruct((n, BRANCH), F32),
        jax.ShapeDtypeStruct((n, BRANCH), BF16),
        jax.ShapeDtypeStruct((n, BRANCH), BF16),
        jax.ShapeDtypeStruct((n, N_STATE), F32),
        jax.ShapeDtypeStruct((n, N_STATE), F32),
        jax.ShapeDtypeStruct((POOL_HIST, n, BRANCH), F32),
    )
    out_specs = tuple(full(s.shape) for s in out_shape)
    return pl.pallas_call(
        _sample_mixer_kernel,
        grid=(1,),
        in_specs=in_specs, out_specs=out_specs, out_shape=out_shape,
        scratch_shapes=[pltpu.VMEM((n, N_STATE), F32), pltpu.VMEM((n, N_STATE), F32)],
        compiler_params=pltpu.CompilerParams(
            dimension_semantics=("arbitrary",), vmem_limit_bytes=VMEM_LIMIT),
        name="sample_mixer",
    )(x_s, p['g_mix_pre'], p['w_in_a'], p['bblk'], p['cre'], p['cimn'], p['ar'], p['ai'],
      p['ssm_d'], p['w_glu'], p['b_glu'], p['pool_w'], p['pool_scale'], h0_re, h0_im, hist_t)


def _sample_attention_kernel(q_ref, k_ref, v_ref, ones_ref, o_ref):
    tiles = KV_ROWS // SUBLANES
    scale = MEM_HEAD_DIM ** -0.5

    def one_sample(b, carry):
        q8 = q_ref[b] * scale
        prod = k_ref[b].reshape(tiles, SUBLANES, MEM_HEAD_DIM) * q8[None]
        prod = prod.reshape(KV_ROWS, MEM_HEAD_DIM).astype(BF16)
        s = _dot(prod, ones_ref[...]).reshape(tiles, SUBLANES, MEM_HEAD_DIM)
        m8 = jnp.max(s, axis=0)
        m8 = jnp.maximum(m8, pltpu.roll(m8, MEM_HEADS, 0))
        e = jnp.exp(s - m8[None])
        l8 = jnp.sum(e, axis=0)
        acc = jnp.sum(e * v_ref[b].reshape(tiles, SUBLANES, MEM_HEAD_DIM), axis=0)
        l8 = l8 + pltpu.roll(l8, MEM_HEADS, 0)
        acc = acc + pltpu.roll(acc, MEM_HEADS, 0)
        o_ref[b] = acc / l8
        return carry

    lax.fori_loop(0, SAMPLE_BLOCK, one_sample, 0, unroll=True)


def _sample_attention(layer, q8, k_cache, v_cache, lane_ones):
    n = q8.shape[0]
    kv_spec = pl.BlockSpec((None, SAMPLE_BLOCK, KV_ROWS, MEM_HEAD_DIM), lambda i: (layer, i, 0, 0))
    q_spec = pl.BlockSpec((SAMPLE_BLOCK, SUBLANES, MEM_HEAD_DIM), lambda i: (i, 0, 0))
    return pl.pallas_call(
        _sample_attention_kernel,
        grid=(n // SAMPLE_BLOCK,),
        in_specs=[q_spec, kv_spec, kv_spec, _const_spec((MEM_HEAD_DIM, MEM_HEAD_DIM))],
        out_specs=q_spec,
        out_shape=jax.ShapeDtypeStruct((n, SUBLANES, MEM_HEAD_DIM), F32),
        compiler_params=pltpu.CompilerParams(
            dimension_semantics=("parallel",), vmem_limit_bytes=VMEM_LIMIT),
        name="sample_attention",
    )(q8, k_cache, v_cache, lane_ones)


def kernel(x_prompt, x_sample, mem_prompt, cache_mem_k, cache_mem_v, state_ssm_re, state_ssm_im, state_pool, g_mix_pre, g_mix_post, g_ffn_pre, g_ffn_post, g_mem, w_in, w_kv, ssm_lam_re, ssm_lam_im, ssm_log_dt, ssm_b_re, ssm_b_im, ssm_c_re, ssm_c_im, ssm_d, ssm_w_glu, ssm_b_glu, pool_w, pool_scale, w_branch_up, w_out, w_ffn_in, w_ffn_out):
    ar, ai, bblk = _discretise(ssm_lam_re, ssm_lam_im, ssm_log_dt, ssm_b_re, ssm_b_im)
    vec = lambda a: a.reshape(DEPTH, 1, a.shape[-1])
    p = {
        'g_mix_pre': vec(g_mix_pre), 'g_mix_post': vec(g_mix_post),
        'g_ffn_pre': vec(g_ffn_pre), 'g_ffn_post': vec(g_ffn_post),
        'w_in_a': w_in[:, :, :3 * BRANCH].astype(BF16),
        'w_gate': w_in[:, :, 3 * BRANCH:].astype(BF16),
        'bblk': bblk, 'ar': ar, 'ai': ai,
        'cre': _c_blocks(ssm_c_re).astype(BF16),
        'cimn': _c_blocks(-ssm_c_im).astype(BF16),
        'ssm_d': vec(ssm_d), 'w_glu': ssm_w_glu.astype(BF16), 'b_glu': vec(ssm_b_glu),
        'pool_w': pool_w.astype(BF16), 'pool_scale': vec(pool_scale),
        'w_up': w_branch_up.astype(BF16), 'w_out': w_out.astype(BF16),
        'w_ffn_in': w_ffn_in.astype(BF16), 'w_ffn_out': w_ffn_out.astype(BF16),
    }
    k_mem, v_mem = _mem_kv(mem_prompt, vec(g_mem), w_kv.astype(BF16))

    lane_ones = jnp.ones((MEM_HEAD_DIM, MEM_HEAD_DIM), BF16)
    k_cache = cache_mem_k.reshape(DEPTH, DEC_BATCH, KV_ROWS, MEM_HEAD_DIM)
    v_cache = cache_mem_v.reshape(DEPTH, DEC_BATCH, KV_ROWS, MEM_HEAD_DIM)
    h0_re = state_ssm_re.reshape(DEPTH, DEC_BATCH, N_STATE)
    h0_im = state_ssm_im.reshape(DEPTH, DEC_BATCH, N_STATE)
    hist_t = jnp.swapaxes(state_pool, 1, 2)

    xp = x_prompt
    xs = x_sample.reshape(DEC_BATCH, D_MODEL)
    n_prompt = BATCH * SEQ
    re_p, im_p, pool_p, re_s, im_s, pool_s = [], [], [], [], [], []
    for layer in range(DEPTH):
        q, o_ssm, o_pool, hre, him, hist = _mixer_a(layer, xp, p)
        o_mem = _attention(layer, q, k_mem, v_mem)
        xp = _merge_ffn(layer, xp.reshape(n_prompt, D_MODEL), o_ssm.reshape(n_prompt, BRANCH),
                        o_pool.reshape(n_prompt, BRANCH), o_mem.reshape(n_prompt, BRANCH), p)
        xp = xp.reshape(BATCH, SEQ, D_MODEL)
        re_p.append(hre.reshape(BATCH, SSM_GROUPS, SSM_STATE))
        im_p.append(him.reshape(BATCH, SSM_GROUPS, SSM_STATE))
        pool_p.append(jnp.swapaxes(hist.reshape(POOL_HIST, BATCH, BRANCH), 0, 1))

        qs, os_ssm, os_pool, hre, him, nhist = _sample_mixer(layer, xs, p, h0_re, h0_im, hist_t)
        qs4 = qs.reshape(DEC_BATCH, MEM_HEADS, MEM_HEAD_DIM)
        os_mem = _sample_attention(layer, jnp.concatenate([qs4, qs4], axis=1),
                                   k_cache, v_cache, lane_ones)
        os_mem = os_mem[:, :MEM_HEADS].reshape(DEC_BATCH, BRANCH).astype(BF16)
        xs = _merge_ffn(layer, xs, os_ssm, os_pool, os_mem, p)
        re_s.append(hre.reshape(DEC_BATCH, SSM_GROUPS, SSM_STATE))
        im_s.append(him.reshape(DEC_BATCH, SSM_GROUPS, SSM_STATE))
        pool_s.append(jnp.swapaxes(nhist, 0, 1))

    y_prompt = xp
    y_sample = xs.reshape(DEC_BATCH, 1, D_MODEL)
    kv_shape = (DEPTH, BATCH, N_MEM, MEM_HEADS, MEM_HEAD_DIM)
    return (y_prompt, y_sample,
            jnp.stack(re_p), jnp.stack(im_p), jnp.stack(pool_p),
            k_mem.reshape(kv_shape), v_mem.reshape(kv_shape),
            jnp.stack(re_s), jnp.stack(im_s), jnp.stack(pool_s))
```

```python
import functools
import math

import jax
import jax.numpy as jnp
from jax import lax
from jax.experimental import pallas as pl
from jax.experimental.pallas import tpu as pltpu

D_MODEL = 1024
BATCH = 8
SEQ = 2048
DEPTH = 4
DEC_BATCH = 128
PAST_LEN = 16384

BRANCH = D_MODEL // 2
SSM_GROUP = 16
SSM_GROUPS = BRANCH // SSM_GROUP
SSM_STATE = 64
N_STATE = SSM_GROUPS * SSM_STATE
POOL_WINDOWS = (2, 4, 8, 16)
POOL_GROUP_WIDTH = BRANCH // len(POOL_WINDOWS)
POOL_HIST = max(POOL_WINDOWS) - 1
N_MEM = 256
MEM_HEADS = 4
MEM_HEAD_DIM = BRANCH // MEM_HEADS
KV_ROWS = N_MEM * MEM_HEADS
N_BRANCH = 3
D_FF = 2816
RMS_EPS = 1e-6

LANES = 128
SUBLANES = 8
GROUPS_PER_BLOCK = LANES // SSM_GROUP
N_SSM_BLOCKS = BRANCH // LANES
BLOCK_STATE = GROUPS_PER_BLOCK * SSM_STATE

ROWS_A = 512
STEPS_A = ROWS_A // BATCH
ROWS_C = 512
SAMPLE_BLOCK = 8
FF_CHUNKS = ((0, 1536), (1536, 2816))
VMEM_LIMIT = 56 * 1024 * 1024

BF16 = jnp.bfloat16
F32 = jnp.float32


def _rms(x, g):
    ms = jnp.mean(x * x, axis=-1, keepdims=True)
    return x * lax.rsqrt(ms + RMS_EPS) * g


def _dot(a, b):
    return jnp.dot(a, b, preferred_element_type=F32)


def _const_spec(shape):
    nd = len(shape)
    return pl.BlockSpec(shape, lambda *_: (0,) * nd, pipeline_mode=pl.Buffered(1))


def _layer_spec(shape, layer):
    nd = len(shape)
    return pl.BlockSpec((None,) + tuple(shape), lambda *_: (layer,) + (0,) * nd,
                        pipeline_mode=pl.Buffered(1))


def _discretise_kernel(lr_ref, li_ref, ldt_ref, br_ref, bi_ref,
                       ar_ref, ai_ref, bbr_ref, bbi_ref):
    lr = lr_ref[...]
    li = li_ref[...]
    dt = jnp.exp(ldt_ref[...])
    zr = lr * dt
    zi = li * dt
    mag = jnp.exp(zr)
    ar = mag * jnp.cos(zi)
    ai = mag * jnp.sin(zi)
    den = lr * lr + li * li
    fr = ((ar - 1.0) * lr + ai * li) / den
    fi = (ai * lr - (ar - 1.0) * li) / den
    br = br_ref[...]
    bi = bi_ref[...]
    ar_ref[...] = ar
    ai_ref[...] = ai
    bbr_ref[...] = fr * br - fi * bi
    bbi_ref[...] = fr * bi + fi * br


def _discretise(lam_re, lam_im, log_dt, b_re, b_im):
    rows = DEPTH * SSM_GROUPS
    width = SSM_GROUP * SSM_STATE

    def tile_p(a):
        return jnp.tile(a.reshape(rows, 1, SSM_STATE), (1, SSM_GROUP, 1)).reshape(rows, width)

    def b_t(a):
        return jnp.swapaxes(a, -1, -2).reshape(rows, width)

    ldt = jnp.broadcast_to(log_dt.reshape(rows, 1), (rows, width))
    out = jax.ShapeDtypeStruct((rows, width), F32)
    ar, ai, bbr, bbi = pl.pallas_call(
        _discretise_kernel, out_shape=(out, out, out, out), name="s5_discretise",
    )(tile_p(lam_re), tile_p(lam_im), ldt, b_t(b_re), b_t(b_im))
    ar = ar[:, :SSM_STATE].reshape(DEPTH, 1, N_STATE)
    ai = ai[:, :SSM_STATE].reshape(DEPTH, 1, N_STATE)
    shape5 = (DEPTH, N_SSM_BLOCKS, GROUPS_PER_BLOCK, SSM_GROUP, SSM_STATE)
    eye = jnp.eye(GROUPS_PER_BLOCK, dtype=F32)

    def b_blocks(bb):
        m = jnp.einsum('lkghp,gG->lkghGp', bb.reshape(shape5), eye)
        return m.reshape(DEPTH, N_SSM_BLOCKS, LANES, BLOCK_STATE)

    bblk = jnp.concatenate([b_blocks(bbr), b_blocks(bbi)], axis=-1).astype(BF16)
    return ar, ai, bblk


def _c_blocks(c):
    shape5 = (DEPTH, N_SSM_BLOCKS, GROUPS_PER_BLOCK, SSM_GROUP, SSM_STATE)
    eye = jnp.eye(GROUPS_PER_BLOCK, dtype=F32)
    m = jnp.einsum('lkghp,gG->lkgpGh', c.reshape(shape5), eye)
    return m.reshape(DEPTH, N_SSM_BLOCKS, BLOCK_STATE, LANES)


def _ssm_input(u_block, bblk, bu_re_ref, bu_im_ref, sl):
    bu = _dot(u_block.astype(BF16), bblk)
    bu_re_ref[:, sl] = bu[:, :BLOCK_STATE]
    bu_im_ref[:, sl] = bu[:, BLOCK_STATE:]


def _ssm_readout(h_re_ref, h_im_ref, sl, cre, cimn):
    return _dot(h_re_ref[:, sl].astype(BF16), cre) + _dot(h_im_ref[:, sl].astype(BF16), cimn)


def _ssm_gate(y, u_ssm, d_ref, wglu_ref, bglu_ref):
    y = jax.nn.gelu(y + d_ref[...] * u_ssm)
    return y * jax.nn.sigmoid(_dot(y.astype(BF16), wglu_ref[...]) + bglu_ref[...])


def _pool_mix(pooled, poolw_ref, pscale_ref):
    outs = []
    for gi in range(len(POOL_WINDOWS)):
        sl = slice(gi * POOL_GROUP_WIDTH, (gi + 1) * POOL_GROUP_WIDTH)
        outs.append(_dot(pooled[gi].astype(BF16), poolw_ref[gi]) * pscale_ref[:, sl])
    return jnp.concatenate(outs, axis=-1)


def _mixer_a_kernel(x_ref, g_ref, win_ref, bblk_ref, cre_ref, cimn_ref, ar_ref, ai_ref,
                    d_ref, wglu_ref, bglu_ref, poolw_ref, pscale_ref,
                    q_ref, ossm_ref, opool_ref, hre_ref, him_ref, hist_ref,
                    bu_re_ref, bu_im_ref, pool_buf, ussm_tm, ossm_tm, opool_tm):
    i = pl.program_id(0)
    hist_rows = POOL_HIST * BATCH

    @pl.when(i == 0)
    def _():
        hre_ref[...] = jnp.zeros_like(hre_ref)
        him_ref[...] = jnp.zeros_like(him_ref)
        pool_buf[:, 0:hist_rows, :] = jnp.zeros((N_SSM_BLOCKS, hist_rows, LANES), F32)

    x = x_ref[...].reshape(ROWS_A, D_MODEL)
    h = _rms(x, g_ref[...]).astype(BF16)
    proj = _dot(h, win_ref[...])
    for b in range(BATCH):
        rows = slice(b * STEPS_A, (b + 1) * STEPS_A)
        q_ref[b] = proj[rows, 2 * BRANCH:].astype(BF16)
        for k in range(N_SSM_BLOCKS):
            ussm_tm[k, pl.ds(b, STEPS_A, stride=BATCH), :] = proj[rows, k * LANES:(k + 1) * LANES]
            pool_buf[k, pl.ds(hist_rows + b, STEPS_A, stride=BATCH), :] = (
                proj[rows, BRANCH + k * LANES:BRANCH + (k + 1) * LANES])
    u_ssm = jnp.concatenate([ussm_tm[k] for k in range(N_SSM_BLOCKS)], axis=-1)

    ys = []
    for k in range(N_SSM_BLOCKS):
        sl = slice(k * BLOCK_STATE, (k + 1) * BLOCK_STATE)
        _ssm_input(ussm_tm[k], bblk_ref[k], bu_re_ref, bu_im_ref, sl)
        a_r = jnp.broadcast_to(ar_ref[:, sl], (BATCH, BLOCK_STATE))
        a_i = jnp.broadcast_to(ai_ref[:, sl], (BATCH, BLOCK_STATE))
        hr = hre_ref[:, sl]
        hi = him_ref[:, sl]
        for t in range(STEPS_A):
            rows = slice(t * BATCH, (t + 1) * BATCH)
            hr, hi = (a_r * hr - a_i * hi + bu_re_ref[rows, sl],
                      a_r * hi + a_i * hr + bu_im_ref[rows, sl])
            bu_re_ref[rows, sl] = hr
            bu_im_ref[rows, sl] = hi
        hre_ref[:, sl] = hr
        him_ref[:, sl] = hi
        ys.append(_ssm_readout(bu_re_ref, bu_im_ref, sl, cre_ref[k], cimn_ref[k]))
    o_ssm = _ssm_gate(jnp.concatenate(ys, axis=-1), u_ssm, d_ref, wglu_ref, bglu_ref)

    t_pos = i * STEPS_A + lax.broadcasted_iota(jnp.int32, (ROWS_A, POOL_GROUP_WIDTH), 0) // BATCH
    pooled = []
    for gi, w in enumerate(POOL_WINDOWS):
        u_g = pool_buf[gi, hist_rows:hist_rows + ROWS_A, :]
        acc = u_g
        for j in range(1, w):
            start = hist_rows - j * BATCH
            acc = acc + pool_buf[gi, start:start + ROWS_A, :]
        cnt = jnp.minimum(t_pos + 1, w).astype(F32)
        pooled.append(acc / cnt - u_g)
    o_pool = _pool_mix(pooled, poolw_ref, pscale_ref)
    pool_buf[:, 0:hist_rows, :] = pool_buf[:, ROWS_A:ROWS_A + hist_rows, :]

    for k in range(N_SSM_BLOCKS):
        ossm_tm[k] = o_ssm[:, k * LANES:(k + 1) * LANES]
        opool_tm[k] = o_pool[:, k * LANES:(k + 1) * LANES]
    for b in range(BATCH):
        rows = pl.ds(b, STEPS_A, stride=BATCH)
        ossm_ref[b] = jnp.concatenate(
            [ossm_tm[k, rows, :] for k in range(N_SSM_BLOCKS)], axis=-1).astype(BF16)
        opool_ref[b] = jnp.concatenate(
            [opool_tm[k, rows, :] for k in range(N_SSM_BLOCKS)], axis=-1).astype(BF16)

    @pl.when(i == pl.num_programs(0) - 1)
    def _():
        for k in range(N_SSM_BLOCKS):
            hist_ref[:, k * LANES:(k + 1) * LANES] = pool_buf[k, 0:hist_rows, :]


def _mixer_a(layer, x, p):
    hist_rows = POOL_HIST * BATCH
    seq_spec = lambda w: pl.BlockSpec((BATCH, STEPS_A, w), lambda i: (0, i, 0))
    in_specs = [
        seq_spec(D_MODEL),
        _layer_spec((1, D_MODEL), layer),
        _layer_spec((D_MODEL, 3 * BRANCH), layer),
        _layer_spec((N_SSM_BLOCKS, LANES, 2 * BLOCK_STATE), layer),
        _layer_spec((N_SSM_BLOCKS, BLOCK_STATE, LANES), layer),
        _layer_spec((N_SSM_BLOCKS, BLOCK_STATE, LANES), layer),
        _layer_spec((1, N_STATE), layer),
        _layer_spec((1, N_STATE), layer),
        _layer_spec((1, BRANCH), layer),
        _layer_spec((BRANCH, BRANCH), layer),
        _layer_spec((1, BRANCH), layer),
        _layer_spec((len(POOL_WINDOWS), POOL_GROUP_WIDTH, POOL_GROUP_WIDTH), layer),
        _layer_spec((1, BRANCH), layer),
    ]
    branch_out = jax.ShapeDtypeStruct((BATCH, SEQ, BRANCH), BF16)
    out_shape = (
        branch_out, branch_out, branch_out,
        jax.ShapeDtypeStruct((BATCH, N_STATE), F32),
        jax.ShapeDtypeStruct((BATCH, N_STATE), F32),
        jax.ShapeDtypeStruct((hist_rows, BRANCH), F32),
    )
    out_specs = (
        seq_spec(BRANCH), seq_spec(BRANCH), seq_spec(BRANCH),
        pl.BlockSpec((BATCH, N_STATE), lambda i: (0, 0)),
        pl.BlockSpec((BATCH, N_STATE), lambda i: (0, 0)),
        pl.BlockSpec((hist_rows, BRANCH), lambda i: (0, 0)),
    )
    return pl.pallas_call(
        _mixer_a_kernel,
        grid=(SEQ // STEPS_A,),
        in_specs=in_specs, out_specs=out_specs, out_shape=out_shape,
        scratch_shapes=[
            pltpu.VMEM((ROWS_A, N_STATE), F32),
            pltpu.VMEM((ROWS_A, N_STATE), F32),
            pltpu.VMEM((N_SSM_BLOCKS, hist_rows + ROWS_A, LANES), F32),
            pltpu.VMEM((N_SSM_BLOCKS, ROWS_A, LANES), F32),
            pltpu.VMEM((N_SSM_BLOCKS, ROWS_A, LANES), F32),
            pltpu.VMEM((N_SSM_BLOCKS, ROWS_A, LANES), F32),
        ],
        compiler_params=pltpu.CompilerParams(
            dimension_semantics=("arbitrary",), vmem_limit_bytes=VMEM_LIMIT),
        name="prompt_mixer_a",
    )(x, p['g_mix_pre'], p['w_in_a'], p['bblk'], p['cre'], p['cimn'], p['ar'], p['ai'],
      p['ssm_d'], p['w_glu'], p['b_glu'], p['pool_w'], p['pool_scale'])


def _mem_kv_kernel(mem_ref, g_ref, wkv_ref, k_ref, v_ref):
    kv = _dot(_rms(mem_ref[...], g_ref[...]).astype(BF16), wkv_ref[...])
    k_ref[...] = kv[:, :BRANCH]
    v_ref[...] = kv[:, BRANCH:]


def _mem_kv(mem, g_mem, w_kv):
    out = jax.ShapeDtypeStruct((DEPTH, BATCH, N_MEM, BRANCH), F32)
    out_spec = pl.BlockSpec((None, None, N_MEM, BRANCH), lambda l, b: (l, b, 0, 0))
    return pl.pallas_call(
        _mem_kv_kernel,
        grid=(DEPTH, BATCH),
        in_specs=[
            pl.BlockSpec((None, N_MEM, D_MODEL), lambda l, b: (b, 0, 0)),
            pl.BlockSpec((None, 1, D_MODEL), lambda l, b: (l, 0, 0)),
            pl.BlockSpec((None, D_MODEL, 2 * BRANCH), lambda l, b: (l, 0, 0)),
        ],
        out_specs=(out_spec, out_spec), out_shape=(out, out),
        compiler_params=pltpu.CompilerParams(dimension_semantics=("arbitrary", "arbitrary")),
        name="mem_kv",
    )(mem, g_mem, w_kv)


def _memory_attention(q_ref, k_ref, v_ref):
    scale = MEM_HEAD_DIM ** -0.5
    outs = []
    for hd in range(MEM_HEADS):
        sl = slice(hd * MEM_HEAD_DIM, (hd + 1) * MEM_HEAD_DIM)
        s = lax.dot_general(q_ref[:, sl], k_ref[:, sl].astype(BF16),
                            (((1,), (1,)), ((), ())), preferred_element_type=F32) * scale
        e = jnp.exp(s - jnp.max(s, axis=-1, keepdims=True))
        prob = e / jnp.sum(e, axis=-1, keepdims=True)
        outs.append(_dot(prob.astype(BF16), v_ref[:, sl].astype(BF16)).astype(BF16))
    return jnp.concatenate(outs, axis=-1)


def _merge_ffn_kernel(with_attention, x_ref, ossm_ref, opool_ref, *refs):
    if with_attention:
        q_ref, k_ref, v_ref = refs[:3]
        refs = refs[3:]
        o_mem = _memory_attention(q_ref, k_ref, v_ref)
    else:
        o_mem = refs[0][...]
        refs = refs[1:]
    (gpre_ref, wgate_ref, wup_ref, wout_ref, gpost_ref, gfpre_ref, wfin_ref, wfout_ref,
     gfpost_ref, y_ref) = refs
    x = x_ref[...]
    h = _rms(x, gpre_ref[...]).astype(BF16)
    merged = None
    for b, o_b in enumerate((ossm_ref[...], opool_ref[...], o_mem)):
        gate = jax.nn.sigmoid(_dot(h, wgate_ref[:, b * D_MODEL:(b + 1) * D_MODEL]))
        term = gate * _dot(o_b, wup_ref[b])
        merged = term if merged is None else merged + term
    x = x + _rms(_dot(merged.astype(BF16), wout_ref[...]), gpost_ref[...])
    hf = _rms(x, gfpre_ref[...]).astype(BF16)
    f = None
    for lo, hi in FF_CHUNKS:
        hg = _dot(hf, wfin_ref[:, lo:hi])
        hu = _dot(hf, wfin_ref[:, D_FF + lo:D_FF + hi])
        part = _dot((jax.nn.silu(hg) * hu).astype(BF16), wfout_ref[lo:hi, :])
        f = part if f is None else f + part
    y_ref[...] = x + _rms(f, gfpost_ref[...])


def _merge_ffn(layer, x, o_ssm, o_pool, third, p):
    n_rows = x.shape[0]
    rows = min(ROWS_C, n_rows)
    row_spec = lambda w: pl.BlockSpec((rows, w), lambda i: (i, 0))
    with_attention = isinstance(third, tuple)
    if with_attention:
        tiles_per_seq = SEQ // rows
        kv_spec = pl.BlockSpec((None, None, N_MEM, BRANCH),
                               lambda i: (layer, i // tiles_per_seq, 0, 0))
        third_specs = [row_spec(BRANCH), kv_spec, kv_spec]
    else:
        third = (third,)
        third_specs = [row_spec(BRANCH)]
    in_specs = [row_spec(D_MODEL), row_spec(BRANCH), row_spec(BRANCH)] + third_specs + [
        _layer_spec((1, D_MODEL), layer),
        _layer_spec((D_MODEL, N_BRANCH * D_MODEL), layer),
        _layer_spec((N_BRANCH, BRANCH, D_MODEL), layer),
        _layer_spec((D_MODEL, D_MODEL), layer),
        _layer_spec((1, D_MODEL), layer),
        _layer_spec((1, D_MODEL), layer),
        _layer_spec((D_MODEL, 2 * D_FF), layer),
        _layer_spec((D_FF, D_MODEL), layer),
        _layer_spec((1, D_MODEL), layer),
    ]
    return pl.pallas_call(
        functools.partial(_merge_ffn_kernel, with_attention),
        grid=(n_rows // rows,),
        in_specs=in_specs, out_specs=row_spec(D_MODEL),
        out_shape=jax.ShapeDtypeStruct((n_rows, D_MODEL), F32),
        compiler_params=pltpu.CompilerParams(
            dimension_semantics=("parallel",), vmem_limit_bytes=VMEM_LIMIT),
        name="merge_ffn",
    )(x, o_ssm, o_pool, *third, p['g_mix_pre'], p['w_gate'], p['w_up'], p['w_out'],
      p['g_mix_post'], p['g_ffn_pre'], p['w_ffn_in'], p['w_ffn_out'], p['g_ffn_post'])


def _sample_mixer_kernel(x_ref, g_ref, win_ref, bblk_ref, cre_ref, cimn_ref, ar_ref, ai_ref,
                         d_ref, wglu_ref, bglu_ref, poolw_ref, pscale_ref,
                         h0re_ref, h0im_ref, hist_ref,
                         q_ref, ossm_ref, opool_ref, hre_ref, him_ref, nhist_ref,
                         bu_re_ref, bu_im_ref):
    h = _rms(x_ref[...], g_ref[...]).astype(BF16)
    proj = _dot(h, win_ref[...])
    u_ssm = proj[:, :BRANCH]
    u_pool = proj[:, BRANCH:2 * BRANCH]
    q_ref[...] = proj[:, 2 * BRANCH:]

    ys = []
    for k in range(N_SSM_BLOCKS):
        sl = slice(k * BLOCK_STATE, (k + 1) * BLOCK_STATE)
        _ssm_input(u_ssm[:, k * LANES:(k + 1) * LANES], bblk_ref[k], bu_re_ref, bu_im_ref, sl)
        a_r = ar_ref[:, sl]
        a_i = ai_ref[:, sl]
        h0r = h0re_ref[:, sl]
        h0i = h0im_ref[:, sl]
        hre_ref[:, sl] = bu_re_ref[:, sl] + a_r * h0r - a_i * h0i
        him_ref[:, sl] = bu_im_ref[:, sl] + a_r * h0i + a_i * h0r
        ys.append(_ssm_readout(hre_ref, him_ref, sl, cre_ref[k], cimn_ref[k]))
    y = _ssm_gate(jnp.concatenate(ys, axis=-1), u_ssm, d_ref, wglu_ref, bglu_ref)
    ossm_ref[...] = y.astype(BF16)

    pooled = []
    for gi, w in enumerate(POOL_WINDOWS):
        sl = slice(gi * POOL_GROUP_WIDTH, (gi + 1) * POOL_GROUP_WIDTH)
        acc = u_pool[:, sl]
        for j in range(1, w):
            acc = acc + hist_ref[POOL_HIST - j, :, sl]
        cnt = float(min(PAST_LEN + 1, w))
        pooled.append(acc / cnt - u_pool[:, sl])
    opool_ref[...] = _pool_mix(pooled, poolw_ref, pscale_ref).astype(BF16)
    for j in range(POOL_HIST - 1):
        nhist_ref[j] = hist_ref[j + 1]
    nhist_ref[POOL_HIST - 1] = u_pool


def _sample_mixer(layer, x_s, p, h0_re, h0_im, hist_t):
    n = x_s.shape[0]
    full = lambda shape: pl.BlockSpec(shape, lambda i: (0,) * len(shape))
    in_specs = [
        full((n, D_MODEL)),
        _layer_spec((1, D_MODEL), layer),
        _layer_spec((D_MODEL, 3 * BRANCH), layer),
        _layer_spec((N_SSM_BLOCKS, LANES, 2 * BLOCK_STATE), layer),
        _layer_spec((N_SSM_BLOCKS, BLOCK_STATE, LANES), layer),
        _layer_spec((N_SSM_BLOCKS, BLOCK_STATE, LANES), layer),
        _layer_spec((1, N_STATE), layer),
        _layer_spec((1, N_STATE), layer),
        _layer_spec((1, BRANCH), layer),
        _layer_spec((BRANCH, BRANCH), layer),
        _layer_spec((1, BRANCH), layer),
        _layer_spec((len(POOL_WINDOWS), POOL_GROUP_WIDTH, POOL_GROUP_WIDTH), layer),
        _layer_spec((1, BRANCH), layer),
        _layer_spec((n, N_STATE), layer),
        _layer_spec((n, N_STATE), layer),
        _layer_spec((POOL_HIST, n, BRANCH), layer),
    ]
    out_shape = (
        jax.ShapeDtypeStruct((n, BRANCH), F32),
        jax.ShapeDtypeStruct((n, BRANCH), BF16),
        jax.ShapeDtypeStruct((n, BRANCH), BF16),
        jax.ShapeDtypeStruct((n, N_STATE), F32),
        jax.ShapeDtypeStruct((n, N_STATE), F32),
        jax.ShapeDtypeStruct((POOL_HIST, n, BRANCH), F32),
    )
    out_specs = tuple(full(s.shape) for s in out_shape)
    return pl.pallas_call(
        _sample_mixer_kernel,
        grid=(1,),
        in_specs=in_specs, out_specs=out_specs, out_shape=out_shape,
        scratch_shapes=[pltpu.VMEM((n, N_STATE), F32), pltpu.VMEM((n, N_STATE), F32)],
        compiler_params=pltpu.CompilerParams(
            dimension_semantics=("arbitrary",), vmem_limit_bytes=VMEM_LIMIT),
        name="sample_mixer",
    )(x_s, p['g_mix_pre'], p['w_in_a'], p['bblk'], p['cre'], p['cimn'], p['ar'], p['ai'],
      p['ssm_d'], p['w_glu'], p['b_glu'], p['pool_w'], p['pool_scale'], h0_re, h0_im, hist_t)


def _sample_attention_kernel(q_ref, k_ref, v_ref, ones_ref, o_ref):
    tiles = KV_ROWS // SUBLANES
    scale = MEM_HEAD_DIM ** -0.5

    def one_sample(b, carry):
        q8 = q_ref[b] * scale
        prod = k_ref[b].reshape(tiles, SUBLANES, MEM_HEAD_DIM) * q8[None]
        prod = prod.reshape(KV_ROWS, MEM_HEAD_DIM).astype(BF16)
        s = _dot(prod, ones_ref[...]).reshape(tiles, SUBLANES, MEM_HEAD_DIM)
        m8 = jnp.max(s, axis=0)
        m8 = jnp.maximum(m8, pltpu.roll(m8, MEM_HEADS, 0))
        e = jnp.exp(s - m8[None])
        l8 = jnp.sum(e, axis=0)
        acc = jnp.sum(e * v_ref[b].reshape(tiles, SUBLANES, MEM_HEAD_DIM), axis=0)
        l8 = l8 + pltpu.roll(l8, MEM_HEADS, 0)
        acc = acc + pltpu.roll(acc, MEM_HEADS, 0)
        o_ref[b] = acc / l8
        return carry

    lax.fori_loop(0, SAMPLE_BLOCK, one_sample, 0, unroll=True)


def _sample_attention(layer, q8, k_cache, v_cache, lane_ones):
    n = q8.shape[0]
    kv_spec = pl.BlockSpec((None, SAMPLE_BLOCK, KV_ROWS, MEM_HEAD_DIM), lambda i: (layer, i, 0, 0))
    q_spec = pl.BlockSpec((SAMPLE_BLOCK, SUBLANES, MEM_HEAD_DIM), lambda i: (i, 0, 0))
    return pl.pallas_call(
        _sample_attention_kernel,
        grid=(n // SAMPLE_BLOCK,),
        in_specs=[q_spec, kv_spec, kv_spec, _const_spec((MEM_HEAD_DIM, MEM_HEAD_DIM))],
        out_specs=q_spec,
        out_shape=jax.ShapeDtypeStruct((n, SUBLANES, MEM_HEAD_DIM), F32),
        compiler_params=pltpu.CompilerParams(
            dimension_semantics=("parallel",), vmem_limit_bytes=VMEM_LIMIT),
        name="sample_attention",
    )(q8, k_cache, v_cache, lane_ones)


def kernel(x_prompt, x_sample, mem_prompt, cache_mem_k, cache_mem_v, state_ssm_re, state_ssm_im, state_pool, g_mix_pre, g_mix_post, g_ffn_pre, g_ffn_post, g_mem, w_in, w_kv, ssm_lam_re, ssm_lam_im, ssm_log_dt, ssm_b_re, ssm_b_im, ssm_c_re, ssm_c_im, ssm_d, ssm_w_glu, ssm_b_glu, pool_w, pool_scale, w_branch_up, w_out, w_ffn_in, w_ffn_out):
    ar, ai, bblk = _discretise(ssm_lam_re, ssm_lam_im, ssm_log_dt, ssm_b_re, ssm_b_im)
    vec = lambda a: a.reshape(DEPTH, 1, a.shape[-1])
    p = {
        'g_mix_pre': vec(g_mix_pre), 'g_mix_post': vec(g_mix_post),
        'g_ffn_pre': vec(g_ffn_pre), 'g_ffn_post': vec(g_ffn_post),
        'w_in_a': w_in[:, :, :3 * BRANCH].astype(BF16),
        'w_gate': w_in[:, :, 3 * BRANCH:].astype(BF16),
        'bblk': bblk, 'ar': ar, 'ai': ai,
        'cre': _c_blocks(ssm_c_re).astype(BF16),
        'cimn': _c_blocks(-ssm_c_im).astype(BF16),
        'ssm_d': vec(ssm_d), 'w_glu': ssm_w_glu.astype(BF16), 'b_glu': vec(ssm_b_glu),
        'pool_w': pool_w.astype(BF16), 'pool_scale': vec(pool_scale),
        'w_up': w_branch_up.astype(BF16), 'w_out': w_out.astype(BF16),
        'w_ffn_in': w_ffn_in.astype(BF16), 'w_ffn_out': w_ffn_out.astype(BF16),
    }
    k_mem, v_mem = _mem_kv(mem_prompt, vec(g_mem), w_kv.astype(BF16))

    lane_ones = jnp.ones((MEM_HEAD_DIM, MEM_HEAD_DIM), BF16)
    k_cache = cache_mem_k.reshape(DEPTH, DEC_BATCH, KV_ROWS, MEM_HEAD_DIM)
    v_cache = cache_mem_v.reshape(DEPTH, DEC_BATCH, KV_ROWS, MEM_HEAD_DIM)
    h0_re = state_ssm_re.reshape(DEPTH, DEC_BATCH, N_STATE)
    h0_im = state_ssm_im.reshape(DEPTH, DEC_BATCH, N_STATE)
    hist_t = jnp.swapaxes(state_pool, 1, 2)

    xp = x_prompt
    xs = x_sample.reshape(DEC_BATCH, D_MODEL)
    n_prompt = BATCH * SEQ
    re_p, im_p, pool_p, re_s, im_s, pool_s = [], [], [], [], [], []
    for layer in range(DEPTH):
        q, o_ssm, o_pool, hre, him, hist = _mixer_a(layer, xp, p)
        xp = _merge_ffn(layer, xp.reshape(n_prompt, D_MODEL), o_ssm.reshape(n_prompt, BRANCH),
                        o_pool.reshape(n_prompt, BRANCH),
                        (q.reshape(n_prompt, BRANCH), k_mem, v_mem), p)
        xp = xp.reshape(BATCH, SEQ, D_MODEL)
        re_p.append(hre.reshape(BATCH, SSM_GROUPS, SSM_STATE))
        im_p.append(him.reshape(BATCH, SSM_GROUPS, SSM_STATE))
        pool_p.append(jnp.swapaxes(hist.reshape(POOL_HIST, BATCH, BRANCH), 0, 1))

        qs, os_ssm, os_pool, hre, him, nhist = _sample_mixer(layer, xs, p, h0_re, h0_im, hist_t)
        qs4 = qs.reshape(DEC_BATCH, MEM_HEADS, MEM_HEAD_DIM)
        os_mem = _sample_attention(layer, jnp.concatenate([qs4, qs4], axis=1),
                                   k_cache, v_cache, lane_ones)
        os_mem = os_mem[:, :MEM_HEADS].reshape(DEC_BATCH, BRANCH).astype(BF16)
        xs = _merge_ffn(layer, xs, os_ssm, os_pool, os_mem, p)
        re_s.append(hre.reshape(DEC_BATCH, SSM_GROUPS, SSM_STATE))
        im_s.append(him.reshape(DEC_BATCH, SSM_GROUPS, SSM_STATE))
        pool_s.append(jnp.swapaxes(nhist, 0, 1))

    y_prompt = xp
    y_sample = xs.reshape(DEC_BATCH, 1, D_MODEL)
    kv_shape = (DEPTH, BATCH, N_MEM, MEM_HEADS, MEM_HEAD_DIM)
    return (y_prompt, y_sample,
            jnp.stack(re_p), jnp.stack(im_p), jnp.stack(pool_p),
            k_mem.reshape(kv_shape), v_mem.reshape(kv_shape),
            jnp.stack(re_s), jnp.stack(im_s), jnp.stack(pool_s))
```

```python
import functools
import math

import jax
import jax.numpy as jnp
from jax import lax
from jax.experimental import pallas as pl
from jax.experimental.pallas import tpu as pltpu

D_MODEL = 1024
BATCH = 8
SEQ = 2048
DEPTH = 4
DEC_BATCH = 128
PAST_LEN = 16384

BRANCH = D_MODEL // 2
SSM_GROUP = 16
SSM_GROUPS = BRANCH // SSM_GROUP
SSM_STATE = 64
N_STATE = SSM_GROUPS * SSM_STATE
POOL_WINDOWS = (2, 4, 8, 16)
POOL_GROUP_WIDTH = BRANCH // len(POOL_WINDOWS)
POOL_HIST = max(POOL_WINDOWS) - 1
N_MEM = 256
MEM_HEADS = 4
MEM_HEAD_DIM = BRANCH // MEM_HEADS
KV_ROWS = N_MEM * MEM_HEADS
N_BRANCH = 3
GATE_HALF = N_BRANCH * D_MODEL // 2
D_FF = 2816
RMS_EPS = 1e-6

LANES = 128
SUBLANES = 8
GROUPS_PER_BLOCK = LANES // SSM_GROUP
N_SSM_BLOCKS = BRANCH // LANES
BLOCK_STATE = GROUPS_PER_BLOCK * SSM_STATE

ROWS_A = 512
STEPS_A = ROWS_A // BATCH
ROWS_C = 512
SAMPLE_BLOCK = 8
FF_CHUNKS = ((0, 1536), (1536, 2816))
VMEM_LIMIT = 56 * 1024 * 1024

BF16 = jnp.bfloat16
F32 = jnp.float32


def _rms(x, g):
    ms = jnp.mean(x * x, axis=-1, keepdims=True)
    return x * lax.rsqrt(ms + RMS_EPS) * g


def _dot(a, b):
    return jnp.dot(a, b, preferred_element_type=F32)


def _const_spec(shape):
    nd = len(shape)
    return pl.BlockSpec(shape, lambda *_: (0,) * nd, pipeline_mode=pl.Buffered(1))


def _layer_spec(shape, layer):
    nd = len(shape)
    return pl.BlockSpec((None,) + tuple(shape), lambda *_: (layer,) + (0,) * nd,
                        pipeline_mode=pl.Buffered(1))


def _discretise_kernel(lr_ref, li_ref, ldt_ref, br_ref, bi_ref,
                       ar_ref, ai_ref, bbr_ref, bbi_ref):
    lr = lr_ref[...]
    li = li_ref[...]
    dt = jnp.exp(ldt_ref[...])
    zr = lr * dt
    zi = li * dt
    mag = jnp.exp(zr)
    ar = mag * jnp.cos(zi)
    ai = mag * jnp.sin(zi)
    den = lr * lr + li * li
    fr = ((ar - 1.0) * lr + ai * li) / den
    fi = (ai * lr - (ar - 1.0) * li) / den
    br = br_ref[...]
    bi = bi_ref[...]
    ar_ref[...] = ar
    ai_ref[...] = ai
    bbr_ref[...] = fr * br - fi * bi
    bbi_ref[...] = fr * bi + fi * br


def _discretise(lam_re, lam_im, log_dt, b_re, b_im):
    rows = DEPTH * SSM_GROUPS
    width = SSM_GROUP * SSM_STATE

    def tile_p(a):
        return jnp.tile(a.reshape(rows, 1, SSM_STATE), (1, SSM_GROUP, 1)).reshape(rows, width)

    def b_t(a):
        return jnp.swapaxes(a, -1, -2).reshape(rows, width)

    ldt = jnp.broadcast_to(log_dt.reshape(rows, 1), (rows, width))
    out = jax.ShapeDtypeStruct((rows, width), F32)
    ar, ai, bbr, bbi = pl.pallas_call(
        _discretise_kernel, out_shape=(out, out, out, out), name="s5_discretise",
    )(tile_p(lam_re), tile_p(lam_im), ldt, b_t(b_re), b_t(b_im))
    ar = ar[:, :SSM_STATE].reshape(DEPTH, 1, N_STATE)
    ai = ai[:, :SSM_STATE].reshape(DEPTH, 1, N_STATE)
    shape5 = (DEPTH, N_SSM_BLOCKS, GROUPS_PER_BLOCK, SSM_GROUP, SSM_STATE)
    eye = jnp.eye(GROUPS_PER_BLOCK, dtype=F32)

    def b_blocks(bb):
        m = jnp.einsum('lkghp,gG->lkghGp', bb.reshape(shape5), eye)
        return m.reshape(DEPTH, N_SSM_BLOCKS, LANES, BLOCK_STATE)

    bblk = jnp.concatenate([b_blocks(bbr), b_blocks(bbi)], axis=-1).astype(BF16)
    return ar, ai, bblk


def _c_blocks(c):
    shape5 = (DEPTH, N_SSM_BLOCKS, GROUPS_PER_BLOCK, SSM_GROUP, SSM_STATE)
    eye = jnp.eye(GROUPS_PER_BLOCK, dtype=F32)
    m = jnp.einsum('lkghp,gG->lkgpGh', c.reshape(shape5), eye)
    return m.reshape(DEPTH, N_SSM_BLOCKS, BLOCK_STATE, LANES)


def _ssm_input(u_block, bblk, bu_re_ref, bu_im_ref, sl):
    bu = _dot(u_block.astype(BF16), bblk)
    bu_re_ref[:, sl] = bu[:, :BLOCK_STATE]
    bu_im_ref[:, sl] = bu[:, BLOCK_STATE:]


def _ssm_readout(h_re_ref, h_im_ref, sl, cre, cimn):
    return _dot(h_re_ref[:, sl].astype(BF16), cre) + _dot(h_im_ref[:, sl].astype(BF16), cimn)


def _ssm_gate(y, u_ssm, d_ref, wglu_ref, bglu_ref):
    y = jax.nn.gelu(y + d_ref[...] * u_ssm)
    return y * jax.nn.sigmoid(_dot(y.astype(BF16), wglu_ref[...]) + bglu_ref[...])


def _pool_mix(pooled, poolw_ref, pscale_ref):
    outs = []
    for gi in range(len(POOL_WINDOWS)):
        sl = slice(gi * POOL_GROUP_WIDTH, (gi + 1) * POOL_GROUP_WIDTH)
        outs.append(_dot(pooled[gi].astype(BF16), poolw_ref[gi]) * pscale_ref[:, sl])
    return jnp.concatenate(outs, axis=-1)


def _mixer_a_kernel(x_ref, g_ref, win_ref, bblk_ref, cre_ref, cimn_ref, ar_ref, ai_ref,
                    d_ref, wglu_ref, bglu_ref, poolw_ref, pscale_ref,
                    wgate_lo_ref, wgate_hi_ref, wup_ref, wout_ref, wfin_ref, wfout_ref,
                    q_ref, ossm_ref, opool_ref, hre_ref, him_ref, hist_ref,
                    wgate_bf_ref, wup_bf_ref, wout_bf_ref, wfin_bf_ref, wfout_bf_ref,
                    bu_re_ref, bu_im_ref, pool_buf, ussm_tm, ossm_tm, opool_tm):
    i = pl.program_id(0)
    hist_rows = POOL_HIST * BATCH

    wgate_bf_ref[:, :GATE_HALF] = wgate_lo_ref[...].astype(BF16)
    wgate_bf_ref[:, GATE_HALF:] = wgate_hi_ref[...].astype(BF16)
    wup_bf_ref[...] = wup_ref[...].astype(BF16)
    wout_bf_ref[...] = wout_ref[...].astype(BF16)
    wfin_bf_ref[...] = wfin_ref[...].astype(BF16)
    wfout_bf_ref[...] = wfout_ref[...].astype(BF16)

    @pl.when(i == 0)
    def _():
        hre_ref[...] = jnp.zeros_like(hre_ref)
        him_ref[...] = jnp.zeros_like(him_ref)
        pool_buf[:, 0:hist_rows, :] = jnp.zeros((N_SSM_BLOCKS, hist_rows, LANES), F32)

    x = x_ref[...].reshape(ROWS_A, D_MODEL)
    h = _rms(x, g_ref[...]).astype(BF16)
    proj = _dot(h, win_ref[...])
    for b in range(BATCH):
        rows = slice(b * STEPS_A, (b + 1) * STEPS_A)
        q_ref[b] = proj[rows, 2 * BRANCH:].astype(BF16)
        for k in range(N_SSM_BLOCKS):
            ussm_tm[k, pl.ds(b, STEPS_A, stride=BATCH), :] = proj[rows, k * LANES:(k + 1) * LANES]
            pool_buf[k, pl.ds(hist_rows + b, STEPS_A, stride=BATCH), :] = (
                proj[rows, BRANCH + k * LANES:BRANCH + (k + 1) * LANES])
    u_ssm = jnp.concatenate([ussm_tm[k] for k in range(N_SSM_BLOCKS)], axis=-1)

    ys = []
    for k in range(N_SSM_BLOCKS):
        sl = slice(k * BLOCK_STATE, (k + 1) * BLOCK_STATE)
        _ssm_input(ussm_tm[k], bblk_ref[k], bu_re_ref, bu_im_ref, sl)
        a_r = jnp.broadcast_to(ar_ref[:, sl], (BATCH, BLOCK_STATE))
        a_i = jnp.broadcast_to(ai_ref[:, sl], (BATCH, BLOCK_STATE))
        hr = hre_ref[:, sl]
        hi = him_ref[:, sl]
        for t in range(STEPS_A):
            rows = slice(t * BATCH, (t + 1) * BATCH)
            hr, hi = (a_r * hr - a_i * hi + bu_re_ref[rows, sl],
                      a_r * hi + a_i * hr + bu_im_ref[rows, sl])
            bu_re_ref[rows, sl] = hr
            bu_im_ref[rows, sl] = hi
        hre_ref[:, sl] = hr
        him_ref[:, sl] = hi
        ys.append(_ssm_readout(bu_re_ref, bu_im_ref, sl, cre_ref[k], cimn_ref[k]))
    o_ssm = _ssm_gate(jnp.concatenate(ys, axis=-1), u_ssm, d_ref, wglu_ref, bglu_ref)

    t_pos = i * STEPS_A + lax.broadcasted_iota(jnp.int32, (ROWS_A, POOL_GROUP_WIDTH), 0) // BATCH
    pooled = []
    for gi, w in enumerate(POOL_WINDOWS):
        u_g = pool_buf[gi, hist_rows:hist_rows + ROWS_A, :]
        acc = u_g
        for j in range(1, w):
            start = hist_rows - j * BATCH
            acc = acc + pool_buf[gi, start:start + ROWS_A, :]
        cnt = jnp.minimum(t_pos + 1, w).astype(F32)
        pooled.append(acc / cnt - u_g)
    o_pool = _pool_mix(pooled, poolw_ref, pscale_ref)
    pool_buf[:, 0:hist_rows, :] = pool_buf[:, ROWS_A:ROWS_A + hist_rows, :]

    for k in range(N_SSM_BLOCKS):
        ossm_tm[k] = o_ssm[:, k * LANES:(k + 1) * LANES]
        opool_tm[k] = o_pool[:, k * LANES:(k + 1) * LANES]
    for b in range(BATCH):
        rows = pl.ds(b, STEPS_A, stride=BATCH)
        ossm_ref[b] = jnp.concatenate(
            [ossm_tm[k, rows, :] for k in range(N_SSM_BLOCKS)], axis=-1).astype(BF16)
        opool_ref[b] = jnp.concatenate(
            [opool_tm[k, rows, :] for k in range(N_SSM_BLOCKS)], axis=-1).astype(BF16)

    @pl.when(i == pl.num_programs(0) - 1)
    def _():
        for k in range(N_SSM_BLOCKS):
            hist_ref[:, k * LANES:(k + 1) * LANES] = pool_buf[k, 0:hist_rows, :]


def _mixer_a(layer, x, p, raw):
    hist_rows = POOL_HIST * BATCH
    n_steps = SEQ // STEPS_A
    seq_spec = lambda w: pl.BlockSpec((BATCH, STEPS_A, w), lambda i: (0, i, 0))
    cast_plan = {
        'w_gate': (D_MODEL // n_steps, N_BRANCH * D_MODEL, 1),
        'w_up': (N_BRANCH * BRANCH // n_steps, D_MODEL, 1),
        'w_out': (D_MODEL // n_steps, D_MODEL, 1),
        'w_ffn_in': (D_MODEL // n_steps, 2 * D_FF, 1),
        'w_ffn_out': (2 * D_FF // n_steps, D_MODEL, 2),
    }

    def chunk_in(name, width=None, col=0):
        rows, full_width, per = cast_plan[name]
        return pl.BlockSpec((None, rows, width or full_width),
                            lambda i: (layer, i // per, col))

    def chunk_out(name):
        rows, width, per = cast_plan[name]
        return pl.BlockSpec((rows, width), lambda i: (i // per, 0))

    cast_names = ('w_gate', 'w_up', 'w_out', 'w_ffn_in', 'w_ffn_out')
    cast_shapes = tuple(
        jax.ShapeDtypeStruct((cast_plan[n][0] * n_steps // cast_plan[n][2], cast_plan[n][1]), BF16)
        for n in cast_names)
    in_specs = [
        seq_spec(D_MODEL),
        _layer_spec((1, D_MODEL), layer),
        _layer_spec((D_MODEL, 3 * BRANCH), layer),
        _layer_spec((N_SSM_BLOCKS, LANES, 2 * BLOCK_STATE), layer),
        _layer_spec((N_SSM_BLOCKS, BLOCK_STATE, LANES), layer),
        _layer_spec((N_SSM_BLOCKS, BLOCK_STATE, LANES), layer),
        _layer_spec((1, N_STATE), layer),
        _layer_spec((1, N_STATE), layer),
        _layer_spec((1, BRANCH), layer),
        _layer_spec((BRANCH, BRANCH), layer),
        _layer_spec((1, BRANCH), layer),
        _layer_spec((len(POOL_WINDOWS), POOL_GROUP_WIDTH, POOL_GROUP_WIDTH), layer),
        _layer_spec((1, BRANCH), layer),
        chunk_in('w_gate', GATE_HALF, 1), chunk_in('w_gate', GATE_HALF, 2),
        chunk_in('w_up'), chunk_in('w_out'), chunk_in('w_ffn_in'), chunk_in('w_ffn_out'),
    ]
    branch_out = jax.ShapeDtypeStruct((BATCH, SEQ, BRANCH), BF16)
    out_shape = (
        branch_out, branch_out, branch_out,
        jax.ShapeDtypeStruct((BATCH, N_STATE), F32),
        jax.ShapeDtypeStruct((BATCH, N_STATE), F32),
        jax.ShapeDtypeStruct((hist_rows, BRANCH), F32),
    ) + cast_shapes
    out_specs = (
        seq_spec(BRANCH), seq_spec(BRANCH), seq_spec(BRANCH),
        pl.BlockSpec((BATCH, N_STATE), lambda i: (0, 0)),
        pl.BlockSpec((BATCH, N_STATE), lambda i: (0, 0)),
        pl.BlockSpec((hist_rows, BRANCH), lambda i: (0, 0)),
    ) + tuple(chunk_out(n) for n in cast_names)
    outs = pl.pallas_call(
        _mixer_a_kernel,
        grid=(n_steps,),
        in_specs=in_specs, out_specs=out_specs, out_shape=out_shape,
        scratch_shapes=[
            pltpu.VMEM((ROWS_A, N_STATE), F32),
            pltpu.VMEM((ROWS_A, N_STATE), F32),
            pltpu.VMEM((N_SSM_BLOCKS, hist_rows + ROWS_A, LANES), F32),
            pltpu.VMEM((N_SSM_BLOCKS, ROWS_A, LANES), F32),
            pltpu.VMEM((N_SSM_BLOCKS, ROWS_A, LANES), F32),
            pltpu.VMEM((N_SSM_BLOCKS, ROWS_A, LANES), F32),
        ],
        compiler_params=pltpu.CompilerParams(
            dimension_semantics=("arbitrary",), vmem_limit_bytes=VMEM_LIMIT),
        name="prompt_mixer_a",
    )(x, p['g_mix_pre'], p['w_in_a'], p['bblk'], p['cre'], p['cimn'], p['ar'], p['ai'],
      p['ssm_d'], p['w_glu'], p['b_glu'], p['pool_w'], p['pool_scale'],
      raw['w_in'], raw['w_in'], raw['w_up'], raw['w_out'], raw['w_ffn_in'], raw['w_ffn_out'])
    return outs[:6], dict(zip(cast_names, outs[6:]))


def _mem_kv_kernel(mem_ref, g_ref, wkv_ref, k_ref, v_ref):
    for b in range(BATCH):
        kv = _dot(_rms(mem_ref[b], g_ref[...]).astype(BF16), wkv_ref[...])
        for hd in range(MEM_HEADS):
            rows = pl.ds(hd, N_MEM, stride=MEM_HEADS)
            k_ref[b, rows, :] = kv[:, hd * MEM_HEAD_DIM:(hd + 1) * MEM_HEAD_DIM]
            v_ref[b, rows, :] = kv[:, BRANCH + hd * MEM_HEAD_DIM:BRANCH + (hd + 1) * MEM_HEAD_DIM]


def _mem_kv(mem, g_mem, w_kv):
    out = jax.ShapeDtypeStruct((DEPTH, BATCH, KV_ROWS, MEM_HEAD_DIM), F32)
    out_spec = pl.BlockSpec((None, BATCH, KV_ROWS, MEM_HEAD_DIM), lambda l: (l, 0, 0, 0))
    return pl.pallas_call(
        _mem_kv_kernel,
        grid=(DEPTH,),
        in_specs=[
            _const_spec((BATCH, N_MEM, D_MODEL)),
            pl.BlockSpec((None, 1, D_MODEL), lambda l: (l, 0, 0)),
            pl.BlockSpec((None, D_MODEL, 2 * BRANCH), lambda l: (l, 0, 0)),
        ],
        out_specs=(out_spec, out_spec), out_shape=(out, out),
        compiler_params=pltpu.CompilerParams(
            dimension_semantics=("arbitrary",), vmem_limit_bytes=VMEM_LIMIT),
        name="mem_kv",
    )(mem, g_mem, w_kv)


def _memory_attention(q_ref, k_ref, v_ref):
    scale = MEM_HEAD_DIM ** -0.5
    outs = []
    for hd in range(MEM_HEADS):
        sl = slice(hd * MEM_HEAD_DIM, (hd + 1) * MEM_HEAD_DIM)
        head_rows = pl.ds(hd, N_MEM, stride=MEM_HEADS)
        s = lax.dot_general(q_ref[:, sl], k_ref[head_rows, :].astype(BF16),
                            (((1,), (1,)), ((), ())), preferred_element_type=F32) * scale
        e = jnp.exp(s - jnp.max(s, axis=-1, keepdims=True))
        prob = e / jnp.sum(e, axis=-1, keepdims=True)
        outs.append(_dot(prob.astype(BF16), v_ref[head_rows, :].astype(BF16)).astype(BF16))
    return jnp.concatenate(outs, axis=-1)


def _merge_ffn_kernel(with_attention, x_ref, ossm_ref, opool_ref, *refs):
    if with_attention:
        q_ref, k_ref, v_ref = refs[:3]
        refs = refs[3:]
        o_mem = _memory_attention(q_ref, k_ref, v_ref)
    else:
        o_mem = refs[0][...]
        refs = refs[1:]
    (gpre_ref, wgate_ref, wup_ref, wout_ref, gpost_ref, gfpre_ref, wfin_ref, wfout_ref,
     gfpost_ref, y_ref) = refs
    x = x_ref[...]
    h = _rms(x, gpre_ref[...]).astype(BF16)
    merged = None
    for b, o_b in enumerate((ossm_ref[...], opool_ref[...], o_mem)):
        gate = jax.nn.sigmoid(_dot(h, wgate_ref[:, b * D_MODEL:(b + 1) * D_MODEL]))
        term = gate * _dot(o_b, wup_ref[b * BRANCH:(b + 1) * BRANCH, :])
        merged = term if merged is None else merged + term
    x = x + _rms(_dot(merged.astype(BF16), wout_ref[...]), gpost_ref[...])
    hf = _rms(x, gfpre_ref[...]).astype(BF16)
    f = None
    for lo, hi in FF_CHUNKS:
        hg = _dot(hf, wfin_ref[:, lo:hi])
        hu = _dot(hf, wfin_ref[:, D_FF + lo:D_FF + hi])
        part = _dot((jax.nn.silu(hg) * hu).astype(BF16), wfout_ref[lo:hi, :])
        f = part if f is None else f + part
    y_ref[...] = x + _rms(f, gfpost_ref[...])


def _merge_ffn(layer, x, o_ssm, o_pool, third, p, w):
    n_rows = x.shape[0]
    rows = min(ROWS_C, n_rows)
    row_spec = lambda w: pl.BlockSpec((rows, w), lambda i: (i, 0))
    with_attention = isinstance(third, tuple)
    if with_attention:
        tiles_per_seq = SEQ // rows
        kv_spec = pl.BlockSpec((None, None, KV_ROWS, MEM_HEAD_DIM),
                               lambda i: (layer, i // tiles_per_seq, 0, 0))
        third_specs = [row_spec(BRANCH), kv_spec, kv_spec]
    else:
        third = (third,)
        third_specs = [row_spec(BRANCH)]
    in_specs = [row_spec(D_MODEL), row_spec(BRANCH), row_spec(BRANCH)] + third_specs + [
        _layer_spec((1, D_MODEL), layer),
        _const_spec((D_MODEL, N_BRANCH * D_MODEL)),
        _const_spec((N_BRANCH * BRANCH, D_MODEL)),
        _const_spec((D_MODEL, D_MODEL)),
        _layer_spec((1, D_MODEL), layer),
        _layer_spec((1, D_MODEL), layer),
        _const_spec((D_MODEL, 2 * D_FF)),
        _const_spec((D_FF, D_MODEL)),
        _layer_spec((1, D_MODEL), layer),
    ]
    return pl.pallas_call(
        functools.partial(_merge_ffn_kernel, with_attention),
        grid=(n_rows // rows,),
        in_specs=in_specs, out_specs=row_spec(D_MODEL),
        out_shape=jax.ShapeDtypeStruct((n_rows, D_MODEL), F32),
        compiler_params=pltpu.CompilerParams(
            dimension_semantics=("parallel",), vmem_limit_bytes=VMEM_LIMIT),
        name="merge_ffn",
    )(x, o_ssm, o_pool, *third, p['g_mix_pre'], w['w_gate'], w['w_up'], w['w_out'],
      p['g_mix_post'], p['g_ffn_pre'], w['w_ffn_in'], w['w_ffn_out'], p['g_ffn_post'])


def _sample_mixer_kernel(x_ref, g_ref, win_ref, bblk_ref, cre_ref, cimn_ref, ar_ref, ai_ref,
                         d_ref, wglu_ref, bglu_ref, poolw_ref, pscale_ref,
                         h0re_ref, h0im_ref, hist_ref,
                         q_ref, ossm_ref, opool_ref, hre_ref, him_ref, nhist_ref,
                         bu_re_ref, bu_im_ref):
    h = _rms(x_ref[...], g_ref[...]).astype(BF16)
    proj = _dot(h, win_ref[...])
    u_ssm = proj[:, :BRANCH]
    u_pool = proj[:, BRANCH:2 * BRANCH]
    q_ref[...] = proj[:, 2 * BRANCH:]

    ys = []
    for k in range(N_SSM_BLOCKS):
        sl = slice(k * BLOCK_STATE, (k + 1) * BLOCK_STATE)
        _ssm_input(u_ssm[:, k * LANES:(k + 1) * LANES], bblk_ref[k], bu_re_ref, bu_im_ref, sl)
        a_r = ar_ref[:, sl]
        a_i = ai_ref[:, sl]
        h0r = h0re_ref[:, sl]
        h0i = h0im_ref[:, sl]
        hre_ref[:, sl] = bu_re_ref[:, sl] + a_r * h0r - a_i * h0i
        him_ref[:, sl] = bu_im_ref[:, sl] + a_r * h0i + a_i * h0r
        ys.append(_ssm_readout(hre_ref, him_ref, sl, cre_ref[k], cimn_ref[k]))
    y = _ssm_gate(jnp.concatenate(ys, axis=-1), u_ssm, d_ref, wglu_ref, bglu_ref)
    ossm_ref[...] = y.astype(BF16)

    pooled = []
    for gi, w in enumerate(POOL_WINDOWS):
        sl = slice(gi * POOL_GROUP_WIDTH, (gi + 1) * POOL_GROUP_WIDTH)
        acc = u_pool[:, sl]
        for j in range(1, w):
            acc = acc + hist_ref[POOL_HIST - j, :, sl]
        cnt = float(min(PAST_LEN + 1, w))
        pooled.append(acc / cnt - u_pool[:, sl])
    opool_ref[...] = _pool_mix(pooled, poolw_ref, pscale_ref).astype(BF16)
    for j in range(POOL_HIST - 1):
        nhist_ref[j] = hist_ref[j + 1]
    nhist_ref[POOL_HIST - 1] = u_pool


def _sample_mixer(layer, x_s, p, h0_re, h0_im, hist_t):
    n = x_s.shape[0]
    full = lambda shape: pl.BlockSpec(shape, lambda i: (0,) * len(shape))
    in_specs = [
        full((n, D_MODEL)),
        _layer_spec((1, D_MODEL), layer),
        _layer_spec((D_MODEL, 3 * BRANCH), layer),
        _layer_spec((N_SSM_BLOCKS, LANES, 2 * BLOCK_STATE), layer),
        _layer_spec((N_SSM_BLOCKS, BLOCK_STATE, LANES), layer),
        _layer_spec((N_SSM_BLOCKS, BLOCK_STATE, LANES), layer),
        _layer_spec((1, N_STATE), layer),
        _layer_spec((1, N_STATE), layer),
        _layer_spec((1, BRANCH), layer),
        _layer_spec((BRANCH, BRANCH), layer),
        _layer_spec((1, BRANCH), layer),
        _layer_spec((len(POOL_WINDOWS), POOL_GROUP_WIDTH, POOL_GROUP_WIDTH), layer),
        _layer_spec((1, BRANCH), layer),
        _layer_spec((n, N_STATE), layer),
        _layer_spec((n, N_STATE), layer),
        _layer_spec((POOL_HIST, n, BRANCH), layer),
    ]
    out_shape = (
        jax.ShapeDtypeStruct((n, BRANCH), F32),
        jax.ShapeDtypeStruct((n, BRANCH), BF16),
        jax.ShapeDtypeStruct((n, BRANCH), BF16),
        jax.ShapeDtypeStruct((n, N_STATE), F32),
        jax.ShapeDtypeStruct((n, N_STATE), F32),
        jax.ShapeDtypeStruct((POOL_HIST, n, BRANCH), F32),
    )
    out_specs = tuple(full(s.shape) for s in out_shape)
    return pl.pallas_call(
        _sample_mixer_kernel,
        grid=(1,),
        in_specs=in_specs, out_specs=out_specs, out_shape=out_shape,
        scratch_shapes=[pltpu.VMEM((n, N_STATE), F32), pltpu.VMEM((n, N_STATE), F32)],
        compiler_params=pltpu.CompilerParams(
            dimension_semantics=("arbitrary",), vmem_limit_bytes=VMEM_LIMIT),
        name="sample_mixer",
    )(x_s, p['g_mix_pre'], p['w_in_a'], p['bblk'], p['cre'], p['cimn'], p['ar'], p['ai'],
      p['ssm_d'], p['w_glu'], p['b_glu'], p['pool_w'], p['pool_scale'], h0_re, h0_im, hist_t)


def _sample_attention_kernel(q_ref, k_ref, v_ref, ones_ref, o_ref):
    tiles = KV_ROWS // SUBLANES
    scale = MEM_HEAD_DIM ** -0.5

    def one_sample(b, carry):
        q8 = q_ref[b] * scale
        prod = k_ref[b].reshape(tiles, SUBLANES, MEM_HEAD_DIM) * q8[None]
        prod = prod.reshape(KV_ROWS, MEM_HEAD_DIM).astype(BF16)
        s = _dot(prod, ones_ref[...]).reshape(tiles, SUBLANES, MEM_HEAD_DIM)
        m8 = jnp.max(s, axis=0)
        m8 = jnp.maximum(m8, pltpu.roll(m8, MEM_HEADS, 0))
        e = jnp.exp(s - m8[None])
        l8 = jnp.sum(e, axis=0)
        acc = jnp.sum(e * v_ref[b].reshape(tiles, SUBLANES, MEM_HEAD_DIM), axis=0)
        l8 = l8 + pltpu.roll(l8, MEM_HEADS, 0)
        acc = acc + pltpu.roll(acc, MEM_HEADS, 0)
        o_ref[b] = acc / l8
        return carry

    lax.fori_loop(0, SAMPLE_BLOCK, one_sample, 0, unroll=True)


def _sample_attention(layer, q8, k_cache, v_cache, lane_ones):
    n = q8.shape[0]
    kv_spec = pl.BlockSpec((None, SAMPLE_BLOCK, KV_ROWS, MEM_HEAD_DIM), lambda i: (layer, i, 0, 0))
    q_spec = pl.BlockSpec((SAMPLE_BLOCK, SUBLANES, MEM_HEAD_DIM), lambda i: (i, 0, 0))
    return pl.pallas_call(
        _sample_attention_kernel,
        grid=(n // SAMPLE_BLOCK,),
        in_specs=[q_spec, kv_spec, kv_spec, _const_spec((MEM_HEAD_DIM, MEM_HEAD_DIM))],
        out_specs=q_spec,
        out_shape=jax.ShapeDtypeStruct((n, SUBLANES, MEM_HEAD_DIM), F32),
        compiler_params=pltpu.CompilerParams(
            dimension_semantics=("parallel",), vmem_limit_bytes=VMEM_LIMIT),
        name="sample_attention",
    )(q8, k_cache, v_cache, lane_ones)


def kernel(x_prompt, x_sample, mem_prompt, cache_mem_k, cache_mem_v, state_ssm_re, state_ssm_im, state_pool, g_mix_pre, g_mix_post, g_ffn_pre, g_ffn_post, g_mem, w_in, w_kv, ssm_lam_re, ssm_lam_im, ssm_log_dt, ssm_b_re, ssm_b_im, ssm_c_re, ssm_c_im, ssm_d, ssm_w_glu, ssm_b_glu, pool_w, pool_scale, w_branch_up, w_out, w_ffn_in, w_ffn_out):
    ar, ai, bblk = _discretise(ssm_lam_re, ssm_lam_im, ssm_log_dt, ssm_b_re, ssm_b_im)
    vec = lambda a: a.reshape(DEPTH, 1, a.shape[-1])
    p = {
        'g_mix_pre': vec(g_mix_pre), 'g_mix_post': vec(g_mix_post),
        'g_ffn_pre': vec(g_ffn_pre), 'g_ffn_post': vec(g_ffn_post),
        'w_in_a': w_in[:, :, :3 * BRANCH].astype(BF16),
        'bblk': bblk, 'ar': ar, 'ai': ai,
        'cre': _c_blocks(ssm_c_re).astype(BF16),
        'cimn': _c_blocks(-ssm_c_im).astype(BF16),
        'ssm_d': vec(ssm_d), 'w_glu': ssm_w_glu.astype(BF16), 'b_glu': vec(ssm_b_glu),
        'pool_w': pool_w.astype(BF16), 'pool_scale': vec(pool_scale),
    }
    raw = {
        'w_in': w_in, 'w_up': w_branch_up.reshape(DEPTH, N_BRANCH * BRANCH, D_MODEL),
        'w_out': w_out, 'w_ffn_in': w_ffn_in, 'w_ffn_out': w_ffn_out,
    }
    k_mem, v_mem = _mem_kv(mem_prompt, vec(g_mem), w_kv.astype(BF16))

    lane_ones = jnp.ones((MEM_HEAD_DIM, MEM_HEAD_DIM), BF16)
    k_cache = cache_mem_k.reshape(DEPTH, DEC_BATCH, KV_ROWS, MEM_HEAD_DIM)
    v_cache = cache_mem_v.reshape(DEPTH, DEC_BATCH, KV_ROWS, MEM_HEAD_DIM)
    h0_re = state_ssm_re.reshape(DEPTH, DEC_BATCH, N_STATE)
    h0_im = state_ssm_im.reshape(DEPTH, DEC_BATCH, N_STATE)
    hist_t = jnp.swapaxes(state_pool, 1, 2)

    xp = x_prompt
    xs = x_sample.reshape(DEC_BATCH, D_MODEL)
    n_prompt = BATCH * SEQ
    re_p, im_p, pool_p, re_s, im_s, pool_s = [], [], [], [], [], []
    for layer in range(DEPTH):
        (q, o_ssm, o_pool, hre, him, hist), w_bf = _mixer_a(layer, xp, p, raw)
        xp = _merge_ffn(layer, xp.reshape(n_prompt, D_MODEL), o_ssm.reshape(n_prompt, BRANCH),
                        o_pool.reshape(n_prompt, BRANCH),
                        (q.reshape(n_prompt, BRANCH), k_mem, v_mem), p, w_bf)
        xp = xp.reshape(BATCH, SEQ, D_MODEL)
        re_p.append(hre.reshape(BATCH, SSM_GROUPS, SSM_STATE))
        im_p.append(him.reshape(BATCH, SSM_GROUPS, SSM_STATE))
        pool_p.append(jnp.swapaxes(hist.reshape(POOL_HIST, BATCH, BRANCH), 0, 1))

        qs, os_ssm, os_pool, hre, him, nhist = _sample_mixer(layer, xs, p, h0_re, h0_im, hist_t)
        qs4 = qs.reshape(DEC_BATCH, MEM_HEADS, MEM_HEAD_DIM)
        os_mem = _sample_attention(layer, jnp.concatenate([qs4, qs4], axis=1),
                                   k_cache, v_cache, lane_ones)
        os_mem = os_mem[:, :MEM_HEADS].reshape(DEC_BATCH, BRANCH).astype(BF16)
        xs = _merge_ffn(layer, xs, os_ssm, os_pool, os_mem, p, w_bf)
        re_s.append(hre.reshape(DEC_BATCH, SSM_GROUPS, SSM_STATE))
        im_s.append(him.reshape(DEC_BATCH, SSM_GROUPS, SSM_STATE))
        pool_s.append(jnp.swapaxes(nhist, 0, 1))

    y_prompt = xp
    y_sample = xs.reshape(DEC_BATCH, 1, D_MODEL)
    kv_shape = (DEPTH, BATCH, N_MEM, MEM_HEADS, MEM_HEAD_DIM)
    return (y_prompt, y_sample,
            jnp.stack(re_p), jnp.stack(im_p), jnp.stack(pool_p),
            k_mem.reshape(kv_shape), v_mem.reshape(kv_shape),
            jnp.stack(re_s), jnp.stack(im_s), jnp.stack(pool_s))
```

```python
import collections
import functools

import jax
import jax.numpy as jnp
from jax import lax
from jax.experimental import pallas as pl
from jax.experimental.pallas import tpu as pltpu

D_MODEL = 1024
BATCH = 8
SEQ = 2048
DEPTH = 4
DEC_BATCH = 128
PAST_LEN = 16384

BRANCH = D_MODEL // 2
SSM_GROUP = 16
SSM_GROUPS = BRANCH // SSM_GROUP
SSM_STATE = 64
N_STATE = SSM_GROUPS * SSM_STATE
POOL_WINDOWS = (2, 4, 8, 16)
POOL_GROUP_WIDTH = BRANCH // len(POOL_WINDOWS)
POOL_HIST = max(POOL_WINDOWS) - 1
N_MEM = 256
MEM_HEADS = 4
MEM_HEAD_DIM = BRANCH // MEM_HEADS
KV_ROWS = N_MEM * MEM_HEADS
N_BRANCH = 3
GATE_HALF = N_BRANCH * D_MODEL // 2
D_FF = 2816
RMS_EPS = 1e-6

LANES = 128
SUBLANES = 8
GROUPS_PER_BLOCK = LANES // SSM_GROUP
N_SSM_BLOCKS = BRANCH // LANES
BLOCK_STATE = GROUPS_PER_BLOCK * SSM_STATE

ROWS_A = 512
STEPS_A = ROWS_A // BATCH
PIECE_ROWS = 128
PIECE_COLS = 256
ROWS_C = 512
SAMPLE_BLOCK = 8
FF_CHUNKS = ((0, 1536), (1536, 2816))
VMEM_LIMIT = 56 * 1024 * 1024

BF16 = jnp.bfloat16
F32 = jnp.float32


def _rms(x, g):
    ms = jnp.mean(x * x, axis=-1, keepdims=True)
    return x * lax.rsqrt(ms + RMS_EPS) * g


def _dot(a, b):
    return jnp.dot(a, b, preferred_element_type=F32)


def _const_spec(shape):
    nd = len(shape)
    return pl.BlockSpec(shape, lambda *_: (0,) * nd, pipeline_mode=pl.Buffered(1))


def _layer_spec(shape, layer):
    nd = len(shape)
    return pl.BlockSpec((None,) + tuple(shape), lambda *_: (layer,) + (0,) * nd,
                        pipeline_mode=pl.Buffered(1))


def _discretise_kernel(lr_ref, li_ref, ldt_ref, br_ref, bi_ref,
                       ar_ref, ai_ref, bbr_ref, bbi_ref):
    lr = lr_ref[...]
    li = li_ref[...]
    dt = jnp.exp(ldt_ref[...])
    zr = lr * dt
    zi = li * dt
    mag = jnp.exp(zr)
    ar = mag * jnp.cos(zi)
    ai = mag * jnp.sin(zi)
    den = lr * lr + li * li
    fr = ((ar - 1.0) * lr + ai * li) / den
    fi = (ai * lr - (ar - 1.0) * li) / den
    br = br_ref[...]
    bi = bi_ref[...]
    ar_ref[...] = ar
    ai_ref[...] = ai
    bbr_ref[...] = fr * br - fi * bi
    bbi_ref[...] = fr * bi + fi * br


def _discretise(lam_re, lam_im, log_dt, b_re, b_im):
    rows = DEPTH * SSM_GROUPS
    width = SSM_GROUP * SSM_STATE

    def tile_p(a):
        return jnp.tile(a.reshape(rows, 1, SSM_STATE), (1, SSM_GROUP, 1)).reshape(rows, width)

    def b_t(a):
        return jnp.swapaxes(a, -1, -2).reshape(rows, width)

    ldt = jnp.broadcast_to(log_dt.reshape(rows, 1), (rows, width))
    out = jax.ShapeDtypeStruct((rows, width), F32)
    ar, ai, bbr, bbi = pl.pallas_call(
        _discretise_kernel, out_shape=(out, out, out, out), name="s5_discretise",
    )(tile_p(lam_re), tile_p(lam_im), ldt, b_t(b_re), b_t(b_im))
    ar = ar[:, :SSM_STATE].reshape(DEPTH, 1, N_STATE)
    ai = ai[:, :SSM_STATE].reshape(DEPTH, 1, N_STATE)
    shape5 = (DEPTH, N_SSM_BLOCKS, GROUPS_PER_BLOCK, SSM_GROUP, SSM_STATE)
    eye = jnp.eye(GROUPS_PER_BLOCK, dtype=F32)

    def b_blocks(bb):
        m = jnp.einsum('lkghp,gG->lkghGp', bb.reshape(shape5), eye)
        return m.reshape(DEPTH, N_SSM_BLOCKS, LANES, BLOCK_STATE)

    bblk = jnp.concatenate([b_blocks(bbr), b_blocks(bbi)], axis=-1).astype(BF16)
    return ar, ai, bblk


def _c_blocks(c):
    shape5 = (DEPTH, N_SSM_BLOCKS, GROUPS_PER_BLOCK, SSM_GROUP, SSM_STATE)
    eye = jnp.eye(GROUPS_PER_BLOCK, dtype=F32)
    m = jnp.einsum('lkghp,gG->lkgpGh', c.reshape(shape5), eye)
    return m.reshape(DEPTH, N_SSM_BLOCKS, BLOCK_STATE, LANES)


def _ssm_input(u_block, bblk, bu_re_ref, bu_im_ref, sl):
    bu = _dot(u_block.astype(BF16), bblk)
    bu_re_ref[:, sl] = bu[:, :BLOCK_STATE]
    bu_im_ref[:, sl] = bu[:, BLOCK_STATE:]


def _ssm_readout(h_re_ref, h_im_ref, sl, cre, cimn):
    return _dot(h_re_ref[:, sl].astype(BF16), cre) + _dot(h_im_ref[:, sl].astype(BF16), cimn)


def _ssm_gate(y, u_ssm, d_ref, wglu_ref, bglu_ref):
    y = jax.nn.gelu(y + d_ref[...] * u_ssm)
    return y * jax.nn.sigmoid(_dot(y.astype(BF16), wglu_ref[...]) + bglu_ref[...])


def _pool_mix(pooled, poolw_ref, pscale_ref):
    outs = []
    for gi in range(len(POOL_WINDOWS)):
        sl = slice(gi * POOL_GROUP_WIDTH, (gi + 1) * POOL_GROUP_WIDTH)
        outs.append(_dot(pooled[gi].astype(BF16), poolw_ref[gi]) * pscale_ref[:, sl])
    return jnp.concatenate(outs, axis=-1)


def _mixer_a_kernel(xfirst_ref, xnext_ref, g_ref, win_ref, bblk_ref, cre_ref, cimn_ref,
                    ar_ref, ai_ref, d_ref, wglu_ref, bglu_ref, poolw_ref, pscale_ref,
                    wgate_lo_ref, wgate_hi_ref, wup_ref, wout_ref, wfin_ref, wfout_ref,
                    q_ref, ossm_ref, opool_ref, hre_ref, him_ref, hist_ref,
                    wgate_bf_ref, wup_bf_ref, wout_bf_ref, wfin_bf_ref, wfout_bf_ref,
                    bu_re_ref, bu_im_ref, pool_buf, ussm_tm, ossm_tm, opool_tm,
                    proj_scr, hnext_scr):
    i = pl.program_id(0)
    hist_rows = POOL_HIST * BATCH

    wgate_bf_ref[:, :GATE_HALF] = wgate_lo_ref[...].astype(BF16)
    wgate_bf_ref[:, GATE_HALF:] = wgate_hi_ref[...].astype(BF16)
    wup_bf_ref[...] = wup_ref[...].astype(BF16)
    wout_bf_ref[...] = wout_ref[...].astype(BF16)
    wfin_bf_ref[...] = wfin_ref[...].astype(BF16)
    wfout_bf_ref[...] = wfout_ref[...].astype(BF16)

    def normed(x_blk_ref):
        return _rms(x_blk_ref[...].reshape(ROWS_A, D_MODEL), g_ref[...]).astype(BF16)

    @pl.when(i == 0)
    def _():
        hre_ref[...] = jnp.zeros_like(hre_ref)
        him_ref[...] = jnp.zeros_like(him_ref)
        pool_buf[:, 0:hist_rows, :] = jnp.zeros((N_SSM_BLOCKS, hist_rows, LANES), F32)
        proj_scr[...] = _dot(normed(xfirst_ref), win_ref[...])

    row_pieces = [slice(r, r + PIECE_ROWS) for r in range(0, ROWS_A, PIECE_ROWS)]
    pending = collections.deque()

    def emit(n=1):
        for _ in range(n):
            if pending:
                pending.popleft()()

    def proj_piece(rows, cols):
        proj_scr[rows, cols] = _dot(hnext_scr[rows, :], win_ref[:, cols])

    def input_piece(k, rows):
        sl = slice(k * BLOCK_STATE, (k + 1) * BLOCK_STATE)
        bu = _dot(ussm_tm[k, rows, :].astype(BF16), bblk_ref[k])
        bu_re_ref[rows, sl] = bu[:, :BLOCK_STATE]
        bu_im_ref[rows, sl] = bu[:, BLOCK_STATE:]

    def readout_piece(k, rows):
        sl = slice(k * BLOCK_STATE, (k + 1) * BLOCK_STATE)
        ossm_tm[k, rows, :] = (_dot(bu_re_ref[rows, sl].astype(BF16), cre_ref[k])
                               + _dot(bu_im_ref[rows, sl].astype(BF16), cimn_ref[k]))

    seq_rows = [slice(b * STEPS_A, (b + 1) * STEPS_A) for b in range(BATCH)]
    for k in range(N_SSM_BLOCKS):
        for b in range(BATCH):
            ussm_tm[k, pl.ds(b, STEPS_A, stride=BATCH), :] = (
                proj_scr[seq_rows[b], k * LANES:(k + 1) * LANES])
        if k == 0:
            pending.extend(functools.partial(input_piece, 0, rows) for rows in row_pieces)
        emit()
    for k in range(N_SSM_BLOCKS):
        for b in range(BATCH):
            pool_buf[k, pl.ds(hist_rows + b, STEPS_A, stride=BATCH), :] = (
                proj_scr[seq_rows[b], BRANCH + k * LANES:BRANCH + (k + 1) * LANES])
    for b in range(BATCH):
        q_ref[b] = proj_scr[seq_rows[b], 2 * BRANCH:].astype(BF16)
    hnext_scr[...] = normed(xnext_ref)
    emit(len(pending))

    for k in range(N_SSM_BLOCKS):
        if k + 1 < N_SSM_BLOCKS:
            pending.extend(functools.partial(input_piece, k + 1, rows) for rows in row_pieces)
        if k > 0:
            pending.extend(functools.partial(readout_piece, k - 1, rows) for rows in row_pieces)
        every = -(-STEPS_A // len(pending))
        sl = slice(k * BLOCK_STATE, (k + 1) * BLOCK_STATE)
        a_r = jnp.broadcast_to(ar_ref[:, sl], (BATCH, BLOCK_STATE))
        a_i = jnp.broadcast_to(ai_ref[:, sl], (BATCH, BLOCK_STATE))
        hr = hre_ref[:, sl]
        hi = him_ref[:, sl]
        for t in range(STEPS_A):
            rows = slice(t * BATCH, (t + 1) * BATCH)
            hr, hi = (a_r * hr - a_i * hi + bu_re_ref[rows, sl],
                      a_r * hi + a_i * hr + bu_im_ref[rows, sl])
            bu_re_ref[rows, sl] = hr
            bu_im_ref[rows, sl] = hi
            if (t + 1) % every == 0:
                emit()
        emit(len(pending))
        hre_ref[:, sl] = hr
        him_ref[:, sl] = hi

    pending.extend(functools.partial(proj_piece, rows, slice(c, c + PIECE_COLS))
                   for c in range(0, 3 * BRANCH, PIECE_COLS) for rows in row_pieces)
    n_proj = len(pending)
    for rows in row_pieces:
        readout_piece(N_SSM_BLOCKS - 1, rows)
        y = jnp.concatenate([ossm_tm[k, rows, :] for k in range(N_SSM_BLOCKS)], axis=-1)
        u = jnp.concatenate([ussm_tm[k, rows, :] for k in range(N_SSM_BLOCKS)], axis=-1)
        o_ssm = _ssm_gate(y, u, d_ref, wglu_ref, bglu_ref)
        for k in range(N_SSM_BLOCKS):
            ossm_tm[k, rows, :] = o_ssm[:, k * LANES:(k + 1) * LANES]
        emit(n_proj // (2 * len(row_pieces)))

    t_pos = i * STEPS_A + lax.broadcasted_iota(jnp.int32, (ROWS_A, POOL_GROUP_WIDTH), 0) // BATCH
    for gi, w in enumerate(POOL_WINDOWS):
        u_g = pool_buf[gi, hist_rows:hist_rows + ROWS_A, :]
        acc = u_g
        for j in range(1, w):
            start = hist_rows - j * BATCH
            acc = acc + pool_buf[gi, start:start + ROWS_A, :]
        cnt = jnp.minimum(t_pos + 1, w).astype(F32)
        pooled = (acc / cnt - u_g).astype(BF16)
        opool_tm[gi] = (_dot(pooled, poolw_ref[gi])
                        * pscale_ref[:, gi * POOL_GROUP_WIDTH:(gi + 1) * POOL_GROUP_WIDTH])
        emit(n_proj // (3 * len(POOL_WINDOWS)))
    pool_buf[:, 0:hist_rows, :] = pool_buf[:, ROWS_A:ROWS_A + hist_rows, :]

    for b in range(BATCH):
        rows = pl.ds(b, STEPS_A, stride=BATCH)
        ossm_ref[b] = jnp.concatenate(
            [ossm_tm[k, rows, :] for k in range(N_SSM_BLOCKS)], axis=-1).astype(BF16)
        opool_ref[b] = jnp.concatenate(
            [opool_tm[k, rows, :] for k in range(N_SSM_BLOCKS)], axis=-1).astype(BF16)
        emit()
    emit(len(pending))

    @pl.when(i == pl.num_programs(0) - 1)
    def _():
        for k in range(N_SSM_BLOCKS):
            hist_ref[:, k * LANES:(k + 1) * LANES] = pool_buf[k, 0:hist_rows, :]


def _mixer_a(layer, x, p, raw):
    hist_rows = POOL_HIST * BATCH
    n_steps = SEQ // STEPS_A
    seq_spec = lambda w: pl.BlockSpec((BATCH, STEPS_A, w), lambda i: (0, i, 0))
    cast_plan = {
        'w_gate': (D_MODEL // n_steps, N_BRANCH * D_MODEL, 1),
        'w_up': (N_BRANCH * BRANCH // n_steps, D_MODEL, 1),
        'w_out': (D_MODEL // n_steps, D_MODEL, 1),
        'w_ffn_in': (D_MODEL // n_steps, 2 * D_FF, 1),
        'w_ffn_out': (2 * D_FF // n_steps, D_MODEL, 2),
    }

    def chunk_in(name, width=None, col=0):
        rows, full_width, per = cast_plan[name]
        return pl.BlockSpec((None, rows, width or full_width),
                            lambda i: (layer, i // per, col))

    def chunk_out(name):
        rows, width, per = cast_plan[name]
        return pl.BlockSpec((rows, width), lambda i: (i // per, 0))

    cast_names = ('w_gate', 'w_up', 'w_out', 'w_ffn_in', 'w_ffn_out')
    cast_shapes = tuple(
        jax.ShapeDtypeStruct((cast_plan[n][0] * n_steps // cast_plan[n][2], cast_plan[n][1]), BF16)
        for n in cast_names)
    in_specs = [
        pl.BlockSpec((BATCH, STEPS_A, D_MODEL), lambda i: (0, 0, 0), pipeline_mode=pl.Buffered(1)),
        pl.BlockSpec((BATCH, STEPS_A, D_MODEL), lambda i: (0, jnp.minimum(i + 1, n_steps - 1), 0)),
        _layer_spec((1, D_MODEL), layer),
        _layer_spec((D_MODEL, 3 * BRANCH), layer),
        _layer_spec((N_SSM_BLOCKS, LANES, 2 * BLOCK_STATE), layer),
        _layer_spec((N_SSM_BLOCKS, BLOCK_STATE, LANES), layer),
        _layer_spec((N_SSM_BLOCKS, BLOCK_STATE, LANES), layer),
        _layer_spec((1, N_STATE), layer),
        _layer_spec((1, N_STATE), layer),
        _layer_spec((1, BRANCH), layer),
        _layer_spec((BRANCH, BRANCH), layer),
        _layer_spec((1, BRANCH), layer),
        _layer_spec((len(POOL_WINDOWS), POOL_GROUP_WIDTH, POOL_GROUP_WIDTH), layer),
        _layer_spec((1, BRANCH), layer),
        chunk_in('w_gate', GATE_HALF, 1), chunk_in('w_gate', GATE_HALF, 2),
        chunk_in('w_up'), chunk_in('w_out'), chunk_in('w_ffn_in'), chunk_in('w_ffn_out'),
    ]
    branch_out = jax.ShapeDtypeStruct((BATCH, SEQ, BRANCH), BF16)
    out_shape = (
        branch_out, branch_out, branch_out,
        jax.ShapeDtypeStruct((BATCH, N_STATE), F32),
        jax.ShapeDtypeStruct((BATCH, N_STATE), F32),
        jax.ShapeDtypeStruct((hist_rows, BRANCH), F32),
    ) + cast_shapes
    out_specs = (
        seq_spec(BRANCH), seq_spec(BRANCH), seq_spec(BRANCH),
        pl.BlockSpec((BATCH, N_STATE), lambda i: (0, 0)),
        pl.BlockSpec((BATCH, N_STATE), lambda i: (0, 0)),
        pl.BlockSpec((hist_rows, BRANCH), lambda i: (0, 0)),
    ) + tuple(chunk_out(n) for n in cast_names)
    outs = pl.pallas_call(
        _mixer_a_kernel,
        grid=(n_steps,),
        in_specs=in_specs, out_specs=out_specs, out_shape=out_shape,
        scratch_shapes=[
            pltpu.VMEM((ROWS_A, N_STATE), F32),
            pltpu.VMEM((ROWS_A, N_STATE), F32),
            pltpu.VMEM((N_SSM_BLOCKS, hist_rows + ROWS_A, LANES), F32),
            pltpu.VMEM((N_SSM_BLOCKS, ROWS_A, LANES), F32),
            pltpu.VMEM((N_SSM_BLOCKS, ROWS_A, LANES), F32),
            pltpu.VMEM((N_SSM_BLOCKS, ROWS_A, LANES), F32),
            pltpu.VMEM((ROWS_A, 3 * BRANCH), F32),
            pltpu.VMEM((ROWS_A, D_MODEL), BF16),
        ],
        compiler_params=pltpu.CompilerParams(
            dimension_semantics=("arbitrary",), vmem_limit_bytes=VMEM_LIMIT),
        name="prompt_mixer_a",
    )(x, x, p['g_mix_pre'], p['w_in_a'], p['bblk'], p['cre'], p['cimn'], p['ar'], p['ai'],
      p['ssm_d'], p['w_glu'], p['b_glu'], p['pool_w'], p['pool_scale'],
      raw['w_in'], raw['w_in'], raw['w_up'], raw['w_out'], raw['w_ffn_in'], raw['w_ffn_out'])
    return outs[:6], dict(zip(cast_names, outs[6:]))


def _mem_kv_kernel(mem_ref, g_ref, wkv_ref, k_ref, v_ref):
    for b in range(BATCH):
        kv = _dot(_rms(mem_ref[b], g_ref[...]).astype(BF16), wkv_ref[...])
        for hd in range(MEM_HEADS):
            rows = pl.ds(hd, N_MEM, stride=MEM_HEADS)
            k_ref[b, rows, :] = kv[:, hd * MEM_HEAD_DIM:(hd + 1) * MEM_HEAD_DIM]
            v_ref[b, rows, :] = kv[:, BRANCH + hd * MEM_HEAD_DIM:BRANCH + (hd + 1) * MEM_HEAD_DIM]


def _mem_kv(mem, g_mem, w_kv):
    out = jax.ShapeDtypeStruct((DEPTH, BATCH, KV_ROWS, MEM_HEAD_DIM), F32)
    out_spec = pl.BlockSpec((None, BATCH, KV_ROWS, MEM_HEAD_DIM), lambda l: (l, 0, 0, 0))
    return pl.pallas_call(
        _mem_kv_kernel,
        grid=(DEPTH,),
        in_specs=[
            _const_spec((BATCH, N_MEM, D_MODEL)),
            pl.BlockSpec((None, 1, D_MODEL), lambda l: (l, 0, 0)),
            pl.BlockSpec((None, D_MODEL, 2 * BRANCH), lambda l: (l, 0, 0)),
        ],
        out_specs=(out_spec, out_spec), out_shape=(out, out),
        compiler_params=pltpu.CompilerParams(
            dimension_semantics=("arbitrary",), vmem_limit_bytes=VMEM_LIMIT),
        name="mem_kv",
    )(mem, g_mem, w_kv)


def _memory_attention(q_ref, k_ref, v_ref):
    scale = MEM_HEAD_DIM ** -0.5
    outs = []
    for hd in range(MEM_HEADS):
        sl = slice(hd * MEM_HEAD_DIM, (hd + 1) * MEM_HEAD_DIM)
        head_rows = pl.ds(hd, N_MEM, stride=MEM_HEADS)
        s = lax.dot_general(q_ref[:, sl], k_ref[head_rows, :].astype(BF16),
                            (((1,), (1,)), ((), ())), preferred_element_type=F32) * scale
        e = jnp.exp(s - jnp.max(s, axis=-1, keepdims=True))
        prob = e / jnp.sum(e, axis=-1, keepdims=True)
        outs.append(_dot(prob.astype(BF16), v_ref[head_rows, :].astype(BF16)).astype(BF16))
    return jnp.concatenate(outs, axis=-1)


def _merge_ffn_kernel(with_attention, x_ref, ossm_ref, opool_ref, *refs):
    if with_attention:
        q_ref, k_ref, v_ref = refs[:3]
        refs = refs[3:]
        o_mem = _memory_attention(q_ref, k_ref, v_ref)
    else:
        o_mem = refs[0][...]
        refs = refs[1:]
    (gpre_ref, wgate_ref, wup_ref, wout_ref, gpost_ref, gfpre_ref, wfin_ref, wfout_ref,
     gfpost_ref, y_ref) = refs
    x = x_ref[...]
    h = _rms(x, gpre_ref[...]).astype(BF16)
    merged = None
    for b, o_b in enumerate((ossm_ref[...], opool_ref[...], o_mem)):
        gate = jax.nn.sigmoid(_dot(h, wgate_ref[:, b * D_MODEL:(b + 1) * D_MODEL]))
        term = gate * _dot(o_b, wup_ref[b * BRANCH:(b + 1) * BRANCH, :])
        merged = term if merged is None else merged + term
    x = x + _rms(_dot(merged.astype(BF16), wout_ref[...]), gpost_ref[...])
    hf = _rms(x, gfpre_ref[...]).astype(BF16)
    f = None
    for lo, hi in FF_CHUNKS:
        hg = _dot(hf, wfin_ref[:, lo:hi])
        hu = _dot(hf, wfin_ref[:, D_FF + lo:D_FF + hi])
        part = _dot((jax.nn.silu(hg) * hu).astype(BF16), wfout_ref[lo:hi, :])
        f = part if f is None else f + part
    y_ref[...] = x + _rms(f, gfpost_ref[...])


def _merge_ffn(layer, x, o_ssm, o_pool, third, p, w):
    n_rows = x.shape[0]
    rows = min(ROWS_C, n_rows)
    row_spec = lambda w: pl.BlockSpec((rows, w), lambda i: (i, 0))
    with_attention = isinstance(third, tuple)
    if with_attention:
        tiles_per_seq = SEQ // rows
        kv_spec = pl.BlockSpec((None, None, KV_ROWS, MEM_HEAD_DIM),
                               lambda i: (layer, i // tiles_per_seq, 0, 0))
        third_specs = [row_spec(BRANCH), kv_spec, kv_spec]
    else:
        third = (third,)
        third_specs = [row_spec(BRANCH)]
    in_specs = [row_spec(D_MODEL), row_spec(BRANCH), row_spec(BRANCH)] + third_specs + [
        _layer_spec((1, D_MODEL), layer),
        _const_spec((D_MODEL, N_BRANCH * D_MODEL)),
        _const_spec((N_BRANCH * BRANCH, D_MODEL)),
        _const_spec((D_MODEL, D_MODEL)),
        _layer_spec((1, D_MODEL), layer),
        _layer_spec((1, D_MODEL), layer),
        _const_spec((D_MODEL, 2 * D_FF)),
        _const_spec((D_FF, D_MODEL)),
        _layer_spec((1, D_MODEL), layer),
    ]
    return pl.pallas_call(
        functools.partial(_merge_ffn_kernel, with_attention),
        grid=(n_rows // rows,),
        in_specs=in_specs, out_specs=row_spec(D_MODEL),
        out_shape=jax.ShapeDtypeStruct((n_rows, D_MODEL), F32),
        compiler_params=pltpu.CompilerParams(
            dimension_semantics=("parallel",), vmem_limit_bytes=VMEM_LIMIT),
        name="merge_ffn",
    )(x, o_ssm, o_pool, *third, p['g_mix_pre'], w['w_gate'], w['w_up'], w['w_out'],
      p['g_mix_post'], p['g_ffn_pre'], w['w_ffn_in'], w['w_ffn_out'], p['g_ffn_post'])


def _sample_mixer_kernel(x_ref, g_ref, win_ref, bblk_ref, cre_ref, cimn_ref, ar_ref, ai_ref,
                         d_ref, wglu_ref, bglu_ref, poolw_ref, pscale_ref,
                         h0re_ref, h0im_ref, hist_ref,
                         q_ref, ossm_ref, opool_ref, hre_ref, him_ref, nhist_ref,
                         bu_re_ref, bu_im_ref):
    h = _rms(x_ref[...], g_ref[...]).astype(BF16)
    proj = _dot(h, win_ref[...])
    u_ssm = proj[:, :BRANCH]
    u_pool = proj[:, BRANCH:2 * BRANCH]
    q_ref[...] = proj[:, 2 * BRANCH:]

    ys = []
    for k in range(N_SSM_BLOCKS):
        sl = slice(k * BLOCK_STATE, (k + 1) * BLOCK_STATE)
        _ssm_input(u_ssm[:, k * LANES:(k + 1) * LANES], bblk_ref[k], bu_re_ref, bu_im_ref, sl)
        a_r = ar_ref[:, sl]
        a_i = ai_ref[:, sl]
        h0r = h0re_ref[:, sl]
        h0i = h0im_ref[:, sl]
        hre_ref[:, sl] = bu_re_ref[:, sl] + a_r * h0r - a_i * h0i
        him_ref[:, sl] = bu_im_ref[:, sl] + a_r * h0i + a_i * h0r
        ys.append(_ssm_readout(hre_ref, him_ref, sl, cre_ref[k], cimn_ref[k]))
    y = _ssm_gate(jnp.concatenate(ys, axis=-1), u_ssm, d_ref, wglu_ref, bglu_ref)
    ossm_ref[...] = y.astype(BF16)

    pooled = []
    for gi, w in enumerate(POOL_WINDOWS):
        sl = slice(gi * POOL_GROUP_WIDTH, (gi + 1) * POOL_GROUP_WIDTH)
        acc = u_pool[:, sl]
        for j in range(1, w):
            acc = acc + hist_ref[POOL_HIST - j, :, sl]
        cnt = float(min(PAST_LEN + 1, w))
        pooled.append(acc / cnt - u_pool[:, sl])
    opool_ref[...] = _pool_mix(pooled, poolw_ref, pscale_ref).astype(BF16)
    for j in range(POOL_HIST - 1):
        nhist_ref[j] = hist_ref[j + 1]
    nhist_ref[POOL_HIST - 1] = u_pool


def _sample_mixer(layer, x_s, p, h0_re, h0_im, hist_t):
    n = x_s.shape[0]
    full = lambda shape: pl.BlockSpec(shape, lambda i: (0,) * len(shape))
    in_specs = [
        full((n, D_MODEL)),
        _layer_spec((1, D_MODEL), layer),
        _layer_spec((D_MODEL, 3 * BRANCH), layer),
        _layer_spec((N_SSM_BLOCKS, LANES, 2 * BLOCK_STATE), layer),
        _layer_spec((N_SSM_BLOCKS, BLOCK_STATE, LANES), layer),
        _layer_spec((N_SSM_BLOCKS, BLOCK_STATE, LANES), layer),
        _layer_spec((1, N_STATE), layer),
        _layer_spec((1, N_STATE), layer),
        _layer_spec((1, BRANCH), layer),
        _layer_spec((BRANCH, BRANCH), layer),
        _layer_spec((1, BRANCH), layer),
        _layer_spec((len(POOL_WINDOWS), POOL_GROUP_WIDTH, POOL_GROUP_WIDTH), layer),
        _layer_spec((1, BRANCH), layer),
        _layer_spec((n, N_STATE), layer),
        _layer_spec((n, N_STATE), layer),
        _layer_spec((POOL_HIST, n, BRANCH), layer),
    ]
    out_shape = (
        jax.ShapeDtypeStruct((n, BRANCH), F32),
        jax.ShapeDtypeStruct((n, BRANCH), BF16),
        jax.ShapeDtypeStruct((n, BRANCH), BF16),
        jax.ShapeDtypeStruct((n, N_STATE), F32),
        jax.ShapeDtypeStruct((n, N_STATE), F32),
        jax.ShapeDtypeStruct((POOL_HIST, n, BRANCH), F32),
    )
    out_specs = tuple(full(s.shape) for s in out_shape)
    return pl.pallas_call(
        _sample_mixer_kernel,
        grid=(1,),
        in_specs=in_specs, out_specs=out_specs, out_shape=out_shape,
        scratch_shapes=[pltpu.VMEM((n, N_STATE), F32), pltpu.VMEM((n, N_STATE), F32)],
        compiler_params=pltpu.CompilerParams(
            dimension_semantics=("arbitrary",), vmem_limit_bytes=VMEM_LIMIT),
        name="sample_mixer",
    )(x_s, p['g_mix_pre'], p['w_in_a'], p['bblk'], p['cre'], p['cimn'], p['ar'], p['ai'],
      p['ssm_d'], p['w_glu'], p['b_glu'], p['pool_w'], p['pool_scale'], h0_re, h0_im, hist_t)


def _sample_attention_kernel(q_ref, k_ref, v_ref, ones_ref, o_ref):
    tiles = KV_ROWS // SUBLANES
    scale = MEM_HEAD_DIM ** -0.5

    def one_sample(b, carry):
        q8 = q_ref[b] * scale
        prod = k_ref[b].reshape(tiles, SUBLANES, MEM_HEAD_DIM) * q8[None]
        prod = prod.reshape(KV_ROWS, MEM_HEAD_DIM).astype(BF16)
        s = _dot(prod, ones_ref[...]).reshape(tiles, SUBLANES, MEM_HEAD_DIM)
        m8 = jnp.max(s, axis=0)
        m8 = jnp.maximum(m8, pltpu.roll(m8, MEM_HEADS, 0))
        e = jnp.exp(s - m8[None])
        l8 = jnp.sum(e, axis=0)
        acc = jnp.sum(e * v_ref[b].reshape(tiles, SUBLANES, MEM_HEAD_DIM), axis=0)
        l8 = l8 + pltpu.roll(l8, MEM_HEADS, 0)
        acc = acc + pltpu.roll(acc, MEM_HEADS, 0)
        o_ref[b] = acc / l8
        return carry

    lax.fori_loop(0, SAMPLE_BLOCK, one_sample, 0, unroll=True)


def _sample_attention(layer, q8, k_cache, v_cache, lane_ones):
    n = q8.shape[0]
    kv_spec = pl.BlockSpec((None, SAMPLE_BLOCK, KV_ROWS, MEM_HEAD_DIM), lambda i: (layer, i, 0, 0))
    q_spec = pl.BlockSpec((SAMPLE_BLOCK, SUBLANES, MEM_HEAD_DIM), lambda i: (i, 0, 0))
    return pl.pallas_call(
        _sample_attention_kernel,
        grid=(n // SAMPLE_BLOCK,),
        in_specs=[q_spec, kv_spec, kv_spec, _const_spec((MEM_HEAD_DIM, MEM_HEAD_DIM))],
        out_specs=q_spec,
        out_shape=jax.ShapeDtypeStruct((n, SUBLANES, MEM_HEAD_DIM), F32),
        compiler_params=pltpu.CompilerParams(
            dimension_semantics=("parallel",), vmem_limit_bytes=VMEM_LIMIT),
        name="sample_attention",
    )(q8, k_cache, v_cache, lane_ones)


def kernel(x_prompt, x_sample, mem_prompt, cache_mem_k, cache_mem_v, state_ssm_re, state_ssm_im, state_pool, g_mix_pre, g_mix_post, g_ffn_pre, g_ffn_post, g_mem, w_in, w_kv, ssm_lam_re, ssm_lam_im, ssm_log_dt, ssm_b_re, ssm_b_im, ssm_c_re, ssm_c_im, ssm_d, ssm_w_glu, ssm_b_glu, pool_w, pool_scale, w_branch_up, w_out, w_ffn_in, w_ffn_out):
    ar, ai, bblk = _discretise(ssm_lam_re, ssm_lam_im, ssm_log_dt, ssm_b_re, ssm_b_im)
    vec = lambda a: a.reshape(DEPTH, 1, a.shape[-1])
    p = {
        'g_mix_pre': vec(g_mix_pre), 'g_mix_post': vec(g_mix_post),
        'g_ffn_pre': vec(g_ffn_pre), 'g_ffn_post': vec(g_ffn_post),
        'w_in_a': w_in[:, :, :3 * BRANCH].astype(BF16),
        'bblk': bblk, 'ar': ar, 'ai': ai,
        'cre': _c_blocks(ssm_c_re).astype(BF16),
        'cimn': _c_blocks(-ssm_c_im).astype(BF16),
        'ssm_d': vec(ssm_d), 'w_glu': ssm_w_glu.astype(BF16), 'b_glu': vec(ssm_b_glu),
        'pool_w': pool_w.astype(BF16), 'pool_scale': vec(pool_scale),
    }
    raw = {
        'w_in': w_in, 'w_up': w_branch_up.reshape(DEPTH, N_BRANCH * BRANCH, D_MODEL),
        'w_out': w_out, 'w_ffn_in': w_ffn_in, 'w_ffn_out': w_ffn_out,
    }
    k_mem, v_mem = _mem_kv(mem_prompt, vec(g_mem), w_kv.astype(BF16))

    lane_ones = jnp.ones((MEM_HEAD_DIM, MEM_HEAD_DIM), BF16)
    k_cache = cache_mem_k.reshape(DEPTH, DEC_BATCH, KV_ROWS, MEM_HEAD_DIM)
    v_cache = cache_mem_v.reshape(DEPTH, DEC_BATCH, KV_ROWS, MEM_HEAD_DIM)
    h0_re = state_ssm_re.reshape(DEPTH, DEC_BATCH, N_STATE)
    h0_im = state_ssm_im.reshape(DEPTH, DEC_BATCH, N_STATE)
    hist_t = jnp.swapaxes(state_pool, 1, 2)

    xp = x_prompt
    xs = x_sample.reshape(DEC_BATCH, D_MODEL)
    n_prompt = BATCH * SEQ
    re_p, im_p, pool_p, re_s, im_s, pool_s = [], [], [], [], [], []
    for layer in range(DEPTH):
        (q, o_ssm, o_pool, hre, him, hist), w_bf = _mixer_a(layer, xp, p, raw)
        xp = _merge_ffn(layer, xp.reshape(n_prompt, D_MODEL), o_ssm.reshape(n_prompt, BRANCH),
                        o_pool.reshape(n_prompt, BRANCH),
                        (q.reshape(n_prompt, BRANCH), k_mem, v_mem), p, w_bf)
        xp = xp.reshape(BATCH, SEQ, D_MODEL)
        re_p.append(hre.reshape(BATCH, SSM_GROUPS, SSM_STATE))
        im_p.append(him.reshape(BATCH, SSM_GROUPS, SSM_STATE))
        pool_p.append(jnp.swapaxes(hist.reshape(POOL_HIST, BATCH, BRANCH), 0, 1))

        qs, os_ssm, os_pool, hre, him, nhist = _sample_mixer(layer, xs, p, h0_re, h0_im, hist_t)
        qs4 = qs.reshape(DEC_BATCH, MEM_HEADS, MEM_HEAD_DIM)
        os_mem = _sample_attention(layer, jnp.concatenate([qs4, qs4], axis=1),
                                   k_cache, v_cache, lane_ones)
        os_mem = os_mem[:, :MEM_HEADS].reshape(DEC_BATCH, BRANCH).astype(BF16)
        xs = _merge_ffn(layer, xs, os_ssm, os_pool, os_mem, p, w_bf)
        re_s.append(hre.reshape(DEC_BATCH, SSM_GROUPS, SSM_STATE))
        im_s.append(him.reshape(DEC_BATCH, SSM_GROUPS, SSM_STATE))
        pool_s.append(jnp.swapaxes(nhist, 0, 1))

    y_prompt = xp
    y_sample = xs.reshape(DEC_BATCH, 1, D_MODEL)
    kv_shape = (DEPTH, BATCH, N_MEM, MEM_HEADS, MEM_HEAD_DIM)
    return (y_prompt, y_sample,
            jnp.stack(re_p), jnp.stack(im_p), jnp.stack(pool_p),
            k_mem.reshape(kv_shape), v_mem.reshape(kv_shape),
            jnp.stack(re_s), jnp.stack(im_s), jnp.stack(pool_s))
```

```python
import functools

import jax
import jax.numpy as jnp
from jax import lax
from jax.experimental import pallas as pl
from jax.experimental.pallas import tpu as pltpu

D_MODEL = 1024
BATCH = 8
SEQ = 2048
DEPTH = 4
DEC_BATCH = 128
PAST_LEN = 16384

BRANCH = D_MODEL // 2
SSM_GROUP = 16
SSM_GROUPS = BRANCH // SSM_GROUP
SSM_STATE = 64
N_STATE = SSM_GROUPS * SSM_STATE
POOL_WINDOWS = (2, 4, 8, 16)
POOL_GROUP_WIDTH = BRANCH // len(POOL_WINDOWS)
POOL_HIST = max(POOL_WINDOWS) - 1
N_MEM = 256
MEM_HEADS = 4
MEM_HEAD_DIM = BRANCH // MEM_HEADS
KV_ROWS = N_MEM * MEM_HEADS
N_BRANCH = 3
GATE_HALF = N_BRANCH * D_MODEL // 2
D_FF = 2816
RMS_EPS = 1e-6

LANES = 128
SUBLANES = 8
GROUPS_PER_BLOCK = LANES // SSM_GROUP
N_SSM_BLOCKS = BRANCH // LANES
BLOCK_STATE = GROUPS_PER_BLOCK * SSM_STATE

ROWS_A = 512
STEPS_A = ROWS_A // BATCH
ROWS_C = 512
SAMPLE_BLOCK = 16
FF_CHUNKS = ((0, 1536), (1536, 2816))
VMEM_LIMIT = 56 * 1024 * 1024

BF16 = jnp.bfloat16
F32 = jnp.float32


def _rms(x, g):
    ms = jnp.mean(x * x, axis=-1, keepdims=True)
    return x * lax.rsqrt(ms + RMS_EPS) * g


def _dot(a, b):
    return jnp.dot(a, b, preferred_element_type=F32)


def _const_spec(shape):
    nd = len(shape)
    return pl.BlockSpec(shape, lambda *_: (0,) * nd, pipeline_mode=pl.Buffered(1))


def _layer_spec(shape, layer):
    nd = len(shape)
    return pl.BlockSpec((None,) + tuple(shape), lambda *_: (layer,) + (0,) * nd,
                        pipeline_mode=pl.Buffered(1))


def _discretise_kernel(lr_ref, li_ref, ldt_ref, br_ref, bi_ref,
                       ar_ref, ai_ref, bbr_ref, bbi_ref):
    lr = lr_ref[...]
    li = li_ref[...]
    dt = jnp.exp(ldt_ref[...])
    zr = lr * dt
    zi = li * dt
    mag = jnp.exp(zr)
    ar = mag * jnp.cos(zi)
    ai = mag * jnp.sin(zi)
    den = lr * lr + li * li
    fr = ((ar - 1.0) * lr + ai * li) / den
    fi = (ai * lr - (ar - 1.0) * li) / den
    br = br_ref[...]
    bi = bi_ref[...]
    ar_ref[...] = ar
    ai_ref[...] = ai
    bbr_ref[...] = fr * br - fi * bi
    bbi_ref[...] = fr * bi + fi * br


def _discretise(lam_re, lam_im, log_dt, b_re, b_im):
    rows = DEPTH * SSM_GROUPS
    width = SSM_GROUP * SSM_STATE

    def tile_p(a):
        return jnp.tile(a.reshape(rows, 1, SSM_STATE), (1, SSM_GROUP, 1)).reshape(rows, width)

    def b_t(a):
        return jnp.swapaxes(a, -1, -2).reshape(rows, width)

    ldt = jnp.broadcast_to(log_dt.reshape(rows, 1), (rows, width))
    out = jax.ShapeDtypeStruct((rows, width), F32)
    ar, ai, bbr, bbi = pl.pallas_call(
        _discretise_kernel, out_shape=(out, out, out, out), name="s5_discretise",
    )(tile_p(lam_re), tile_p(lam_im), ldt, b_t(b_re), b_t(b_im))
    ar = ar[:, :SSM_STATE].reshape(DEPTH, 1, N_STATE)
    ai = ai[:, :SSM_STATE].reshape(DEPTH, 1, N_STATE)
    shape5 = (DEPTH, N_SSM_BLOCKS, GROUPS_PER_BLOCK, SSM_GROUP, SSM_STATE)
    bblk = jnp.concatenate([_group_block_diag(bbr.reshape(shape5)),
                            _group_block_diag(bbi.reshape(shape5))], axis=-1)
    return ar, ai, bblk


def _group_block_diag(m):
    depth, blocks, groups, n_r, n_c = m.shape
    same = jnp.eye(groups, dtype=bool)[None, None, :, None, :, None]
    out = jnp.where(same, m[:, :, :, :, None, :], 0.0).astype(BF16)
    return out.reshape(depth, blocks, groups * n_r, groups * n_c)


def _c_blocks(c):
    shape5 = (DEPTH, N_SSM_BLOCKS, GROUPS_PER_BLOCK, SSM_GROUP, SSM_STATE)
    return _group_block_diag(jnp.swapaxes(c.reshape(shape5), -1, -2))


def _ssm_input(u_block, bblk, bu_re_ref, bu_im_ref, sl):
    bu = _dot(u_block.astype(BF16), bblk)
    bu_re_ref[:, sl] = bu[:, :BLOCK_STATE]
    bu_im_ref[:, sl] = bu[:, BLOCK_STATE:]


def _ssm_readout(h_re_ref, h_im_ref, sl, cre, cimn):
    return _dot(h_re_ref[:, sl].astype(BF16), cre) + _dot(h_im_ref[:, sl].astype(BF16), cimn)


def _ssm_gate(y, u_ssm, d_ref, wglu_ref, bglu_ref):
    y = jax.nn.gelu(y + d_ref[...] * u_ssm)
    return y * jax.nn.sigmoid(_dot(y.astype(BF16), wglu_ref[...]) + bglu_ref[...])


def _pool_mix(pooled, poolw_ref, pscale_ref):
    outs = []
    for gi in range(len(POOL_WINDOWS)):
        sl = slice(gi * POOL_GROUP_WIDTH, (gi + 1) * POOL_GROUP_WIDTH)
        outs.append(_dot(pooled[gi].astype(BF16), poolw_ref[gi]) * pscale_ref[:, sl])
    return jnp.concatenate(outs, axis=-1)


def _mixer_a_kernel(x_ref, g_ref, win_ref, bblk_ref, cre_ref, cimn_ref, ar_ref, ai_ref,
                    d_ref, wglu_ref, bglu_ref, poolw_ref, pscale_ref,
                    wgate_lo_ref, wgate_hi_ref, wup_ref, wout_ref, wfin_ref, wfout_ref,
                    q_ref, ossm_ref, opool_ref, hre_ref, him_ref, hist_ref,
                    wgate_bf_ref, wup_bf_ref, wout_bf_ref, wfin_bf_ref, wfout_bf_ref,
                    bu_re_ref, bu_im_ref, pool_buf, ussm_tm, ossm_tm, opool_tm, win_bf):
    i = pl.program_id(0)
    hist_rows = POOL_HIST * BATCH

    wgate_bf_ref[:, :GATE_HALF] = wgate_lo_ref[...].astype(BF16)
    wgate_bf_ref[:, GATE_HALF:] = wgate_hi_ref[...].astype(BF16)
    wup_bf_ref[...] = wup_ref[...].astype(BF16)
    wout_bf_ref[...] = wout_ref[...].astype(BF16)
    wfin_bf_ref[...] = wfin_ref[...].astype(BF16)
    wfout_bf_ref[...] = wfout_ref[...].astype(BF16)

    @pl.when(i == 0)
    def _():
        hre_ref[...] = jnp.zeros_like(hre_ref)
        him_ref[...] = jnp.zeros_like(him_ref)
        pool_buf[:, 0:hist_rows, :] = jnp.zeros((N_SSM_BLOCKS, hist_rows, LANES), F32)
        win_bf[...] = win_ref[...].astype(BF16)

    x = x_ref[...].reshape(ROWS_A, D_MODEL)
    h = _rms(x, g_ref[...]).astype(BF16)
    proj = _dot(h, win_bf[...])
    for b in range(BATCH):
        rows = slice(b * STEPS_A, (b + 1) * STEPS_A)
        q_ref[b] = proj[rows, 2 * BRANCH:].astype(BF16)
        for k in range(N_SSM_BLOCKS):
            ussm_tm[k, pl.ds(b, STEPS_A, stride=BATCH), :] = proj[rows, k * LANES:(k + 1) * LANES]
            pool_buf[k, pl.ds(hist_rows + b, STEPS_A, stride=BATCH), :] = (
                proj[rows, BRANCH + k * LANES:BRANCH + (k + 1) * LANES])
    u_ssm = jnp.concatenate([ussm_tm[k] for k in range(N_SSM_BLOCKS)], axis=-1)

    ys = []
    for k in range(N_SSM_BLOCKS):
        sl = slice(k * BLOCK_STATE, (k + 1) * BLOCK_STATE)
        _ssm_input(ussm_tm[k], bblk_ref[k], bu_re_ref, bu_im_ref, sl)
        a_r = jnp.broadcast_to(ar_ref[:, sl], (BATCH, BLOCK_STATE))
        a_i = jnp.broadcast_to(ai_ref[:, sl], (BATCH, BLOCK_STATE))
        hr = hre_ref[:, sl]
        hi = him_ref[:, sl]
        for t in range(STEPS_A):
            rows = slice(t * BATCH, (t + 1) * BATCH)
            hr, hi = (a_r * hr - a_i * hi + bu_re_ref[rows, sl],
                      a_r * hi + a_i * hr + bu_im_ref[rows, sl])
            bu_re_ref[rows, sl] = hr
            bu_im_ref[rows, sl] = hi
        hre_ref[:, sl] = hr
        him_ref[:, sl] = hi
        ys.append(_ssm_readout(bu_re_ref, bu_im_ref, sl, cre_ref[k], cimn_ref[k]))
    o_ssm = _ssm_gate(jnp.concatenate(ys, axis=-1), u_ssm, d_ref, wglu_ref, bglu_ref)

    t_pos = i * STEPS_A + lax.broadcasted_iota(jnp.int32, (ROWS_A, POOL_GROUP_WIDTH), 0) // BATCH
    pooled = []
    for gi, w in enumerate(POOL_WINDOWS):
        u_g = pool_buf[gi, hist_rows:hist_rows + ROWS_A, :]
        acc = u_g
        for j in range(1, w):
            start = hist_rows - j * BATCH
            acc = acc + pool_buf[gi, start:start + ROWS_A, :]
        cnt = jnp.minimum(t_pos + 1, w).astype(F32)
        pooled.append(acc / cnt - u_g)
    o_pool = _pool_mix(pooled, poolw_ref, pscale_ref)
    pool_buf[:, 0:hist_rows, :] = pool_buf[:, ROWS_A:ROWS_A + hist_rows, :]

    for k in range(N_SSM_BLOCKS):
        ossm_tm[k] = o_ssm[:, k * LANES:(k + 1) * LANES]
        opool_tm[k] = o_pool[:, k * LANES:(k + 1) * LANES]
    for b in range(BATCH):
        rows = pl.ds(b, STEPS_A, stride=BATCH)
        ossm_ref[b] = jnp.concatenate(
            [ossm_tm[k, rows, :] for k in range(N_SSM_BLOCKS)], axis=-1).astype(BF16)
        opool_ref[b] = jnp.concatenate(
            [opool_tm[k, rows, :] for k in range(N_SSM_BLOCKS)], axis=-1).astype(BF16)

    @pl.when(i == pl.num_programs(0) - 1)
    def _():
        for k in range(N_SSM_BLOCKS):
            hist_ref[:, k * LANES:(k + 1) * LANES] = pool_buf[k, 0:hist_rows, :]


def _mixer_a(layer, x, p, raw):
    hist_rows = POOL_HIST * BATCH
    n_steps = SEQ // STEPS_A
    seq_spec = lambda w: pl.BlockSpec((BATCH, STEPS_A, w), lambda i: (0, i, 0))
    cast_plan = {
        'w_gate': (D_MODEL // n_steps, N_BRANCH * D_MODEL, 1),
        'w_up': (N_BRANCH * BRANCH // n_steps, D_MODEL, 1),
        'w_out': (D_MODEL // n_steps, D_MODEL, 1),
        'w_ffn_in': (D_MODEL // n_steps, 2 * D_FF, 1),
        'w_ffn_out': (2 * D_FF // n_steps, D_MODEL, 2),
    }

    def chunk_in(name, width=None, col=0):
        rows, full_width, per = cast_plan[name]
        return pl.BlockSpec((None, rows, width or full_width),
                            lambda i: (layer, i // per, col))

    def chunk_out(name):
        rows, width, per = cast_plan[name]
        return pl.BlockSpec((rows, width), lambda i: (i // per, 0))

    cast_names = ('w_gate', 'w_up', 'w_out', 'w_ffn_in', 'w_ffn_out')
    cast_shapes = tuple(
        jax.ShapeDtypeStruct((cast_plan[n][0] * n_steps // cast_plan[n][2], cast_plan[n][1]), BF16)
        for n in cast_names)
    in_specs = [
        seq_spec(D_MODEL),
        _layer_spec((1, D_MODEL), layer),
        _layer_spec((D_MODEL, 3 * BRANCH), layer),
        _layer_spec((N_SSM_BLOCKS, LANES, 2 * BLOCK_STATE), layer),
        _layer_spec((N_SSM_BLOCKS, BLOCK_STATE, LANES), layer),
        _layer_spec((N_SSM_BLOCKS, BLOCK_STATE, LANES), layer),
        _layer_spec((1, N_STATE), layer),
        _layer_spec((1, N_STATE), layer),
        _layer_spec((1, BRANCH), layer),
        _layer_spec((BRANCH, BRANCH), layer),
        _layer_spec((1, BRANCH), layer),
        _layer_spec((len(POOL_WINDOWS), POOL_GROUP_WIDTH, POOL_GROUP_WIDTH), layer),
        _layer_spec((1, BRANCH), layer),
        chunk_in('w_gate', GATE_HALF, 1), chunk_in('w_gate', GATE_HALF, 2),
        chunk_in('w_up'), chunk_in('w_out'), chunk_in('w_ffn_in'), chunk_in('w_ffn_out'),
    ]
    branch_out = jax.ShapeDtypeStruct((BATCH, SEQ, BRANCH), BF16)
    out_shape = (
        branch_out, branch_out, branch_out,
        jax.ShapeDtypeStruct((BATCH, N_STATE), F32),
        jax.ShapeDtypeStruct((BATCH, N_STATE), F32),
        jax.ShapeDtypeStruct((hist_rows, BRANCH), F32),
    ) + cast_shapes
    out_specs = (
        seq_spec(BRANCH), seq_spec(BRANCH), seq_spec(BRANCH),
        pl.BlockSpec((BATCH, N_STATE), lambda i: (0, 0)),
        pl.BlockSpec((BATCH, N_STATE), lambda i: (0, 0)),
        pl.BlockSpec((hist_rows, BRANCH), lambda i: (0, 0)),
    ) + tuple(chunk_out(n) for n in cast_names)
    outs = pl.pallas_call(
        _mixer_a_kernel,
        grid=(n_steps,),
        in_specs=in_specs, out_specs=out_specs, out_shape=out_shape,
        scratch_shapes=[
            pltpu.VMEM((ROWS_A, N_STATE), F32),
            pltpu.VMEM((ROWS_A, N_STATE), F32),
            pltpu.VMEM((N_SSM_BLOCKS, hist_rows + ROWS_A, LANES), F32),
            pltpu.VMEM((N_SSM_BLOCKS, ROWS_A, LANES), F32),
            pltpu.VMEM((N_SSM_BLOCKS, ROWS_A, LANES), F32),
            pltpu.VMEM((N_SSM_BLOCKS, ROWS_A, LANES), F32),
            pltpu.VMEM((D_MODEL, 3 * BRANCH), BF16),
        ],
        compiler_params=pltpu.CompilerParams(
            dimension_semantics=("arbitrary",), vmem_limit_bytes=VMEM_LIMIT),
        name="prompt_mixer_a",
    )(x, p['g_mix_pre'], raw['w_in'], p['bblk'], p['cre'], p['cimn'], p['ar'], p['ai'],
      p['ssm_d'], p['w_glu'], p['b_glu'], p['pool_w'], p['pool_scale'],
      raw['w_in'], raw['w_in'], raw['w_up'], raw['w_out'], raw['w_ffn_in'], raw['w_ffn_out'])
    return outs[:6], dict(zip(cast_names, outs[6:]))


def _mem_kv_kernel(mem_ref, g_ref, wkv_ref, k_ref, v_ref):
    wkv = wkv_ref[...].astype(BF16)
    for b in range(BATCH):
        kv = _dot(_rms(mem_ref[b], g_ref[...]).astype(BF16), wkv)
        for hd in range(MEM_HEADS):
            rows = pl.ds(hd, N_MEM, stride=MEM_HEADS)
            k_ref[b, rows, :] = kv[:, hd * MEM_HEAD_DIM:(hd + 1) * MEM_HEAD_DIM]
            v_ref[b, rows, :] = kv[:, BRANCH + hd * MEM_HEAD_DIM:BRANCH + (hd + 1) * MEM_HEAD_DIM]


def _mem_kv(mem, g_mem, w_kv):
    out = jax.ShapeDtypeStruct((DEPTH, BATCH, KV_ROWS, MEM_HEAD_DIM), F32)
    out_spec = pl.BlockSpec((None, BATCH, KV_ROWS, MEM_HEAD_DIM), lambda l: (l, 0, 0, 0))
    return pl.pallas_call(
        _mem_kv_kernel,
        grid=(DEPTH,),
        in_specs=[
            _const_spec((BATCH, N_MEM, D_MODEL)),
            pl.BlockSpec((None, 1, D_MODEL), lambda l: (l, 0, 0)),
            pl.BlockSpec((None, D_MODEL, 2 * BRANCH), lambda l: (l, 0, 0)),
        ],
        out_specs=(out_spec, out_spec), out_shape=(out, out),
        compiler_params=pltpu.CompilerParams(
            dimension_semantics=("arbitrary",), vmem_limit_bytes=VMEM_LIMIT),
        name="mem_kv",
    )(mem, g_mem, w_kv)


def _memory_attention(q_ref, k_ref, v_ref):
    scale = MEM_HEAD_DIM ** -0.5
    outs = []
    for hd in range(MEM_HEADS):
        sl = slice(hd * MEM_HEAD_DIM, (hd + 1) * MEM_HEAD_DIM)
        head_rows = pl.ds(hd, N_MEM, stride=MEM_HEADS)
        s = lax.dot_general(q_ref[:, sl], k_ref[head_rows, :].astype(BF16),
                            (((1,), (1,)), ((), ())), preferred_element_type=F32) * scale
        e = jnp.exp(s - jnp.max(s, axis=-1, keepdims=True))
        prob = e / jnp.sum(e, axis=-1, keepdims=True)
        outs.append(_dot(prob.astype(BF16), v_ref[head_rows, :].astype(BF16)).astype(BF16))
    return jnp.concatenate(outs, axis=-1)


def _merge_ffn_kernel(with_attention, x_ref, ossm_ref, opool_ref, *refs):
    if with_attention:
        q_ref, k_ref, v_ref = refs[:3]
        refs = refs[3:]
        o_mem = _memory_attention(q_ref, k_ref, v_ref)
    else:
        o_mem = refs[0][...]
        refs = refs[1:]
    (gpre_ref, wgate_ref, wup_ref, wout_ref, gpost_ref, gfpre_ref, wfin_ref, wfout_ref,
     gfpost_ref, y_ref) = refs
    x = x_ref[...]
    h = _rms(x, gpre_ref[...]).astype(BF16)
    merged = None
    for b, o_b in enumerate((ossm_ref[...], opool_ref[...], o_mem)):
        gate = jax.nn.sigmoid(_dot(h, wgate_ref[:, b * D_MODEL:(b + 1) * D_MODEL]))
        term = gate * _dot(o_b, wup_ref[b * BRANCH:(b + 1) * BRANCH, :])
        merged = term if merged is None else merged + term
    x = x + _rms(_dot(merged.astype(BF16), wout_ref[...]), gpost_ref[...])
    hf = _rms(x, gfpre_ref[...]).astype(BF16)
    f = None
    for lo, hi in FF_CHUNKS:
        hg = _dot(hf, wfin_ref[:, lo:hi])
        hu = _dot(hf, wfin_ref[:, D_FF + lo:D_FF + hi])
        part = _dot((jax.nn.silu(hg) * hu).astype(BF16), wfout_ref[lo:hi, :])
        f = part if f is None else f + part
    y_ref[...] = x + _rms(f, gfpost_ref[...])


def _merge_ffn(layer, x, o_ssm, o_pool, third, p, w):
    n_rows = x.shape[0]
    rows = min(ROWS_C, n_rows)
    row_spec = lambda w: pl.BlockSpec((rows, w), lambda i: (i, 0))
    with_attention = isinstance(third, tuple)
    if with_attention:
        tiles_per_seq = SEQ // rows
        kv_spec = pl.BlockSpec((None, None, KV_ROWS, MEM_HEAD_DIM),
                               lambda i: (layer, i // tiles_per_seq, 0, 0))
        third_specs = [row_spec(BRANCH), kv_spec, kv_spec]
    else:
        third = (third,)
        third_specs = [row_spec(BRANCH)]
    in_specs = [row_spec(D_MODEL), row_spec(BRANCH), row_spec(BRANCH)] + third_specs + [
        _layer_spec((1, D_MODEL), layer),
        _const_spec((D_MODEL, N_BRANCH * D_MODEL)),
        _const_spec((N_BRANCH * BRANCH, D_MODEL)),
        _const_spec((D_MODEL, D_MODEL)),
        _layer_spec((1, D_MODEL), layer),
        _layer_spec((1, D_MODEL), layer),
        _const_spec((D_MODEL, 2 * D_FF)),
        _const_spec((D_FF, D_MODEL)),
        _layer_spec((1, D_MODEL), layer),
    ]
    return pl.pallas_call(
        functools.partial(_merge_ffn_kernel, with_attention),
        grid=(n_rows // rows,),
        in_specs=in_specs, out_specs=row_spec(D_MODEL),
        out_shape=jax.ShapeDtypeStruct((n_rows, D_MODEL), F32),
        compiler_params=pltpu.CompilerParams(
            dimension_semantics=("parallel",), vmem_limit_bytes=VMEM_LIMIT),
        name="merge_ffn",
    )(x, o_ssm, o_pool, *third, p['g_mix_pre'], w['w_gate'], w['w_up'], w['w_out'],
      p['g_mix_post'], p['g_ffn_pre'], w['w_ffn_in'], w['w_ffn_out'], p['g_ffn_post'])


def _sample_mixer_kernel(x_ref, g_ref, win_ref, bblk_ref, cre_ref, cimn_ref, ar_ref, ai_ref,
                         d_ref, wglu_ref, bglu_ref, poolw_ref, pscale_ref,
                         h0re_ref, h0im_ref, hist_ref,
                         q_ref, ossm_ref, opool_ref, hre_ref, him_ref, nhist_ref,
                         bu_re_ref, bu_im_ref):
    h = _rms(x_ref[...], g_ref[...]).astype(BF16)
    proj = _dot(h, win_ref[...].astype(BF16))
    u_ssm = proj[:, :BRANCH]
    u_pool = proj[:, BRANCH:2 * BRANCH]
    q_ref[...] = proj[:, 2 * BRANCH:]

    ys = []
    for k in range(N_SSM_BLOCKS):
        sl = slice(k * BLOCK_STATE, (k + 1) * BLOCK_STATE)
        _ssm_input(u_ssm[:, k * LANES:(k + 1) * LANES], bblk_ref[k], bu_re_ref, bu_im_ref, sl)
        a_r = ar_ref[:, sl]
        a_i = ai_ref[:, sl]
        h0r = h0re_ref[:, sl]
        h0i = h0im_ref[:, sl]
        hre_ref[:, sl] = bu_re_ref[:, sl] + a_r * h0r - a_i * h0i
        him_ref[:, sl] = bu_im_ref[:, sl] + a_r * h0i + a_i * h0r
        ys.append(_ssm_readout(hre_ref, him_ref, sl, cre_ref[k], cimn_ref[k]))
    y = _ssm_gate(jnp.concatenate(ys, axis=-1), u_ssm, d_ref, wglu_ref, bglu_ref)
    ossm_ref[...] = y.astype(BF16)

    pooled = []
    for gi, w in enumerate(POOL_WINDOWS):
        sl = slice(gi * POOL_GROUP_WIDTH, (gi + 1) * POOL_GROUP_WIDTH)
        acc = u_pool[:, sl]
        for j in range(1, w):
            acc = acc + hist_ref[POOL_HIST - j, :, sl]
        cnt = float(min(PAST_LEN + 1, w))
        pooled.append(acc / cnt - u_pool[:, sl])
    opool_ref[...] = _pool_mix(pooled, poolw_ref, pscale_ref).astype(BF16)
    for j in range(POOL_HIST - 1):
        nhist_ref[j] = hist_ref[j + 1]
    nhist_ref[POOL_HIST - 1] = u_pool


def _sample_mixer(layer, x_s, p, w_in, h0_re, h0_im, hist_t):
    n = x_s.shape[0]
    full = lambda shape: pl.BlockSpec(shape, lambda i: (0,) * len(shape))
    in_specs = [
        full((n, D_MODEL)),
        _layer_spec((1, D_MODEL), layer),
        _layer_spec((D_MODEL, 3 * BRANCH), layer),
        _layer_spec((N_SSM_BLOCKS, LANES, 2 * BLOCK_STATE), layer),
        _layer_spec((N_SSM_BLOCKS, BLOCK_STATE, LANES), layer),
        _layer_spec((N_SSM_BLOCKS, BLOCK_STATE, LANES), layer),
        _layer_spec((1, N_STATE), layer),
        _layer_spec((1, N_STATE), layer),
        _layer_spec((1, BRANCH), layer),
        _layer_spec((BRANCH, BRANCH), layer),
        _layer_spec((1, BRANCH), layer),
        _layer_spec((len(POOL_WINDOWS), POOL_GROUP_WIDTH, POOL_GROUP_WIDTH), layer),
        _layer_spec((1, BRANCH), layer),
        _layer_spec((n, N_STATE), layer),
        _layer_spec((n, N_STATE), layer),
        _layer_spec((POOL_HIST, n, BRANCH), layer),
    ]
    out_shape = (
        jax.ShapeDtypeStruct((n, BRANCH), F32),
        jax.ShapeDtypeStruct((n, BRANCH), BF16),
        jax.ShapeDtypeStruct((n, BRANCH), BF16),
        jax.ShapeDtypeStruct((n, N_STATE), F32),
        jax.ShapeDtypeStruct((n, N_STATE), F32),
        jax.ShapeDtypeStruct((POOL_HIST, n, BRANCH), F32),
    )
    out_specs = tuple(full(s.shape) for s in out_shape)
    return pl.pallas_call(
        _sample_mixer_kernel,
        grid=(1,),
        in_specs=in_specs, out_specs=out_specs, out_shape=out_shape,
        scratch_shapes=[pltpu.VMEM((n, N_STATE), F32), pltpu.VMEM((n, N_STATE), F32)],
        compiler_params=pltpu.CompilerParams(
            dimension_semantics=("arbitrary",), vmem_limit_bytes=VMEM_LIMIT),
        name="sample_mixer",
    )(x_s, p['g_mix_pre'], w_in, p['bblk'], p['cre'], p['cimn'], p['ar'], p['ai'],
      p['ssm_d'], p['w_glu'], p['b_glu'], p['pool_w'], p['pool_scale'], h0_re, h0_im, hist_t)


def _sample_attention_kernel(q_ref, k_ref, v_ref, ones_ref, o_ref):
    tiles = KV_ROWS // SUBLANES
    scale = MEM_HEAD_DIM ** -0.5

    def one_sample(b, carry):
        q8 = q_ref[b] * scale
        prod = k_ref[b].reshape(tiles, SUBLANES, MEM_HEAD_DIM) * q8[None]
        prod = prod.reshape(KV_ROWS, MEM_HEAD_DIM).astype(BF16)
        s = _dot(prod, ones_ref[...]).reshape(tiles, SUBLANES, MEM_HEAD_DIM)
        m8 = jnp.max(s, axis=0)
        m8 = jnp.maximum(m8, pltpu.roll(m8, MEM_HEADS, 0))
        e = jnp.exp(s - m8[None])
        l8 = jnp.sum(e, axis=0)
        acc = jnp.sum(e * v_ref[b].reshape(tiles, SUBLANES, MEM_HEAD_DIM), axis=0)
        l8 = l8 + pltpu.roll(l8, MEM_HEADS, 0)
        acc = acc + pltpu.roll(acc, MEM_HEADS, 0)
        o_ref[b] = acc / l8
        return carry

    lax.fori_loop(0, SAMPLE_BLOCK, one_sample, 0, unroll=True)


def _sample_attention(layer, q8, k_cache, v_cache, lane_ones):
    n = q8.shape[0]
    kv_spec = pl.BlockSpec((None, SAMPLE_BLOCK, KV_ROWS, MEM_HEAD_DIM), lambda i: (layer, i, 0, 0))
    q_spec = pl.BlockSpec((SAMPLE_BLOCK, SUBLANES, MEM_HEAD_DIM), lambda i: (i, 0, 0))
    return pl.pallas_call(
        _sample_attention_kernel,
        grid=(n // SAMPLE_BLOCK,),
        in_specs=[q_spec, kv_spec, kv_spec, _const_spec((MEM_HEAD_DIM, MEM_HEAD_DIM))],
        out_specs=q_spec,
        out_shape=jax.ShapeDtypeStruct((n, SUBLANES, MEM_HEAD_DIM), F32),
        compiler_params=pltpu.CompilerParams(
            dimension_semantics=("parallel",), vmem_limit_bytes=VMEM_LIMIT),
        name="sample_attention",
    )(q8, k_cache, v_cache, lane_ones)


def kernel(x_prompt, x_sample, mem_prompt, cache_mem_k, cache_mem_v, state_ssm_re, state_ssm_im, state_pool, g_mix_pre, g_mix_post, g_ffn_pre, g_ffn_post, g_mem, w_in, w_kv, ssm_lam_re, ssm_lam_im, ssm_log_dt, ssm_b_re, ssm_b_im, ssm_c_re, ssm_c_im, ssm_d, ssm_w_glu, ssm_b_glu, pool_w, pool_scale, w_branch_up, w_out, w_ffn_in, w_ffn_out):
    ar, ai, bblk = _discretise(ssm_lam_re, ssm_lam_im, ssm_log_dt, ssm_b_re, ssm_b_im)
    vec = lambda a: a.reshape(DEPTH, 1, a.shape[-1])
    p = {
        'g_mix_pre': vec(g_mix_pre), 'g_mix_post': vec(g_mix_post),
        'g_ffn_pre': vec(g_ffn_pre), 'g_ffn_post': vec(g_ffn_post),
        'bblk': bblk, 'ar': ar, 'ai': ai,
        'cre': _c_blocks(ssm_c_re), 'cimn': _c_blocks(-ssm_c_im),
        'ssm_d': vec(ssm_d), 'w_glu': ssm_w_glu.astype(BF16), 'b_glu': vec(ssm_b_glu),
        'pool_w': pool_w.astype(BF16), 'pool_scale': vec(pool_scale),
    }
    raw = {
        'w_in': w_in, 'w_up': w_branch_up.reshape(DEPTH, N_BRANCH * BRANCH, D_MODEL),
        'w_out': w_out, 'w_ffn_in': w_ffn_in, 'w_ffn_out': w_ffn_out,
    }
    k_mem, v_mem = _mem_kv(mem_prompt, vec(g_mem), w_kv)

    lane_ones = jnp.ones((MEM_HEAD_DIM, MEM_HEAD_DIM), BF16)
    k_cache = cache_mem_k.reshape(DEPTH, DEC_BATCH, KV_ROWS, MEM_HEAD_DIM)
    v_cache = cache_mem_v.reshape(DEPTH, DEC_BATCH, KV_ROWS, MEM_HEAD_DIM)
    h0_re = state_ssm_re.reshape(DEPTH, DEC_BATCH, N_STATE)
    h0_im = state_ssm_im.reshape(DEPTH, DEC_BATCH, N_STATE)
    hist_t = jnp.swapaxes(state_pool, 1, 2)

    xp = x_prompt
    xs = x_sample.reshape(DEC_BATCH, D_MODEL)
    n_prompt = BATCH * SEQ
    re_p, im_p, pool_p, re_s, im_s, pool_s = [], [], [], [], [], []
    for layer in range(DEPTH):
        (q, o_ssm, o_pool, hre, him, hist), w_bf = _mixer_a(layer, xp, p, raw)
        xp = _merge_ffn(layer, xp.reshape(n_prompt, D_MODEL), o_ssm.reshape(n_prompt, BRANCH),
                        o_pool.reshape(n_prompt, BRANCH),
                        (q.reshape(n_prompt, BRANCH), k_mem, v_mem), p, w_bf)
        xp = xp.reshape(BATCH, SEQ, D_MODEL)
        re_p.append(hre.reshape(BATCH, SSM_GROUPS, SSM_STATE))
        im_p.append(him.reshape(BATCH, SSM_GROUPS, SSM_STATE))
        pool_p.append(jnp.swapaxes(hist.reshape(POOL_HIST, BATCH, BRANCH), 0, 1))

        qs, os_ssm, os_pool, hre, him, nhist = _sample_mixer(layer, xs, p, w_in, h0_re, h0_im,
                                                             hist_t)
        qs4 = qs.reshape(DEC_BATCH, MEM_HEADS, MEM_HEAD_DIM)
        os_mem = _sample_attention(layer, jnp.concatenate([qs4, qs4], axis=1),
                                   k_cache, v_cache, lane_ones)
        os_mem = os_mem[:, :MEM_HEADS].reshape(DEC_BATCH, BRANCH).astype(BF16)
        xs = _merge_ffn(layer, xs, os_ssm, os_pool, os_mem, p, w_bf)
        re_s.append(hre.reshape(DEC_BATCH, SSM_GROUPS, SSM_STATE))
        im_s.append(him.reshape(DEC_BATCH, SSM_GROUPS, SSM_STATE))
        pool_s.append(jnp.swapaxes(nhist, 0, 1))

    y_prompt = xp
    y_sample = xs.reshape(DEC_BATCH, 1, D_MODEL)
    kv_shape = (DEPTH, BATCH, N_MEM, MEM_HEADS, MEM_HEAD_DIM)
    return (y_prompt, y_sample,
            jnp.stack(re_p), jnp.stack(im_p), jnp.stack(pool_p),
            k_mem.reshape(kv_shape), v_mem.reshape(kv_shape),
            jnp.stack(re_s), jnp.stack(im_s), jnp.stack(pool_s))
```

```python
import functools

import jax
import jax.numpy as jnp
from jax import lax
from jax.experimental import pallas as pl
from jax.experimental.pallas import tpu as pltpu

D_MODEL = 1024
BATCH = 8
SEQ = 2048
DEPTH = 4
DEC_BATCH = 128
PAST_LEN = 16384

BRANCH = D_MODEL // 2
SSM_GROUP = 16
SSM_GROUPS = BRANCH // SSM_GROUP
SSM_STATE = 64
N_STATE = SSM_GROUPS * SSM_STATE
POOL_WINDOWS = (2, 4, 8, 16)
POOL_GROUP_WIDTH = BRANCH // len(POOL_WINDOWS)
POOL_HIST = max(POOL_WINDOWS) - 1
N_MEM = 256
MEM_HEADS = 4
MEM_HEAD_DIM = BRANCH // MEM_HEADS
KV_ROWS = N_MEM * MEM_HEADS
N_BRANCH = 3
GATE_HALF = N_BRANCH * D_MODEL // 2
D_FF = 2816
RMS_EPS = 1e-6

LANES = 128
SUBLANES = 8
GROUPS_PER_BLOCK = LANES // SSM_GROUP
N_SSM_BLOCKS = BRANCH // LANES
BLOCK_STATE = GROUPS_PER_BLOCK * SSM_STATE

ROWS_A = 512
STEPS_A = ROWS_A // BATCH
ROWS_C = 512
MERGE_COLS = 256
SAMPLE_BLOCK = 16
FF_CHUNKS = ((0, 1536), (1536, 2816))
VMEM_LIMIT = 56 * 1024 * 1024

BF16 = jnp.bfloat16
F32 = jnp.float32


def _rms(x, g):
    ms = jnp.mean(x * x, axis=-1, keepdims=True)
    return x * lax.rsqrt(ms + RMS_EPS) * g


def _dot(a, b):
    return jnp.dot(a, b, preferred_element_type=F32)


def _const_spec(shape):
    nd = len(shape)
    return pl.BlockSpec(shape, lambda *_: (0,) * nd, pipeline_mode=pl.Buffered(1))


def _layer_spec(shape, layer):
    nd = len(shape)
    return pl.BlockSpec((None,) + tuple(shape), lambda *_: (layer,) + (0,) * nd,
                        pipeline_mode=pl.Buffered(1))


def _discretise_kernel(lr_ref, li_ref, ldt_ref, br_ref, bi_ref,
                       ar_ref, ai_ref, bbr_ref, bbi_ref):
    lr = lr_ref[...]
    li = li_ref[...]
    dt = jnp.exp(ldt_ref[...])
    zr = lr * dt
    zi = li * dt
    mag = jnp.exp(zr)
    ar = mag * jnp.cos(zi)
    ai = mag * jnp.sin(zi)
    den = lr * lr + li * li
    fr = ((ar - 1.0) * lr + ai * li) / den
    fi = (ai * lr - (ar - 1.0) * li) / den
    br = br_ref[...]
    bi = bi_ref[...]
    ar_ref[...] = ar
    ai_ref[...] = ai
    bbr_ref[...] = fr * br - fi * bi
    bbi_ref[...] = fr * bi + fi * br


def _discretise(lam_re, lam_im, log_dt, b_re, b_im):
    rows = DEPTH * SSM_GROUPS
    width = SSM_GROUP * SSM_STATE

    def tile_p(a):
        return jnp.tile(a.reshape(rows, 1, SSM_STATE), (1, SSM_GROUP, 1)).reshape(rows, width)

    def b_t(a):
        return jnp.swapaxes(a, -1, -2).reshape(rows, width)

    ldt = jnp.broadcast_to(log_dt.reshape(rows, 1), (rows, width))
    out = jax.ShapeDtypeStruct((rows, width), F32)
    ar, ai, bbr, bbi = pl.pallas_call(
        _discretise_kernel, out_shape=(out, out, out, out), name="s5_discretise",
    )(tile_p(lam_re), tile_p(lam_im), ldt, b_t(b_re), b_t(b_im))
    ar = ar[:, :SSM_STATE].reshape(DEPTH, 1, N_STATE)
    ai = ai[:, :SSM_STATE].reshape(DEPTH, 1, N_STATE)
    shape5 = (DEPTH, N_SSM_BLOCKS, GROUPS_PER_BLOCK, SSM_GROUP, SSM_STATE)
    bblk = jnp.concatenate([_group_block_diag(bbr.reshape(shape5)),
                            _group_block_diag(bbi.reshape(shape5))], axis=-1)
    return ar, ai, bblk


def _group_block_diag(m):
    depth, blocks, groups, n_r, n_c = m.shape
    same = jnp.eye(groups, dtype=bool)[None, None, :, None, :, None]
    out = jnp.where(same, m[:, :, :, :, None, :], 0.0).astype(BF16)
    return out.reshape(depth, blocks, groups * n_r, groups * n_c)


def _c_blocks(c):
    shape5 = (DEPTH, N_SSM_BLOCKS, GROUPS_PER_BLOCK, SSM_GROUP, SSM_STATE)
    return _group_block_diag(jnp.swapaxes(c.reshape(shape5), -1, -2))


def _ssm_input(u_block, bblk, bu_re_ref, bu_im_ref, sl):
    bu = _dot(u_block.astype(BF16), bblk)
    bu_re_ref[:, sl] = bu[:, :BLOCK_STATE]
    bu_im_ref[:, sl] = bu[:, BLOCK_STATE:]


def _ssm_readout(h_re_ref, h_im_ref, sl, cre, cimn):
    return _dot(h_re_ref[:, sl].astype(BF16), cre) + _dot(h_im_ref[:, sl].astype(BF16), cimn)


def _ssm_gate(y, u_ssm, d_ref, wglu_ref, bglu_ref):
    y = jax.nn.gelu(y + d_ref[...] * u_ssm)
    return y * jax.nn.sigmoid(_dot(y.astype(BF16), wglu_ref[...]) + bglu_ref[...])


def _pool_mix(pooled, poolw_ref, pscale_ref):
    outs = []
    for gi in range(len(POOL_WINDOWS)):
        sl = slice(gi * POOL_GROUP_WIDTH, (gi + 1) * POOL_GROUP_WIDTH)
        outs.append(_dot(pooled[gi].astype(BF16), poolw_ref[gi]) * pscale_ref[:, sl])
    return jnp.concatenate(outs, axis=-1)


def _mixer_a_kernel(x_ref, g_ref, win_ref, bblk_ref, cre_ref, cimn_ref, ar_ref, ai_ref,
                    d_ref, wglu_ref, bglu_ref, poolw_ref, pscale_ref,
                    wgate_lo_ref, wgate_hi_ref, wup_ref, wout_ref, wfin_ref, wfout_ref,
                    q_ref, ossm_ref, opool_ref, hre_ref, him_ref, hist_ref,
                    wgate_bf_ref, wup_bf_ref, wout_bf_ref, wfin_bf_ref, wfout_bf_ref,
                    bu_re_ref, bu_im_ref, pool_buf, ussm_tm, ossm_tm, opool_tm, win_bf):
    i = pl.program_id(0)
    hist_rows = POOL_HIST * BATCH

    wgate_bf_ref[:, :GATE_HALF] = wgate_lo_ref[...].astype(BF16)
    wgate_bf_ref[:, GATE_HALF:] = wgate_hi_ref[...].astype(BF16)
    wup_bf_ref[...] = wup_ref[...].astype(BF16)
    wout_bf_ref[...] = wout_ref[...].astype(BF16)
    wfin_bf_ref[...] = wfin_ref[...].astype(BF16)
    wfout_bf_ref[...] = wfout_ref[...].astype(BF16)

    @pl.when(i == 0)
    def _():
        hre_ref[...] = jnp.zeros_like(hre_ref)
        him_ref[...] = jnp.zeros_like(him_ref)
        pool_buf[:, 0:hist_rows, :] = jnp.zeros((N_SSM_BLOCKS, hist_rows, LANES), F32)
        win_bf[...] = win_ref[...].astype(BF16)

    x = x_ref[...].reshape(ROWS_A, D_MODEL)
    h = _rms(x, g_ref[...]).astype(BF16)
    proj = _dot(h, win_bf[...])
    for b in range(BATCH):
        rows = slice(b * STEPS_A, (b + 1) * STEPS_A)
        q_ref[b] = proj[rows, 2 * BRANCH:].astype(BF16)
        for k in range(N_SSM_BLOCKS):
            ussm_tm[k, pl.ds(b, STEPS_A, stride=BATCH), :] = proj[rows, k * LANES:(k + 1) * LANES]
            pool_buf[k, pl.ds(hist_rows + b, STEPS_A, stride=BATCH), :] = (
                proj[rows, BRANCH + k * LANES:BRANCH + (k + 1) * LANES])
    u_ssm = jnp.concatenate([ussm_tm[k] for k in range(N_SSM_BLOCKS)], axis=-1)

    ys = []
    for k in range(N_SSM_BLOCKS):
        sl = slice(k * BLOCK_STATE, (k + 1) * BLOCK_STATE)
        _ssm_input(ussm_tm[k], bblk_ref[k], bu_re_ref, bu_im_ref, sl)
        a_r = jnp.broadcast_to(ar_ref[:, sl], (BATCH, BLOCK_STATE))
        a_i = jnp.broadcast_to(ai_ref[:, sl], (BATCH, BLOCK_STATE))
        hr = hre_ref[:, sl]
        hi = him_ref[:, sl]
        for t in range(STEPS_A):
            rows = slice(t * BATCH, (t + 1) * BATCH)
            hr, hi = (a_r * hr - a_i * hi + bu_re_ref[rows, sl],
                      a_r * hi + a_i * hr + bu_im_ref[rows, sl])
            bu_re_ref[rows, sl] = hr
            bu_im_ref[rows, sl] = hi
        hre_ref[:, sl] = hr
        him_ref[:, sl] = hi
        ys.append(_ssm_readout(bu_re_ref, bu_im_ref, sl, cre_ref[k], cimn_ref[k]))
    o_ssm = _ssm_gate(jnp.concatenate(ys, axis=-1), u_ssm, d_ref, wglu_ref, bglu_ref)

    t_pos = i * STEPS_A + lax.broadcasted_iota(jnp.int32, (ROWS_A, POOL_GROUP_WIDTH), 0) // BATCH
    pooled = []
    for gi, w in enumerate(POOL_WINDOWS):
        u_g = pool_buf[gi, hist_rows:hist_rows + ROWS_A, :]
        acc = u_g
        for j in range(1, w):
            start = hist_rows - j * BATCH
            acc = acc + pool_buf[gi, start:start + ROWS_A, :]
        cnt = jnp.minimum(t_pos + 1, w).astype(F32)
        pooled.append(acc / cnt - u_g)
    o_pool = _pool_mix(pooled, poolw_ref, pscale_ref)
    pool_buf[:, 0:hist_rows, :] = pool_buf[:, ROWS_A:ROWS_A + hist_rows, :]

    for k in range(N_SSM_BLOCKS):
        ossm_tm[k] = o_ssm[:, k * LANES:(k + 1) * LANES]
        opool_tm[k] = o_pool[:, k * LANES:(k + 1) * LANES]
    for b in range(BATCH):
        rows = pl.ds(b, STEPS_A, stride=BATCH)
        ossm_ref[b] = jnp.concatenate(
            [ossm_tm[k, rows, :] for k in range(N_SSM_BLOCKS)], axis=-1).astype(BF16)
        opool_ref[b] = jnp.concatenate(
            [opool_tm[k, rows, :] for k in range(N_SSM_BLOCKS)], axis=-1).astype(BF16)

    @pl.when(i == pl.num_programs(0) - 1)
    def _():
        for k in range(N_SSM_BLOCKS):
            hist_ref[:, k * LANES:(k + 1) * LANES] = pool_buf[k, 0:hist_rows, :]


def _mixer_a(layer, x, p, raw):
    hist_rows = POOL_HIST * BATCH
    n_steps = SEQ // STEPS_A
    seq_spec = lambda w: pl.BlockSpec((BATCH, STEPS_A, w), lambda i: (0, i, 0))
    cast_plan = {
        'w_gate': (D_MODEL // n_steps, N_BRANCH * D_MODEL, 1),
        'w_up': (N_BRANCH * BRANCH // n_steps, D_MODEL, 1),
        'w_out': (D_MODEL // n_steps, D_MODEL, 1),
        'w_ffn_in': (D_MODEL // n_steps, 2 * D_FF, 1),
        'w_ffn_out': (2 * D_FF // n_steps, D_MODEL, 2),
    }

    def chunk_in(name, width=None, col=0):
        rows, full_width, per = cast_plan[name]
        return pl.BlockSpec((None, rows, width or full_width),
                            lambda i: (layer, i // per, col))

    def chunk_out(name):
        rows, width, per = cast_plan[name]
        return pl.BlockSpec((rows, width), lambda i: (i // per, 0))

    cast_names = ('w_gate', 'w_up', 'w_out', 'w_ffn_in', 'w_ffn_out')
    cast_shapes = tuple(
        jax.ShapeDtypeStruct((cast_plan[n][0] * n_steps // cast_plan[n][2], cast_plan[n][1]), BF16)
        for n in cast_names)
    in_specs = [
        seq_spec(D_MODEL),
        _layer_spec((1, D_MODEL), layer),
        _layer_spec((D_MODEL, 3 * BRANCH), layer),
        _layer_spec((N_SSM_BLOCKS, LANES, 2 * BLOCK_STATE), layer),
        _layer_spec((N_SSM_BLOCKS, BLOCK_STATE, LANES), layer),
        _layer_spec((N_SSM_BLOCKS, BLOCK_STATE, LANES), layer),
        _layer_spec((1, N_STATE), layer),
        _layer_spec((1, N_STATE), layer),
        _layer_spec((1, BRANCH), layer),
        _layer_spec((BRANCH, BRANCH), layer),
        _layer_spec((1, BRANCH), layer),
        _layer_spec((len(POOL_WINDOWS), POOL_GROUP_WIDTH, POOL_GROUP_WIDTH), layer),
        _layer_spec((1, BRANCH), layer),
        chunk_in('w_gate', GATE_HALF, 1), chunk_in('w_gate', GATE_HALF, 2),
        chunk_in('w_up'), chunk_in('w_out'), chunk_in('w_ffn_in'), chunk_in('w_ffn_out'),
    ]
    branch_out = jax.ShapeDtypeStruct((BATCH, SEQ, BRANCH), BF16)
    out_shape = (
        branch_out, branch_out, branch_out,
        jax.ShapeDtypeStruct((BATCH, N_STATE), F32),
        jax.ShapeDtypeStruct((BATCH, N_STATE), F32),
        jax.ShapeDtypeStruct((hist_rows, BRANCH), F32),
    ) + cast_shapes
    out_specs = (
        seq_spec(BRANCH), seq_spec(BRANCH), seq_spec(BRANCH),
        pl.BlockSpec((BATCH, N_STATE), lambda i: (0, 0)),
        pl.BlockSpec((BATCH, N_STATE), lambda i: (0, 0)),
        pl.BlockSpec((hist_rows, BRANCH), lambda i: (0, 0)),
    ) + tuple(chunk_out(n) for n in cast_names)
    outs = pl.pallas_call(
        _mixer_a_kernel,
        grid=(n_steps,),
        in_specs=in_specs, out_specs=out_specs, out_shape=out_shape,
        scratch_shapes=[
            pltpu.VMEM((ROWS_A, N_STATE), F32),
            pltpu.VMEM((ROWS_A, N_STATE), F32),
            pltpu.VMEM((N_SSM_BLOCKS, hist_rows + ROWS_A, LANES), F32),
            pltpu.VMEM((N_SSM_BLOCKS, ROWS_A, LANES), F32),
            pltpu.VMEM((N_SSM_BLOCKS, ROWS_A, LANES), F32),
            pltpu.VMEM((N_SSM_BLOCKS, ROWS_A, LANES), F32),
            pltpu.VMEM((D_MODEL, 3 * BRANCH), BF16),
        ],
        compiler_params=pltpu.CompilerParams(
            dimension_semantics=("arbitrary",), vmem_limit_bytes=VMEM_LIMIT),
        name="prompt_mixer_a",
    )(x, p['g_mix_pre'], raw['w_in'], p['bblk'], p['cre'], p['cimn'], p['ar'], p['ai'],
      p['ssm_d'], p['w_glu'], p['b_glu'], p['pool_w'], p['pool_scale'],
      raw['w_in'], raw['w_in'], raw['w_up'], raw['w_out'], raw['w_ffn_in'], raw['w_ffn_out'])
    return outs[:6], dict(zip(cast_names, outs[6:]))


def _mem_kv_kernel(mem_ref, g_ref, wkv_ref, k_ref, v_ref):
    wkv = wkv_ref[...].astype(BF16)
    for b in range(BATCH):
        kv = _dot(_rms(mem_ref[b], g_ref[...]).astype(BF16), wkv)
        for hd in range(MEM_HEADS):
            rows = pl.ds(hd, N_MEM, stride=MEM_HEADS)
            k_ref[b, rows, :] = kv[:, hd * MEM_HEAD_DIM:(hd + 1) * MEM_HEAD_DIM]
            v_ref[b, rows, :] = kv[:, BRANCH + hd * MEM_HEAD_DIM:BRANCH + (hd + 1) * MEM_HEAD_DIM]


def _mem_kv(mem, g_mem, w_kv):
    out = jax.ShapeDtypeStruct((DEPTH, BATCH, KV_ROWS, MEM_HEAD_DIM), F32)
    out_spec = pl.BlockSpec((None, BATCH, KV_ROWS, MEM_HEAD_DIM), lambda l: (l, 0, 0, 0))
    return pl.pallas_call(
        _mem_kv_kernel,
        grid=(DEPTH,),
        in_specs=[
            _const_spec((BATCH, N_MEM, D_MODEL)),
            pl.BlockSpec((None, 1, D_MODEL), lambda l: (l, 0, 0)),
            pl.BlockSpec((None, D_MODEL, 2 * BRANCH), lambda l: (l, 0, 0)),
        ],
        out_specs=(out_spec, out_spec), out_shape=(out, out),
        compiler_params=pltpu.CompilerParams(
            dimension_semantics=("arbitrary",), vmem_limit_bytes=VMEM_LIMIT),
        name="mem_kv",
    )(mem, g_mem, w_kv)


def _memory_attention_stages(q_ref, k_ref, v_ref, result):
    scale = MEM_HEAD_DIM ** -0.5
    head_rows = [pl.ds(hd, N_MEM, stride=MEM_HEADS) for hd in range(MEM_HEADS)]
    head_cols = [slice(hd * MEM_HEAD_DIM, (hd + 1) * MEM_HEAD_DIM) for hd in range(MEM_HEADS)]
    scores = [lax.dot_general(q_ref[:, head_cols[hd]], k_ref[head_rows[hd], :].astype(BF16),
                              (((1,), (1,)), ((), ())), preferred_element_type=F32) * scale
              for hd in range(MEM_HEADS)]
    yield
    probs = []
    for s in scores:
        e = jnp.exp(s - jnp.max(s, axis=-1, keepdims=True))
        probs.append((e / jnp.sum(e, axis=-1, keepdims=True)).astype(BF16))
        yield
    outs = []
    for hd in range(MEM_HEADS):
        outs.append(_dot(probs[hd], v_ref[head_rows[hd], :].astype(BF16)).astype(BF16))
        yield
    result.append(jnp.concatenate(outs, axis=-1))


def _merge_ffn_kernel(with_attention, x_ref, ossm_ref, opool_ref, *refs):
    o_mem = []
    if with_attention:
        attention = _memory_attention_stages(*refs[:3], o_mem)
        refs = refs[3:]
    else:
        attention = iter(())
        o_mem.append(refs[0][...])
        refs = refs[1:]
    (gpre_ref, wgate_ref, wup_ref, wout_ref, gpost_ref, gfpre_ref, wfin_ref, wfout_ref,
     gfpost_ref, y_ref) = refs
    x = x_ref[...]
    h = _rms(x, gpre_ref[...]).astype(BF16)

    def branch_term(b, o_b, cols):
        gate_cols = slice(b * D_MODEL + cols.start, b * D_MODEL + cols.stop)
        gate = jax.nn.sigmoid(_dot(h, wgate_ref[:, gate_cols]))
        return gate * _dot(o_b, wup_ref[b * BRANCH:(b + 1) * BRANCH, cols])

    col_chunks = [slice(c, c + MERGE_COLS) for c in range(0, D_MODEL, MERGE_COLS)]
    next(attention, None)
    partial = []
    for b, o_b in enumerate((ossm_ref[...], opool_ref[...])):
        for ci, cols in enumerate(col_chunks):
            term = branch_term(b, o_b, cols)
            if b == 0:
                partial.append(term)
            else:
                partial[ci] = partial[ci] + term
            next(attention, None)
    for _ in attention:
        pass
    merged = jnp.concatenate(
        [partial[ci] + branch_term(N_BRANCH - 1, o_mem[0], cols)
         for ci, cols in enumerate(col_chunks)], axis=-1)
    x = x + _rms(_dot(merged.astype(BF16), wout_ref[...]), gpost_ref[...])
    hf = _rms(x, gfpre_ref[...]).astype(BF16)
    f = None
    for lo, hi in FF_CHUNKS:
        hg = _dot(hf, wfin_ref[:, lo:hi])
        hu = _dot(hf, wfin_ref[:, D_FF + lo:D_FF + hi])
        part = _dot((jax.nn.silu(hg) * hu).astype(BF16), wfout_ref[lo:hi, :])
        f = part if f is None else f + part
    y_ref[...] = x + _rms(f, gfpost_ref[...])


def _merge_ffn(layer, x, o_ssm, o_pool, third, p, w):
    n_rows = x.shape[0]
    rows = min(ROWS_C, n_rows)
    row_spec = lambda w: pl.BlockSpec((rows, w), lambda i: (i, 0))
    with_attention = isinstance(third, tuple)
    if with_attention:
        tiles_per_seq = SEQ // rows
        kv_spec = pl.BlockSpec((None, None, KV_ROWS, MEM_HEAD_DIM),
                               lambda i: (layer, i // tiles_per_seq, 0, 0))
        third_specs = [row_spec(BRANCH), kv_spec, kv_spec]
    else:
        third = (third,)
        third_specs = [row_spec(BRANCH)]
    in_specs = [row_spec(D_MODEL), row_spec(BRANCH), row_spec(BRANCH)] + third_specs + [
        _layer_spec((1, D_MODEL), layer),
        _const_spec((D_MODEL, N_BRANCH * D_MODEL)),
        _const_spec((N_BRANCH * BRANCH, D_MODEL)),
        _const_spec((D_MODEL, D_MODEL)),
        _layer_spec((1, D_MODEL), layer),
        _layer_spec((1, D_MODEL), layer),
        _const_spec((D_MODEL, 2 * D_FF)),
        _const_spec((D_FF, D_MODEL)),
        _layer_spec((1, D_MODEL), layer),
    ]
    return pl.pallas_call(
        functools.partial(_merge_ffn_kernel, with_attention),
        grid=(n_rows // rows,),
        in_specs=in_specs, out_specs=row_spec(D_MODEL),
        out_shape=jax.ShapeDtypeStruct((n_rows, D_MODEL), F32),
        compiler_params=pltpu.CompilerParams(
            dimension_semantics=("parallel",), vmem_limit_bytes=VMEM_LIMIT),
        name="merge_ffn",
    )(x, o_ssm, o_pool, *third, p['g_mix_pre'], w['w_gate'], w['w_up'], w['w_out'],
      p['g_mix_post'], p['g_ffn_pre'], w['w_ffn_in'], w['w_ffn_out'], p['g_ffn_post'])


def _sample_mixer_kernel(x_ref, g_ref, win_ref, bblk_ref, cre_ref, cimn_ref, ar_ref, ai_ref,
                         d_ref, wglu_ref, bglu_ref, poolw_ref, pscale_ref,
                         h0re_ref, h0im_ref, hist_ref,
                         q_ref, ossm_ref, opool_ref, hre_ref, him_ref, nhist_ref,
                         bu_re_ref, bu_im_ref):
    h = _rms(x_ref[...], g_ref[...]).astype(BF16)
    proj = _dot(h, win_ref[...].astype(BF16))
    u_ssm = proj[:, :BRANCH]
    u_pool = proj[:, BRANCH:2 * BRANCH]
    q_ref[...] = proj[:, 2 * BRANCH:]

    ys = []
    for k in range(N_SSM_BLOCKS):
        sl = slice(k * BLOCK_STATE, (k + 1) * BLOCK_STATE)
        _ssm_input(u_ssm[:, k * LANES:(k + 1) * LANES], bblk_ref[k], bu_re_ref, bu_im_ref, sl)
        a_r = ar_ref[:, sl]
        a_i = ai_ref[:, sl]
        h0r = h0re_ref[:, sl]
        h0i = h0im_ref[:, sl]
        hre_ref[:, sl] = bu_re_ref[:, sl] + a_r * h0r - a_i * h0i
        him_ref[:, sl] = bu_im_ref[:, sl] + a_r * h0i + a_i * h0r
        ys.append(_ssm_readout(hre_ref, him_ref, sl, cre_ref[k], cimn_ref[k]))
    y = _ssm_gate(jnp.concatenate(ys, axis=-1), u_ssm, d_ref, wglu_ref, bglu_ref)
    ossm_ref[...] = y.astype(BF16)

    pooled = []
    for gi, w in enumerate(POOL_WINDOWS):
        sl = slice(gi * POOL_GROUP_WIDTH, (gi + 1) * POOL_GROUP_WIDTH)
        acc = u_pool[:, sl]
        for j in range(1, w):
            acc = acc + hist_ref[POOL_HIST - j, :, sl]
        cnt = float(min(PAST_LEN + 1, w))
        pooled.append(acc / cnt - u_pool[:, sl])
    opool_ref[...] = _pool_mix(pooled, poolw_ref, pscale_ref).astype(BF16)
    for j in range(POOL_HIST - 1):
        nhist_ref[j] = hist_ref[j + 1]
    nhist_ref[POOL_HIST - 1] = u_pool


def _sample_mixer(layer, x_s, p, w_in, h0_re, h0_im, hist_t):
    n = x_s.shape[0]
    full = lambda shape: pl.BlockSpec(shape, lambda i: (0,) * len(shape))
    in_specs = [
        full((n, D_MODEL)),
        _layer_spec((1, D_MODEL), layer),
        _layer_spec((D_MODEL, 3 * BRANCH), layer),
        _layer_spec((N_SSM_BLOCKS, LANES, 2 * BLOCK_STATE), layer),
        _layer_spec((N_SSM_BLOCKS, BLOCK_STATE, LANES), layer),
        _layer_spec((N_SSM_BLOCKS, BLOCK_STATE, LANES), layer),
        _layer_spec((1, N_STATE), layer),
        _layer_spec((1, N_STATE), layer),
        _layer_spec((1, BRANCH), layer),
        _layer_spec((BRANCH, BRANCH), layer),
        _layer_spec((1, BRANCH), layer),
        _layer_spec((len(POOL_WINDOWS), POOL_GROUP_WIDTH, POOL_GROUP_WIDTH), layer),
        _layer_spec((1, BRANCH), layer),
        _layer_spec((n, N_STATE), layer),
        _layer_spec((n, N_STATE), layer),
        _layer_spec((POOL_HIST, n, BRANCH), layer),
    ]
    out_shape = (
        jax.ShapeDtypeStruct((n, BRANCH), F32),
        jax.ShapeDtypeStruct((n, BRANCH), BF16),
        jax.ShapeDtypeStruct((n, BRANCH), BF16),
        jax.ShapeDtypeStruct((n, N_STATE), F32),
        jax.ShapeDtypeStruct((n, N_STATE), F32),
        jax.ShapeDtypeStruct((POOL_HIST, n, BRANCH), F32),
    )
    out_specs = tuple(full(s.shape) for s in out_shape)
    return pl.pallas_call(
        _sample_mixer_kernel,
        grid=(1,),
        in_specs=in_specs, out_specs=out_specs, out_shape=out_shape,
        scratch_shapes=[pltpu.VMEM((n, N_STATE), F32), pltpu.VMEM((n, N_STATE), F32)],
        compiler_params=pltpu.CompilerParams(
            dimension_semantics=("arbitrary",), vmem_limit_bytes=VMEM_LIMIT),
        name="sample_mixer",
    )(x_s, p['g_mix_pre'], w_in, p['bblk'], p['cre'], p['cimn'], p['ar'], p['ai'],
      p['ssm_d'], p['w_glu'], p['b_glu'], p['pool_w'], p['pool_scale'], h0_re, h0_im, hist_t)


def _sample_attention_kernel(q_ref, k_ref, v_ref, ones_ref, o_ref):
    tiles = KV_ROWS // SUBLANES
    scale = MEM_HEAD_DIM ** -0.5

    def one_sample(b, carry):
        q8 = q_ref[b] * scale
        prod = k_ref[b].reshape(tiles, SUBLANES, MEM_HEAD_DIM) * q8[None]
        prod = prod.reshape(KV_ROWS, MEM_HEAD_DIM).astype(BF16)
        s = _dot(prod, ones_ref[...]).reshape(tiles, SUBLANES, MEM_HEAD_DIM)
        m8 = jnp.max(s, axis=0)
        m8 = jnp.maximum(m8, pltpu.roll(m8, MEM_HEADS, 0))
        e = jnp.exp(s - m8[None])
        l8 = jnp.sum(e, axis=0)
        acc = jnp.sum(e * v_ref[b].reshape(tiles, SUBLANES, MEM_HEAD_DIM), axis=0)
        l8 = l8 + pltpu.roll(l8, MEM_HEADS, 0)
        acc = acc + pltpu.roll(acc, MEM_HEADS, 0)
        o_ref[b] = acc / l8
        return carry

    lax.fori_loop(0, SAMPLE_BLOCK, one_sample, 0, unroll=True)


def _sample_attention(layer, q8, k_cache, v_cache, lane_ones):
    n = q8.shape[0]
    kv_spec = pl.BlockSpec((None, SAMPLE_BLOCK, KV_ROWS, MEM_HEAD_DIM), lambda i: (layer, i, 0, 0))
    q_spec = pl.BlockSpec((SAMPLE_BLOCK, SUBLANES, MEM_HEAD_DIM), lambda i: (i, 0, 0))
    return pl.pallas_call(
        _sample_attention_kernel,
        grid=(n // SAMPLE_BLOCK,),
        in_specs=[q_spec, kv_spec, kv_spec, _const_spec((MEM_HEAD_DIM, MEM_HEAD_DIM))],
        out_specs=q_spec,
        out_shape=jax.ShapeDtypeStruct((n, SUBLANES, MEM_HEAD_DIM), F32),
        compiler_params=pltpu.CompilerParams(
            dimension_semantics=("parallel",), vmem_limit_bytes=VMEM_LIMIT),
        name="sample_attention",
    )(q8, k_cache, v_cache, lane_ones)


def kernel(x_prompt, x_sample, mem_prompt, cache_mem_k, cache_mem_v, state_ssm_re, state_ssm_im, state_pool, g_mix_pre, g_mix_post, g_ffn_pre, g_ffn_post, g_mem, w_in, w_kv, ssm_lam_re, ssm_lam_im, ssm_log_dt, ssm_b_re, ssm_b_im, ssm_c_re, ssm_c_im, ssm_d, ssm_w_glu, ssm_b_glu, pool_w, pool_scale, w_branch_up, w_out, w_ffn_in, w_ffn_out):
    ar, ai, bblk = _discretise(ssm_lam_re, ssm_lam_im, ssm_log_dt, ssm_b_re, ssm_b_im)
    vec = lambda a: a.reshape(DEPTH, 1, a.shape[-1])
    p = {
        'g_mix_pre': vec(g_mix_pre), 'g_mix_post': vec(g_mix_post),
        'g_ffn_pre': vec(g_ffn_pre), 'g_ffn_post': vec(g_ffn_post),
        'bblk': bblk, 'ar': ar, 'ai': ai,
        'cre': _c_blocks(ssm_c_re), 'cimn': _c_blocks(-ssm_c_im),
        'ssm_d': vec(ssm_d), 'w_glu': ssm_w_glu.astype(BF16), 'b_glu': vec(ssm_b_glu),
        'pool_w': pool_w.astype(BF16), 'pool_scale': vec(pool_scale),
    }
    raw = {
        'w_in': w_in, 'w_up': w_branch_up.reshape(DEPTH, N_BRANCH * BRANCH, D_MODEL),
        'w_out': w_out, 'w_ffn_in': w_ffn_in, 'w_ffn_out': w_ffn_out,
    }
    k_mem, v_mem = _mem_kv(mem_prompt, vec(g_mem), w_kv)

    lane_ones = jnp.ones((MEM_HEAD_DIM, MEM_HEAD_DIM), BF16)
    k_cache = cache_mem_k.reshape(DEPTH, DEC_BATCH, KV_ROWS, MEM_HEAD_DIM)
    v_cache = cache_mem_v.reshape(DEPTH, DEC_BATCH, KV_ROWS, MEM_HEAD_DIM)
    h0_re = state_ssm_re.reshape(DEPTH, DEC_BATCH, N_STATE)
    h0_im = state_ssm_im.reshape(DEPTH, DEC_BATCH, N_STATE)
    hist_t = jnp.swapaxes(state_pool, 1, 2)

    xp = x_prompt
    xs = x_sample.reshape(DEC_BATCH, D_MODEL)
    n_prompt = BATCH * SEQ
    re_p, im_p, pool_p, re_s, im_s, pool_s = [], [], [], [], [], []
    for layer in range(DEPTH):
        (q, o_ssm, o_pool, hre, him, hist), w_bf = _mixer_a(layer, xp, p, raw)
        xp = _merge_ffn(layer, xp.reshape(n_prompt, D_MODEL), o_ssm.reshape(n_prompt, BRANCH),
                        o_pool.reshape(n_prompt, BRANCH),
                        (q.reshape(n_prompt, BRANCH), k_mem, v_mem), p, w_bf)
        xp = xp.reshape(BATCH, SEQ, D_MODEL)
        re_p.append(hre.reshape(BATCH, SSM_GROUPS, SSM_STATE))
        im_p.append(him.reshape(BATCH, SSM_GROUPS, SSM_STATE))
        pool_p.append(jnp.swapaxes(hist.reshape(POOL_HIST, BATCH, BRANCH), 0, 1))

        qs, os_ssm, os_pool, hre, him, nhist = _sample_mixer(layer, xs, p, w_in, h0_re, h0_im,
                                                             hist_t)
        qs4 = qs.reshape(DEC_BATCH, MEM_HEADS, MEM_HEAD_DIM)
        os_mem = _sample_attention(layer, jnp.concatenate([qs4, qs4], axis=1),
                                   k_cache, v_cache, lane_ones)
        os_mem = os_mem[:, :MEM_HEADS].reshape(DEC_BATCH, BRANCH).astype(BF16)
        xs = _merge_ffn(layer, xs, os_ssm, os_pool, os_mem, p, w_bf)
        re_s.append(hre.reshape(DEC_BATCH, SSM_GROUPS, SSM_STATE))
        im_s.append(him.reshape(DEC_BATCH, SSM_GROUPS, SSM_STATE))
        pool_s.append(jnp.swapaxes(nhist, 0, 1))

    y_prompt = xp
    y_sample = xs.reshape(DEC_BATCH, 1, D_MODEL)
    kv_shape = (DEPTH, BATCH, N_MEM, MEM_HEADS, MEM_HEAD_DIM)
    return (y_prompt, y_sample,
            jnp.stack(re_p), jnp.stack(im_p), jnp.stack(pool_p),
            k_mem.reshape(kv_shape), v_mem.reshape(kv_shape),
            jnp.stack(re_s), jnp.stack(im_s), jnp.stack(pool_s))
```

```python
import functools

import jax
import jax.numpy as jnp
from jax import lax
from jax.experimental import pallas as pl
from jax.experimental.pallas import tpu as pltpu

D_MODEL = 1024
BATCH = 8
SEQ = 2048
DEPTH = 4
DEC_BATCH = 128
PAST_LEN = 16384

BRANCH = D_MODEL // 2
SSM_GROUP = 16
SSM_GROUPS = BRANCH // SSM_GROUP
SSM_STATE = 64
N_STATE = SSM_GROUPS * SSM_STATE
POOL_WINDOWS = (2, 4, 8, 16)
POOL_GROUP_WIDTH = BRANCH // len(POOL_WINDOWS)
POOL_HIST = max(POOL_WINDOWS) - 1
N_MEM = 256
MEM_HEADS = 4
MEM_HEAD_DIM = BRANCH // MEM_HEADS
KV_ROWS = N_MEM * MEM_HEADS
N_BRANCH = 3
GATE_HALF = N_BRANCH * D_MODEL // 2
D_FF = 2816
RMS_EPS = 1e-6

LANES = 128
SUBLANES = 8
GROUPS_PER_BLOCK = LANES // SSM_GROUP
N_SSM_BLOCKS = BRANCH // LANES
BLOCK_STATE = GROUPS_PER_BLOCK * SSM_STATE

ROWS_A = 512
STEPS_A = ROWS_A // BATCH
ROWS_C = 512
MERGE_COLS = 256
TAIL_GROUPS = 2
PIECE_ROWS = 128
SAMPLE_BLOCK = 16
FF_CHUNKS = ((0, 1536), (1536, 2816))
VMEM_LIMIT = 56 * 1024 * 1024

BF16 = jnp.bfloat16
F32 = jnp.float32


def _rms(x, g):
    ms = jnp.mean(x * x, axis=-1, keepdims=True)
    return x * lax.rsqrt(ms + RMS_EPS) * g


def _dot(a, b):
    return jnp.dot(a, b, preferred_element_type=F32)


def _const_spec(shape):
    nd = len(shape)
    return pl.BlockSpec(shape, lambda *_: (0,) * nd, pipeline_mode=pl.Buffered(1))


def _layer_spec(shape, layer):
    nd = len(shape)
    return pl.BlockSpec((None,) + tuple(shape), lambda *_: (layer,) + (0,) * nd,
                        pipeline_mode=pl.Buffered(1))


def _discretise_kernel(lr_ref, li_ref, ldt_ref, br_ref, bi_ref,
                       ar_ref, ai_ref, bbr_ref, bbi_ref):
    lr = lr_ref[...]
    li = li_ref[...]
    dt = jnp.exp(ldt_ref[...])
    zr = lr * dt
    zi = li * dt
    mag = jnp.exp(zr)
    ar = mag * jnp.cos(zi)
    ai = mag * jnp.sin(zi)
    den = lr * lr + li * li
    fr = ((ar - 1.0) * lr + ai * li) / den
    fi = (ai * lr - (ar - 1.0) * li) / den
    br = br_ref[...]
    bi = bi_ref[...]
    ar_ref[...] = ar
    ai_ref[...] = ai
    bbr_ref[...] = fr * br - fi * bi
    bbi_ref[...] = fr * bi + fi * br


def _discretise(lam_re, lam_im, log_dt, b_re, b_im):
    rows = DEPTH * SSM_GROUPS
    width = SSM_GROUP * SSM_STATE

    def tile_p(a):
        return jnp.tile(a.reshape(rows, 1, SSM_STATE), (1, SSM_GROUP, 1)).reshape(rows, width)

    def b_t(a):
        return jnp.swapaxes(a, -1, -2).reshape(rows, width)

    ldt = jnp.broadcast_to(log_dt.reshape(rows, 1), (rows, width))
    out = jax.ShapeDtypeStruct((rows, width), F32)
    ar, ai, bbr, bbi = pl.pallas_call(
        _discretise_kernel, out_shape=(out, out, out, out), name="s5_discretise",
    )(tile_p(lam_re), tile_p(lam_im), ldt, b_t(b_re), b_t(b_im))
    ar = ar[:, :SSM_STATE].reshape(DEPTH, 1, N_STATE)
    ai = ai[:, :SSM_STATE].reshape(DEPTH, 1, N_STATE)
    shape5 = (DEPTH, N_SSM_BLOCKS, GROUPS_PER_BLOCK, SSM_GROUP, SSM_STATE)
    bblk = jnp.concatenate([_group_block_diag(bbr.reshape(shape5)),
                            _group_block_diag(bbi.reshape(shape5))], axis=-1)
    return ar, ai, bblk


def _group_block_diag(m):
    depth, blocks, groups, n_r, n_c = m.shape
    same = jnp.eye(groups, dtype=bool)[None, None, :, None, :, None]
    out = jnp.where(same, m[:, :, :, :, None, :], 0.0).astype(BF16)
    return out.reshape(depth, blocks, groups * n_r, groups * n_c)


def _c_blocks(c):
    shape5 = (DEPTH, N_SSM_BLOCKS, GROUPS_PER_BLOCK, SSM_GROUP, SSM_STATE)
    return _group_block_diag(jnp.swapaxes(c.reshape(shape5), -1, -2))


def _ssm_input(u_block, bblk, bu_re_ref, bu_im_ref, sl):
    bu = _dot(u_block.astype(BF16), bblk)
    bu_re_ref[:, sl] = bu[:, :BLOCK_STATE]
    bu_im_ref[:, sl] = bu[:, BLOCK_STATE:]


def _ssm_readout(h_re_ref, h_im_ref, sl, cre, cimn):
    return _dot(h_re_ref[:, sl].astype(BF16), cre) + _dot(h_im_ref[:, sl].astype(BF16), cimn)


def _ssm_gate(y, u_ssm, d_ref, wglu_ref, bglu_ref):
    y = jax.nn.gelu(y + d_ref[...] * u_ssm)
    return y * jax.nn.sigmoid(_dot(y.astype(BF16), wglu_ref[...]) + bglu_ref[...])


def _pool_mix(pooled, poolw_ref, pscale_ref):
    outs = []
    for gi in range(len(POOL_WINDOWS)):
        sl = slice(gi * POOL_GROUP_WIDTH, (gi + 1) * POOL_GROUP_WIDTH)
        outs.append(_dot(pooled[gi].astype(BF16), poolw_ref[gi]) * pscale_ref[:, sl])
    return jnp.concatenate(outs, axis=-1)


def _mixer_a_kernel(x_ref, g_ref, win_ref, bblk_ref, cre_ref, cimn_ref, ar_ref, ai_ref,
                    d_ref, wglu_ref, bglu_ref, poolw_ref, pscale_ref,
                    wgate_lo_ref, wgate_hi_ref, wup_ref, wout_ref, wfin_ref, wfout_ref,
                    q_ref, ossm_ref, opool_ref, hre_ref, him_ref, hist_ref,
                    wgate_bf_ref, wup_bf_ref, wout_bf_ref, wfin_bf_ref, wfout_bf_ref,
                    bu_re_ref, bu_im_ref, pool_buf, ussm_tm, ossm_tm, opool_tm, win_bf):
    i = pl.program_id(0)
    hist_rows = POOL_HIST * BATCH

    wgate_bf_ref[:, :GATE_HALF] = wgate_lo_ref[...].astype(BF16)
    wgate_bf_ref[:, GATE_HALF:] = wgate_hi_ref[...].astype(BF16)
    wup_bf_ref[...] = wup_ref[...].astype(BF16)
    wout_bf_ref[...] = wout_ref[...].astype(BF16)
    wfin_bf_ref[...] = wfin_ref[...].astype(BF16)
    wfout_bf_ref[...] = wfout_ref[...].astype(BF16)

    @pl.when(i == 0)
    def _():
        hre_ref[...] = jnp.zeros_like(hre_ref)
        him_ref[...] = jnp.zeros_like(him_ref)
        pool_buf[:, 0:hist_rows, :] = jnp.zeros((N_SSM_BLOCKS, hist_rows, LANES), F32)
        win_bf[...] = win_ref[...].astype(BF16)

    x = x_ref[...].reshape(ROWS_A, D_MODEL)
    h = _rms(x, g_ref[...]).astype(BF16)
    proj = _dot(h, win_bf[...])
    for b in range(BATCH):
        rows = slice(b * STEPS_A, (b + 1) * STEPS_A)
        q_ref[b] = proj[rows, 2 * BRANCH:].astype(BF16)
        for k in range(N_SSM_BLOCKS):
            ussm_tm[k, pl.ds(b, STEPS_A, stride=BATCH), :] = proj[rows, k * LANES:(k + 1) * LANES]
            pool_buf[k, pl.ds(hist_rows + b, STEPS_A, stride=BATCH), :] = (
                proj[rows, BRANCH + k * LANES:BRANCH + (k + 1) * LANES])
    u_ssm = jnp.concatenate([ussm_tm[k] for k in range(N_SSM_BLOCKS)], axis=-1)

    ys = []
    for k in range(N_SSM_BLOCKS):
        sl = slice(k * BLOCK_STATE, (k + 1) * BLOCK_STATE)
        _ssm_input(ussm_tm[k], bblk_ref[k], bu_re_ref, bu_im_ref, sl)
        a_r = jnp.broadcast_to(ar_ref[:, sl], (BATCH, BLOCK_STATE))
        a_i = jnp.broadcast_to(ai_ref[:, sl], (BATCH, BLOCK_STATE))
        hr = hre_ref[:, sl]
        hi = him_ref[:, sl]
        for t in range(STEPS_A):
            rows = slice(t * BATCH, (t + 1) * BATCH)
            hr, hi = (a_r * hr - a_i * hi + bu_re_ref[rows, sl],
                      a_r * hi + a_i * hr + bu_im_ref[rows, sl])
            bu_re_ref[rows, sl] = hr
            bu_im_ref[rows, sl] = hi
        hre_ref[:, sl] = hr
        him_ref[:, sl] = hi
        ys.append(_ssm_readout(bu_re_ref, bu_im_ref, sl, cre_ref[k], cimn_ref[k]))
    o_ssm = _ssm_gate(jnp.concatenate(ys, axis=-1), u_ssm, d_ref, wglu_ref, bglu_ref)

    t_pos = i * STEPS_A + lax.broadcasted_iota(jnp.int32, (ROWS_A, POOL_GROUP_WIDTH), 0) // BATCH
    pooled = []
    for gi, w in enumerate(POOL_WINDOWS):
        u_g = pool_buf[gi, hist_rows:hist_rows + ROWS_A, :]
        acc = u_g
        for j in range(1, w):
            start = hist_rows - j * BATCH
            acc = acc + pool_buf[gi, start:start + ROWS_A, :]
        cnt = jnp.minimum(t_pos + 1, w).astype(F32)
        pooled.append(acc / cnt - u_g)
    o_pool = _pool_mix(pooled, poolw_ref, pscale_ref)
    pool_buf[:, 0:hist_rows, :] = pool_buf[:, ROWS_A:ROWS_A + hist_rows, :]

    for k in range(N_SSM_BLOCKS):
        ossm_tm[k] = o_ssm[:, k * LANES:(k + 1) * LANES]
        opool_tm[k] = o_pool[:, k * LANES:(k + 1) * LANES]
    for b in range(BATCH):
        rows = pl.ds(b, STEPS_A, stride=BATCH)
        ossm_ref[b] = jnp.concatenate(
            [ossm_tm[k, rows, :] for k in range(N_SSM_BLOCKS)], axis=-1).astype(BF16)
        opool_ref[b] = jnp.concatenate(
            [opool_tm[k, rows, :] for k in range(N_SSM_BLOCKS)], axis=-1).astype(BF16)

    @pl.when(i == pl.num_programs(0) - 1)
    def _():
        for k in range(N_SSM_BLOCKS):
            hist_ref[:, k * LANES:(k + 1) * LANES] = pool_buf[k, 0:hist_rows, :]


def _mixer_a(layer, x, p, raw):
    hist_rows = POOL_HIST * BATCH
    n_steps = SEQ // STEPS_A
    seq_spec = lambda w: pl.BlockSpec((BATCH, STEPS_A, w), lambda i: (0, i, 0))
    cast_plan = {
        'w_gate': (D_MODEL // n_steps, N_BRANCH * D_MODEL, 1),
        'w_up': (N_BRANCH * BRANCH // n_steps, D_MODEL, 1),
        'w_out': (D_MODEL // n_steps, D_MODEL, 1),
        'w_ffn_in': (D_MODEL // n_steps, 2 * D_FF, 1),
        'w_ffn_out': (2 * D_FF // n_steps, D_MODEL, 2),
    }

    def chunk_in(name, width=None, col=0):
        rows, full_width, per = cast_plan[name]
        return pl.BlockSpec((None, rows, width or full_width),
                            lambda i: (layer, i // per, col))

    def chunk_out(name):
        rows, width, per = cast_plan[name]
        return pl.BlockSpec((rows, width), lambda i: (i // per, 0))

    cast_names = ('w_gate', 'w_up', 'w_out', 'w_ffn_in', 'w_ffn_out')
    cast_shapes = tuple(
        jax.ShapeDtypeStruct((cast_plan[n][0] * n_steps // cast_plan[n][2], cast_plan[n][1]), BF16)
        for n in cast_names)
    in_specs = [
        seq_spec(D_MODEL),
        _layer_spec((1, D_MODEL), layer),
        _layer_spec((D_MODEL, 3 * BRANCH), layer),
        _layer_spec((N_SSM_BLOCKS, LANES, 2 * BLOCK_STATE), layer),
        _layer_spec((N_SSM_BLOCKS, BLOCK_STATE, LANES), layer),
        _layer_spec((N_SSM_BLOCKS, BLOCK_STATE, LANES), layer),
        _layer_spec((1, N_STATE), layer),
        _layer_spec((1, N_STATE), layer),
        _layer_spec((1, BRANCH), layer),
        _layer_spec((BRANCH, BRANCH), layer),
        _layer_spec((1, BRANCH), layer),
        _layer_spec((len(POOL_WINDOWS), POOL_GROUP_WIDTH, POOL_GROUP_WIDTH), layer),
        _layer_spec((1, BRANCH), layer),
        chunk_in('w_gate', GATE_HALF, 1), chunk_in('w_gate', GATE_HALF, 2),
        chunk_in('w_up'), chunk_in('w_out'), chunk_in('w_ffn_in'), chunk_in('w_ffn_out'),
    ]
    branch_out = jax.ShapeDtypeStruct((BATCH, SEQ, BRANCH), BF16)
    out_shape = (
        branch_out, branch_out, branch_out,
        jax.ShapeDtypeStruct((BATCH, N_STATE), F32),
        jax.ShapeDtypeStruct((BATCH, N_STATE), F32),
        jax.ShapeDtypeStruct((hist_rows, BRANCH), F32),
    ) + cast_shapes
    out_specs = (
        seq_spec(BRANCH), seq_spec(BRANCH), seq_spec(BRANCH),
        pl.BlockSpec((BATCH, N_STATE), lambda i: (0, 0)),
        pl.BlockSpec((BATCH, N_STATE), lambda i: (0, 0)),
        pl.BlockSpec((hist_rows, BRANCH), lambda i: (0, 0)),
    ) + tuple(chunk_out(n) for n in cast_names)
    outs = pl.pallas_call(
        _mixer_a_kernel,
        grid=(n_steps,),
        in_specs=in_specs, out_specs=out_specs, out_shape=out_shape,
        scratch_shapes=[
            pltpu.VMEM((ROWS_A, N_STATE), F32),
            pltpu.VMEM((ROWS_A, N_STATE), F32),
            pltpu.VMEM((N_SSM_BLOCKS, hist_rows + ROWS_A, LANES), F32),
            pltpu.VMEM((N_SSM_BLOCKS, ROWS_A, LANES), F32),
            pltpu.VMEM((N_SSM_BLOCKS, ROWS_A, LANES), F32),
            pltpu.VMEM((N_SSM_BLOCKS, ROWS_A, LANES), F32),
            pltpu.VMEM((D_MODEL, 3 * BRANCH), BF16),
        ],
        compiler_params=pltpu.CompilerParams(
            dimension_semantics=("arbitrary",), vmem_limit_bytes=VMEM_LIMIT),
        name="prompt_mixer_a",
    )(x, p['g_mix_pre'], raw['w_in'], p['bblk'], p['cre'], p['cimn'], p['ar'], p['ai'],
      p['ssm_d'], p['w_glu'], p['b_glu'], p['pool_w'], p['pool_scale'],
      raw['w_in'], raw['w_in'], raw['w_up'], raw['w_out'], raw['w_ffn_in'], raw['w_ffn_out'])
    return outs[:6], dict(zip(cast_names, outs[6:]))


def _mem_kv_kernel(mem_ref, g_ref, wkv_ref, k_ref, v_ref):
    wkv = wkv_ref[...].astype(BF16)
    for b in range(BATCH):
        kv = _dot(_rms(mem_ref[b], g_ref[...]).astype(BF16), wkv)
        for hd in range(MEM_HEADS):
            rows = pl.ds(hd, N_MEM, stride=MEM_HEADS)
            k_ref[b, rows, :] = kv[:, hd * MEM_HEAD_DIM:(hd + 1) * MEM_HEAD_DIM]
            v_ref[b, rows, :] = kv[:, BRANCH + hd * MEM_HEAD_DIM:BRANCH + (hd + 1) * MEM_HEAD_DIM]


def _mem_kv(mem, g_mem, w_kv):
    out = jax.ShapeDtypeStruct((DEPTH, BATCH, KV_ROWS, MEM_HEAD_DIM), F32)
    out_spec = pl.BlockSpec((None, BATCH, KV_ROWS, MEM_HEAD_DIM), lambda l: (l, 0, 0, 0))
    return pl.pallas_call(
        _mem_kv_kernel,
        grid=(DEPTH,),
        in_specs=[
            _const_spec((BATCH, N_MEM, D_MODEL)),
            pl.BlockSpec((None, 1, D_MODEL), lambda l: (l, 0, 0)),
            pl.BlockSpec((None, D_MODEL, 2 * BRANCH), lambda l: (l, 0, 0)),
        ],
        out_specs=(out_spec, out_spec), out_shape=(out, out),
        compiler_params=pltpu.CompilerParams(
            dimension_semantics=("arbitrary",), vmem_limit_bytes=VMEM_LIMIT),
        name="mem_kv",
    )(mem, g_mem, w_kv)


def _memory_attention_stages(q_ref, k_ref, v_ref, result):
    scale = MEM_HEAD_DIM ** -0.5
    head_rows = [pl.ds(hd, N_MEM, stride=MEM_HEADS) for hd in range(MEM_HEADS)]
    head_cols = [slice(hd * MEM_HEAD_DIM, (hd + 1) * MEM_HEAD_DIM) for hd in range(MEM_HEADS)]
    scores = [lax.dot_general(q_ref[:, head_cols[hd]], k_ref[head_rows[hd], :].astype(BF16),
                              (((1,), (1,)), ((), ())), preferred_element_type=F32) * scale
              for hd in range(MEM_HEADS)]
    yield
    probs = []
    for s in scores:
        e = jnp.exp(s - jnp.max(s, axis=-1, keepdims=True))
        probs.append((e / jnp.sum(e, axis=-1, keepdims=True)).astype(BF16))
        yield
    outs = []
    for hd in range(MEM_HEADS):
        outs.append(_dot(probs[hd], v_ref[head_rows[hd], :].astype(BF16)).astype(BF16))
        yield
    result.append(jnp.concatenate(outs, axis=-1))


def _merge_ffn_kernel(with_attention, x_ref, ossm_ref, opool_ref, *refs):
    o_mem = []
    if with_attention:
        attention = _memory_attention_stages(*refs[:3], o_mem)
        refs = refs[3:]
    else:
        attention = iter(())
        o_mem.append(refs[0][...])
        refs = refs[1:]
    (gpre_ref, wgate_ref, wup_ref, wout_ref, gpost_ref, gfpre_ref, wfin_ref, wfout_ref,
     gfpost_ref, y_ref) = refs
    x = x_ref[...]
    h = _rms(x, gpre_ref[...]).astype(BF16)

    def branch_term(b, o_b, cols):
        gate_cols = slice(b * D_MODEL + cols.start, b * D_MODEL + cols.stop)
        gate = jax.nn.sigmoid(_dot(h, wgate_ref[:, gate_cols]))
        return gate * _dot(o_b, wup_ref[b * BRANCH:(b + 1) * BRANCH, cols])

    col_chunks = [slice(c, c + MERGE_COLS) for c in range(0, D_MODEL, MERGE_COLS)]
    next(attention, None)
    partial = []
    for b, o_b in enumerate((ossm_ref[...], opool_ref[...])):
        for ci, cols in enumerate(col_chunks):
            term = branch_term(b, o_b, cols)
            if b == 0:
                partial.append(term)
            else:
                partial[ci] = partial[ci] + term
            next(attention, None)
    for _ in attention:
        pass
    merged = jnp.concatenate(
        [partial[ci] + branch_term(N_BRANCH - 1, o_mem[0], cols)
         for ci, cols in enumerate(col_chunks)], axis=-1).astype(BF16)

    def tail_stages(rows):
        m2 = _dot(merged[rows, :], wout_ref[...])
        yield
        x1 = x[rows, :] + _rms(m2, gpost_ref[...])
        hf = _rms(x1, gfpre_ref[...]).astype(BF16)
        yield
        f = None
        for lo, hi in FF_CHUNKS:
            hg = _dot(hf, wfin_ref[:, lo:hi])
            hu = _dot(hf, wfin_ref[:, D_FF + lo:D_FF + hi])
            yield
            act = (jax.nn.silu(hg) * hu).astype(BF16)
            yield
            part = _dot(act, wfout_ref[lo:hi, :])
            f = part if f is None else f + part
            yield
        y_ref[rows, :] = x1 + _rms(f, gfpost_ref[...])

    n_groups = TAIL_GROUPS if x.shape[0] >= TAIL_GROUPS * PIECE_ROWS else 1
    group_rows = x.shape[0] // n_groups
    waiting = [tail_stages(slice(g * group_rows, (g + 1) * group_rows)) for g in range(n_groups)]
    running = []
    while waiting or running:
        if waiting:
            running.append(waiting.pop(0))
        for chain in list(running):
            if next(chain, StopIteration) is StopIteration:
                running.remove(chain)


def _merge_ffn(layer, x, o_ssm, o_pool, third, p, w):
    n_rows = x.shape[0]
    rows = min(ROWS_C, n_rows)
    row_spec = lambda w: pl.BlockSpec((rows, w), lambda i: (i, 0))
    with_attention = isinstance(third, tuple)
    if with_attention:
        tiles_per_seq = SEQ // rows
        kv_spec = pl.BlockSpec((None, None, KV_ROWS, MEM_HEAD_DIM),
                               lambda i: (layer, i // tiles_per_seq, 0, 0))
        third_specs = [row_spec(BRANCH), kv_spec, kv_spec]
    else:
        third = (third,)
        third_specs = [row_spec(BRANCH)]
    in_specs = [row_spec(D_MODEL), row_spec(BRANCH), row_spec(BRANCH)] + third_specs + [
        _layer_spec((1, D_MODEL), layer),
        _const_spec((D_MODEL, N_BRANCH * D_MODEL)),
        _const_spec((N_BRANCH * BRANCH, D_MODEL)),
        _const_spec((D_MODEL, D_MODEL)),
        _layer_spec((1, D_MODEL), layer),
        _layer_spec((1, D_MODEL), layer),
        _const_spec((D_MODEL, 2 * D_FF)),
        _const_spec((D_FF, D_MODEL)),
        _layer_spec((1, D_MODEL), layer),
    ]
    return pl.pallas_call(
        functools.partial(_merge_ffn_kernel, with_attention),
        grid=(n_rows // rows,),
        in_specs=in_specs, out_specs=row_spec(D_MODEL),
        out_shape=jax.ShapeDtypeStruct((n_rows, D_MODEL), F32),
        compiler_params=pltpu.CompilerParams(
            dimension_semantics=("parallel",), vmem_limit_bytes=VMEM_LIMIT),
        name="merge_ffn",
    )(x, o_ssm, o_pool, *third, p['g_mix_pre'], w['w_gate'], w['w_up'], w['w_out'],
      p['g_mix_post'], p['g_ffn_pre'], w['w_ffn_in'], w['w_ffn_out'], p['g_ffn_post'])


def _sample_mixer_kernel(x_ref, g_ref, win_ref, bblk_ref, cre_ref, cimn_ref, ar_ref, ai_ref,
                         d_ref, wglu_ref, bglu_ref, poolw_ref, pscale_ref,
                         h0re_ref, h0im_ref, hist_ref,
                         q_ref, ossm_ref, opool_ref, hre_ref, him_ref, nhist_ref,
                         bu_re_ref, bu_im_ref):
    h = _rms(x_ref[...], g_ref[...]).astype(BF16)
    proj = _dot(h, win_ref[...].astype(BF16))
    u_ssm = proj[:, :BRANCH]
    u_pool = proj[:, BRANCH:2 * BRANCH]
    q_ref[...] = proj[:, 2 * BRANCH:]

    ys = []
    for k in range(N_SSM_BLOCKS):
        sl = slice(k * BLOCK_STATE, (k + 1) * BLOCK_STATE)
        _ssm_input(u_ssm[:, k * LANES:(k + 1) * LANES], bblk_ref[k], bu_re_ref, bu_im_ref, sl)
        a_r = ar_ref[:, sl]
        a_i = ai_ref[:, sl]
        h0r = h0re_ref[:, sl]
        h0i = h0im_ref[:, sl]
        hre_ref[:, sl] = bu_re_ref[:, sl] + a_r * h0r - a_i * h0i
        him_ref[:, sl] = bu_im_ref[:, sl] + a_r * h0i + a_i * h0r
        ys.append(_ssm_readout(hre_ref, him_ref, sl, cre_ref[k], cimn_ref[k]))
    y = _ssm_gate(jnp.concatenate(ys, axis=-1), u_ssm, d_ref, wglu_ref, bglu_ref)
    ossm_ref[...] = y.astype(BF16)

    pooled = []
    for gi, w in enumerate(POOL_WINDOWS):
        sl = slice(gi * POOL_GROUP_WIDTH, (gi + 1) * POOL_GROUP_WIDTH)
        acc = u_pool[:, sl]
        for j in range(1, w):
            acc = acc + hist_ref[POOL_HIST - j, :, sl]
        cnt = float(min(PAST_LEN + 1, w))
        pooled.append(acc / cnt - u_pool[:, sl])
    opool_ref[...] = _pool_mix(pooled, poolw_ref, pscale_ref).astype(BF16)
    for j in range(POOL_HIST - 1):
        nhist_ref[j] = hist_ref[j + 1]
    nhist_ref[POOL_HIST - 1] = u_pool


def _sample_mixer(layer, x_s, p, w_in, h0_re, h0_im, hist_t):
    n = x_s.shape[0]
    full = lambda shape: pl.BlockSpec(shape, lambda i: (0,) * len(shape))
    in_specs = [
        full((n, D_MODEL)),
        _layer_spec((1, D_MODEL), layer),
        _layer_spec((D_MODEL, 3 * BRANCH), layer),
        _layer_spec((N_SSM_BLOCKS, LANES, 2 * BLOCK_STATE), layer),
        _layer_spec((N_SSM_BLOCKS, BLOCK_STATE, LANES), layer),
        _layer_spec((N_SSM_BLOCKS, BLOCK_STATE, LANES), layer),
        _layer_spec((1, N_STATE), layer),
        _layer_spec((1, N_STATE), layer),
        _layer_spec((1, BRANCH), layer),
        _layer_spec((BRANCH, BRANCH), layer),
        _layer_spec((1, BRANCH), layer),
        _layer_spec((len(POOL_WINDOWS), POOL_GROUP_WIDTH, POOL_GROUP_WIDTH), layer),
        _layer_spec((1, BRANCH), layer),
        _layer_spec((n, N_STATE), layer),
        _layer_spec((n, N_STATE), layer),
        _layer_spec((POOL_HIST, n, BRANCH), layer),
    ]
    out_shape = (
        jax.ShapeDtypeStruct((n, BRANCH), F32),
        jax.ShapeDtypeStruct((n, BRANCH), BF16),
        jax.ShapeDtypeStruct((n, BRANCH), BF16),
        jax.ShapeDtypeStruct((n, N_STATE), F32),
        jax.ShapeDtypeStruct((n, N_STATE), F32),
        jax.ShapeDtypeStruct((POOL_HIST, n, BRANCH), F32),
    )
    out_specs = tuple(full(s.shape) for s in out_shape)
    return pl.pallas_call(
        _sample_mixer_kernel,
        grid=(1,),
        in_specs=in_specs, out_specs=out_specs, out_shape=out_shape,
        scratch_shapes=[pltpu.VMEM((n, N_STATE), F32), pltpu.VMEM((n, N_STATE), F32)],
        compiler_params=pltpu.CompilerParams(
            dimension_semantics=("arbitrary",), vmem_limit_bytes=VMEM_LIMIT),
        name="sample_mixer",
    )(x_s, p['g_mix_pre'], w_in, p['bblk'], p['cre'], p['cimn'], p['ar'], p['ai'],
      p['ssm_d'], p['w_glu'], p['b_glu'], p['pool_w'], p['pool_scale'], h0_re, h0_im, hist_t)


def _sample_attention_kernel(q_ref, k_ref, v_ref, ones_ref, o_ref):
    tiles = KV_ROWS // SUBLANES
    scale = MEM_HEAD_DIM ** -0.5

    def one_sample(b, carry):
        q8 = q_ref[b] * scale
        prod = k_ref[b].reshape(tiles, SUBLANES, MEM_HEAD_DIM) * q8[None]
        prod = prod.reshape(KV_ROWS, MEM_HEAD_DIM).astype(BF16)
        s = _dot(prod, ones_ref[...]).reshape(tiles, SUBLANES, MEM_HEAD_DIM)
        m8 = jnp.max(s, axis=0)
        m8 = jnp.maximum(m8, pltpu.roll(m8, MEM_HEADS, 0))
        e = jnp.exp(s - m8[None])
        l8 = jnp.sum(e, axis=0)
        acc = jnp.sum(e * v_ref[b].reshape(tiles, SUBLANES, MEM_HEAD_DIM), axis=0)
        l8 = l8 + pltpu.roll(l8, MEM_HEADS, 0)
        acc = acc + pltpu.roll(acc, MEM_HEADS, 0)
        o_ref[b] = acc / l8
        return carry

    lax.fori_loop(0, SAMPLE_BLOCK, one_sample, 0, unroll=True)


def _sample_attention(layer, q8, k_cache, v_cache, lane_ones):
    n = q8.shape[0]
    kv_spec = pl.BlockSpec((None, SAMPLE_BLOCK, KV_ROWS, MEM_HEAD_DIM), lambda i: (layer, i, 0, 0))
    q_spec = pl.BlockSpec((SAMPLE_BLOCK, SUBLANES, MEM_HEAD_DIM), lambda i: (i, 0, 0))
    return pl.pallas_call(
        _sample_attention_kernel,
        grid=(n // SAMPLE_BLOCK,),
        in_specs=[q_spec, kv_spec, kv_spec, _const_spec((MEM_HEAD_DIM, MEM_HEAD_DIM))],
        out_specs=q_spec,
        out_shape=jax.ShapeDtypeStruct((n, SUBLANES, MEM_HEAD_DIM), F32),
        compiler_params=pltpu.CompilerParams(
            dimension_semantics=("parallel",), vmem_limit_bytes=VMEM_LIMIT),
        name="sample_attention",
    )(q8, k_cache, v_cache, lane_ones)


def kernel(x_prompt, x_sample, mem_prompt, cache_mem_k, cache_mem_v, state_ssm_re, state_ssm_im, state_pool, g_mix_pre, g_mix_post, g_ffn_pre, g_ffn_post, g_mem, w_in, w_kv, ssm_lam_re, ssm_lam_im, ssm_log_dt, ssm_b_re, ssm_b_im, ssm_c_re, ssm_c_im, ssm_d, ssm_w_glu, ssm_b_glu, pool_w, pool_scale, w_branch_up, w_out, w_ffn_in, w_ffn_out):
    ar, ai, bblk = _discretise(ssm_lam_re, ssm_lam_im, ssm_log_dt, ssm_b_re, ssm_b_im)
    vec = lambda a: a.reshape(DEPTH, 1, a.shape[-1])
    p = {
        'g_mix_pre': vec(g_mix_pre), 'g_mix_post': vec(g_mix_post),
        'g_ffn_pre': vec(g_ffn_pre), 'g_ffn_post': vec(g_ffn_post),
        'bblk': bblk, 'ar': ar, 'ai': ai,
        'cre': _c_blocks(ssm_c_re), 'cimn': _c_blocks(-ssm_c_im),
        'ssm_d': vec(ssm_d), 'w_glu': ssm_w_glu.astype(BF16), 'b_glu': vec(ssm_b_glu),
        'pool_w': pool_w.astype(BF16), 'pool_scale': vec(pool_scale),
    }
    raw = {
        'w_in': w_in, 'w_up': w_branch_up.reshape(DEPTH, N_BRANCH * BRANCH, D_MODEL),
        'w_out': w_out, 'w_ffn_in': w_ffn_in, 'w_ffn_out': w_ffn_out,
    }
    k_mem, v_mem = _mem_kv(mem_prompt, vec(g_mem), w_kv)

    lane_ones = jnp.ones((MEM_HEAD_DIM, MEM_HEAD_DIM), BF16)
    k_cache = cache_mem_k.reshape(DEPTH, DEC_BATCH, KV_ROWS, MEM_HEAD_DIM)
    v_cache = cache_mem_v.reshape(DEPTH, DEC_BATCH, KV_ROWS, MEM_HEAD_DIM)
    h0_re = state_ssm_re.reshape(DEPTH, DEC_BATCH, N_STATE)
    h0_im = state_ssm_im.reshape(DEPTH, DEC_BATCH, N_STATE)
    hist_t = jnp.swapaxes(state_pool, 1, 2)

    xp = x_prompt
    xs = x_sample.reshape(DEC_BATCH, D_MODEL)
    n_prompt = BATCH * SEQ
    re_p, im_p, pool_p, re_s, im_s, pool_s = [], [], [], [], [], []
    for layer in range(DEPTH):
        (q, o_ssm, o_pool, hre, him, hist), w_bf = _mixer_a(layer, xp, p, raw)
        xp = _merge_ffn(layer, xp.reshape(n_prompt, D_MODEL), o_ssm.reshape(n_prompt, BRANCH),
                        o_pool.reshape(n_prompt, BRANCH),
                        (q.reshape(n_prompt, BRANCH), k_mem, v_mem), p, w_bf)
        xp = xp.reshape(BATCH, SEQ, D_MODEL)
        re_p.append(hre.reshape(BATCH, SSM_GROUPS, SSM_STATE))
        im_p.append(him.reshape(BATCH, SSM_GROUPS, SSM_STATE))
        pool_p.append(jnp.swapaxes(hist.reshape(POOL_HIST, BATCH, BRANCH), 0, 1))

        qs, os_ssm, os_pool, hre, him, nhist = _sample_mixer(layer, xs, p, w_in, h0_re, h0_im,
                                                             hist_t)
        qs4 = qs.reshape(DEC_BATCH, MEM_HEADS, MEM_HEAD_DIM)
        os_mem = _sample_attention(layer, jnp.concatenate([qs4, qs4], axis=1),
                                   k_cache, v_cache, lane_ones)
        os_mem = os_mem[:, :MEM_HEADS].reshape(DEC_BATCH, BRANCH).astype(BF16)
        xs = _merge_ffn(layer, xs, os_ssm, os_pool, os_mem, p, w_bf)
        re_s.append(hre.reshape(DEC_BATCH, SSM_GROUPS, SSM_STATE))
        im_s.append(him.reshape(DEC_BATCH, SSM_GROUPS, SSM_STATE))
        pool_s.append(jnp.swapaxes(nhist, 0, 1))

    y_prompt = xp
    y_sample = xs.reshape(DEC_BATCH, 1, D_MODEL)
    kv_shape = (DEPTH, BATCH, N_MEM, MEM_HEADS, MEM_HEAD_DIM)
    return (y_prompt, y_sample,
            jnp.stack(re_p), jnp.stack(im_p), jnp.stack(pool_p),
            k_mem.reshape(kv_shape), v_mem.reshape(kv_shape),
            jnp.stack(re_s), jnp.stack(im_s), jnp.stack(pool_s))
```

```python
import functools

import jax
import jax.numpy as jnp
from jax import lax
from jax.experimental import pallas as pl
from jax.experimental.pallas import tpu as pltpu

D_MODEL = 1024
BATCH = 8
SEQ = 2048
DEPTH = 4
DEC_BATCH = 128
PAST_LEN = 16384

BRANCH = D_MODEL // 2
SSM_GROUP = 16
SSM_GROUPS = BRANCH // SSM_GROUP
SSM_STATE = 64
N_STATE = SSM_GROUPS * SSM_STATE
POOL_WINDOWS = (2, 4, 8, 16)
POOL_GROUP_WIDTH = BRANCH // len(POOL_WINDOWS)
POOL_HIST = max(POOL_WINDOWS) - 1
N_MEM = 256
MEM_HEADS = 4
MEM_HEAD_DIM = BRANCH // MEM_HEADS
KV_ROWS = N_MEM * MEM_HEADS
N_BRANCH = 3
GATE_HALF = N_BRANCH * D_MODEL // 2
D_FF = 2816
RMS_EPS = 1e-6

LANES = 128
SUBLANES = 8
GROUPS_PER_BLOCK = LANES // SSM_GROUP
N_SSM_BLOCKS = BRANCH // LANES
BLOCK_STATE = GROUPS_PER_BLOCK * SSM_STATE

ROWS_A = 512
STEPS_A = ROWS_A // BATCH
ROWS_C = 512
MERGE_COLS = 256
TAIL_GROUPS = 2
PIECE_ROWS = 128
SAMPLE_BLOCK = 16
FF_CHUNKS = ((0, 1536), (1536, 2816))
VMEM_LIMIT = 56 * 1024 * 1024

BF16 = jnp.bfloat16
F32 = jnp.float32


def _rms(x, g):
    ms = jnp.mean(x * x, axis=-1, keepdims=True)
    return x * lax.rsqrt(ms + RMS_EPS) * g


def _dot(a, b):
    return jnp.dot(a, b, preferred_element_type=F32)


def _const_spec(shape):
    nd = len(shape)
    return pl.BlockSpec(shape, lambda *_: (0,) * nd, pipeline_mode=pl.Buffered(1))


def _layer_spec(shape, layer):
    nd = len(shape)
    return pl.BlockSpec((None,) + tuple(shape), lambda *_: (layer,) + (0,) * nd,
                        pipeline_mode=pl.Buffered(1))


def _discretise_kernel(lr_ref, li_ref, ldt_ref, br_ref, bi_ref,
                       ar_ref, ai_ref, bbr_ref, bbi_ref):
    lr = lr_ref[...]
    li = li_ref[...]
    dt = jnp.exp(ldt_ref[...])
    zr = lr * dt
    zi = li * dt
    mag = jnp.exp(zr)
    ar = mag * jnp.cos(zi)
    ai = mag * jnp.sin(zi)
    den = lr * lr + li * li
    fr = ((ar - 1.0) * lr + ai * li) / den
    fi = (ai * lr - (ar - 1.0) * li) / den
    br = br_ref[...]
    bi = bi_ref[...]
    ar_ref[...] = ar
    ai_ref[...] = ai
    bbr_ref[...] = fr * br - fi * bi
    bbi_ref[...] = fr * bi + fi * br


def _discretise(lam_re, lam_im, log_dt, b_re, b_im):
    rows = DEPTH * SSM_GROUPS
    width = SSM_GROUP * SSM_STATE

    def tile_p(a):
        return jnp.tile(a.reshape(rows, 1, SSM_STATE), (1, SSM_GROUP, 1)).reshape(rows, width)

    def b_t(a):
        return jnp.swapaxes(a, -1, -2).reshape(rows, width)

    ldt = jnp.broadcast_to(log_dt.reshape(rows, 1), (rows, width))
    out = jax.ShapeDtypeStruct((rows, width), F32)
    ar, ai, bbr, bbi = pl.pallas_call(
        _discretise_kernel, out_shape=(out, out, out, out), name="s5_discretise",
    )(tile_p(lam_re), tile_p(lam_im), ldt, b_t(b_re), b_t(b_im))
    ar = ar[:, :SSM_STATE].reshape(DEPTH, 1, N_STATE)
    ai = ai[:, :SSM_STATE].reshape(DEPTH, 1, N_STATE)
    shape5 = (DEPTH, N_SSM_BLOCKS, GROUPS_PER_BLOCK, SSM_GROUP, SSM_STATE)
    bblk = jnp.concatenate([_group_block_diag(bbr.reshape(shape5)),
                            _group_block_diag(bbi.reshape(shape5))], axis=-1)
    return ar, ai, bblk


def _group_block_diag(m):
    depth, blocks, groups, n_r, n_c = m.shape
    same = jnp.eye(groups, dtype=bool)[None, None, :, None, :, None]
    out = jnp.where(same, m[:, :, :, :, None, :], 0.0).astype(BF16)
    return out.reshape(depth, blocks, groups * n_r, groups * n_c)


def _c_blocks(c):
    shape5 = (DEPTH, N_SSM_BLOCKS, GROUPS_PER_BLOCK, SSM_GROUP, SSM_STATE)
    return _group_block_diag(jnp.swapaxes(c.reshape(shape5), -1, -2))


def _ssm_input(u_block, bblk, bu_re_ref, bu_im_ref, sl):
    bu = _dot(u_block.astype(BF16), bblk)
    bu_re_ref[:, sl] = bu[:, :BLOCK_STATE]
    bu_im_ref[:, sl] = bu[:, BLOCK_STATE:]


def _ssm_readout(h_re_ref, h_im_ref, sl, cre, cimn):
    return _dot(h_re_ref[:, sl].astype(BF16), cre) + _dot(h_im_ref[:, sl].astype(BF16), cimn)


def _ssm_gate(y, u_ssm, d_ref, wglu_ref, bglu_ref):
    y = jax.nn.gelu(y + d_ref[...] * u_ssm)
    return y * jax.nn.sigmoid(_dot(y.astype(BF16), wglu_ref[...]) + bglu_ref[...])


def _pool_mix(pooled, poolw_ref, pscale_ref):
    outs = []
    for gi in range(len(POOL_WINDOWS)):
        sl = slice(gi * POOL_GROUP_WIDTH, (gi + 1) * POOL_GROUP_WIDTH)
        outs.append(_dot(pooled[gi].astype(BF16), poolw_ref[gi]) * pscale_ref[:, sl])
    return jnp.concatenate(outs, axis=-1)


def _mixer_a_kernel(xfirst_ref, xnext_ref, g_ref, win_ref, bblk_ref, cre_ref, cimn_ref,
                    ar_ref, ai_ref, d_ref, wglu_ref, bglu_ref, poolw_ref, pscale_ref,
                    wgate_lo_ref, wgate_hi_ref, wup_ref, wout_ref, wfin_ref, wfout_ref,
                    q_ref, ossm_ref, opool_ref, hre_ref, him_ref, hist_ref,
                    wgate_bf_ref, wup_bf_ref, wout_bf_ref, wfin_bf_ref, wfout_bf_ref,
                    bu_re_ref, bu_im_ref, pool_buf, ussm_tm, ossm_tm, opool_tm, win_bf,
                    proj_scr, hnext_scr):
    i = pl.program_id(0)
    hist_rows = POOL_HIST * BATCH

    wgate_bf_ref[:, :GATE_HALF] = wgate_lo_ref[...].astype(BF16)
    wgate_bf_ref[:, GATE_HALF:] = wgate_hi_ref[...].astype(BF16)
    wup_bf_ref[...] = wup_ref[...].astype(BF16)
    wout_bf_ref[...] = wout_ref[...].astype(BF16)
    wfin_bf_ref[...] = wfin_ref[...].astype(BF16)
    wfout_bf_ref[...] = wfout_ref[...].astype(BF16)

    def normed(x_blk_ref):
        return _rms(x_blk_ref[...].reshape(ROWS_A, D_MODEL), g_ref[...]).astype(BF16)

    @pl.when(i == 0)
    def _():
        hre_ref[...] = jnp.zeros_like(hre_ref)
        him_ref[...] = jnp.zeros_like(him_ref)
        pool_buf[:, 0:hist_rows, :] = jnp.zeros((N_SSM_BLOCKS, hist_rows, LANES), F32)
        win_bf[...] = win_ref[...].astype(BF16)
        proj_scr[...] = _dot(normed(xfirst_ref), win_bf[...])

    for b in range(BATCH):
        rows = slice(b * STEPS_A, (b + 1) * STEPS_A)
        q_ref[b] = proj_scr[rows, 2 * BRANCH:].astype(BF16)
        for k in range(N_SSM_BLOCKS):
            ussm_tm[k, pl.ds(b, STEPS_A, stride=BATCH), :] = (
                proj_scr[rows, k * LANES:(k + 1) * LANES])
            pool_buf[k, pl.ds(hist_rows + b, STEPS_A, stride=BATCH), :] = (
                proj_scr[rows, BRANCH + k * LANES:BRANCH + (k + 1) * LANES])
    u_ssm = jnp.concatenate([ussm_tm[k] for k in range(N_SSM_BLOCKS)], axis=-1)
    hnext_scr[...] = normed(xnext_ref)
    next_cols = [slice(c, c + MERGE_COLS) for c in range(0, 3 * BRANCH, MERGE_COLS)]

    def project_next_chunk():
        if next_cols:
            cols = next_cols.pop(0)
            proj_scr[:, cols] = _dot(hnext_scr[...], win_bf[:, cols])

    ys = []
    for k in range(N_SSM_BLOCKS):
        sl = slice(k * BLOCK_STATE, (k + 1) * BLOCK_STATE)
        _ssm_input(ussm_tm[k], bblk_ref[k], bu_re_ref, bu_im_ref, sl)
        project_next_chunk()
        a_r = jnp.broadcast_to(ar_ref[:, sl], (BATCH, BLOCK_STATE))
        a_i = jnp.broadcast_to(ai_ref[:, sl], (BATCH, BLOCK_STATE))
        hr = hre_ref[:, sl]
        hi = him_ref[:, sl]
        for t in range(STEPS_A):
            rows = slice(t * BATCH, (t + 1) * BATCH)
            hr, hi = (a_r * hr - a_i * hi + bu_re_ref[rows, sl],
                      a_r * hi + a_i * hr + bu_im_ref[rows, sl])
            bu_re_ref[rows, sl] = hr
            bu_im_ref[rows, sl] = hi
        hre_ref[:, sl] = hr
        him_ref[:, sl] = hi
        ys.append(_ssm_readout(bu_re_ref, bu_im_ref, sl, cre_ref[k], cimn_ref[k]))
    project_next_chunk()
    o_ssm = _ssm_gate(jnp.concatenate(ys, axis=-1), u_ssm, d_ref, wglu_ref, bglu_ref)

    project_next_chunk()
    t_pos = i * STEPS_A + lax.broadcasted_iota(jnp.int32, (ROWS_A, POOL_GROUP_WIDTH), 0) // BATCH
    pooled = []
    for gi, w in enumerate(POOL_WINDOWS):
        u_g = pool_buf[gi, hist_rows:hist_rows + ROWS_A, :]
        acc = u_g
        for j in range(1, w):
            start = hist_rows - j * BATCH
            acc = acc + pool_buf[gi, start:start + ROWS_A, :]
        cnt = jnp.minimum(t_pos + 1, w).astype(F32)
        pooled.append(acc / cnt - u_g)
    o_pool = _pool_mix(pooled, poolw_ref, pscale_ref)
    pool_buf[:, 0:hist_rows, :] = pool_buf[:, ROWS_A:ROWS_A + hist_rows, :]

    for k in range(N_SSM_BLOCKS):
        ossm_tm[k] = o_ssm[:, k * LANES:(k + 1) * LANES]
        opool_tm[k] = o_pool[:, k * LANES:(k + 1) * LANES]
    for b in range(BATCH):
        rows = pl.ds(b, STEPS_A, stride=BATCH)
        ossm_ref[b] = jnp.concatenate(
            [ossm_tm[k, rows, :] for k in range(N_SSM_BLOCKS)], axis=-1).astype(BF16)
        opool_ref[b] = jnp.concatenate(
            [opool_tm[k, rows, :] for k in range(N_SSM_BLOCKS)], axis=-1).astype(BF16)

    @pl.when(i == pl.num_programs(0) - 1)
    def _():
        for k in range(N_SSM_BLOCKS):
            hist_ref[:, k * LANES:(k + 1) * LANES] = pool_buf[k, 0:hist_rows, :]


def _mixer_a(layer, x, p, raw):
    hist_rows = POOL_HIST * BATCH
    n_steps = SEQ // STEPS_A
    seq_spec = lambda w: pl.BlockSpec((BATCH, STEPS_A, w), lambda i: (0, i, 0))
    cast_plan = {
        'w_gate': (D_MODEL // n_steps, N_BRANCH * D_MODEL, 1),
        'w_up': (N_BRANCH * BRANCH // n_steps, D_MODEL, 1),
        'w_out': (D_MODEL // n_steps, D_MODEL, 1),
        'w_ffn_in': (D_MODEL // n_steps, 2 * D_FF, 1),
        'w_ffn_out': (2 * D_FF // n_steps, D_MODEL, 2),
    }

    def chunk_in(name, width=None, col=0):
        rows, full_width, per = cast_plan[name]
        return pl.BlockSpec((None, rows, width or full_width),
                            lambda i: (layer, i // per, col))

    def chunk_out(name):
        rows, width, per = cast_plan[name]
        return pl.BlockSpec((rows, width), lambda i: (i // per, 0))

    cast_names = ('w_gate', 'w_up', 'w_out', 'w_ffn_in', 'w_ffn_out')
    cast_shapes = tuple(
        jax.ShapeDtypeStruct((cast_plan[n][0] * n_steps // cast_plan[n][2], cast_plan[n][1]), BF16)
        for n in cast_names)
    in_specs = [
        pl.BlockSpec((BATCH, STEPS_A, D_MODEL), lambda i: (0, 0, 0), pipeline_mode=pl.Buffered(1)),
        pl.BlockSpec((BATCH, STEPS_A, D_MODEL), lambda i: (0, jnp.minimum(i + 1, n_steps - 1), 0)),
        _layer_spec((1, D_MODEL), layer),
        _layer_spec((D_MODEL, 3 * BRANCH), layer),
        _layer_spec((N_SSM_BLOCKS, LANES, 2 * BLOCK_STATE), layer),
        _layer_spec((N_SSM_BLOCKS, BLOCK_STATE, LANES), layer),
        _layer_spec((N_SSM_BLOCKS, BLOCK_STATE, LANES), layer),
        _layer_spec((1, N_STATE), layer),
        _layer_spec((1, N_STATE), layer),
        _layer_spec((1, BRANCH), layer),
        _layer_spec((BRANCH, BRANCH), layer),
        _layer_spec((1, BRANCH), layer),
        _layer_spec((len(POOL_WINDOWS), POOL_GROUP_WIDTH, POOL_GROUP_WIDTH), layer),
        _layer_spec((1, BRANCH), layer),
        chunk_in('w_gate', GATE_HALF, 1), chunk_in('w_gate', GATE_HALF, 2),
        chunk_in('w_up'), chunk_in('w_out'), chunk_in('w_ffn_in'), chunk_in('w_ffn_out'),
    ]
    branch_out = jax.ShapeDtypeStruct((BATCH, SEQ, BRANCH), BF16)
    out_shape = (
        branch_out, branch_out, branch_out,
        jax.ShapeDtypeStruct((BATCH, N_STATE), F32),
        jax.ShapeDtypeStruct((BATCH, N_STATE), F32),
        jax.ShapeDtypeStruct((hist_rows, BRANCH), F32),
    ) + cast_shapes
    out_specs = (
        seq_spec(BRANCH), seq_spec(BRANCH), seq_spec(BRANCH),
        pl.BlockSpec((BATCH, N_STATE), lambda i: (0, 0)),
        pl.BlockSpec((BATCH, N_STATE), lambda i: (0, 0)),
        pl.BlockSpec((hist_rows, BRANCH), lambda i: (0, 0)),
    ) + tuple(chunk_out(n) for n in cast_names)
    outs = pl.pallas_call(
        _mixer_a_kernel,
        grid=(n_steps,),
        in_specs=in_specs, out_specs=out_specs, out_shape=out_shape,
        scratch_shapes=[
            pltpu.VMEM((ROWS_A, N_STATE), F32),
            pltpu.VMEM((ROWS_A, N_STATE), F32),
            pltpu.VMEM((N_SSM_BLOCKS, hist_rows + ROWS_A, LANES), F32),
            pltpu.VMEM((N_SSM_BLOCKS, ROWS_A, LANES), F32),
            pltpu.VMEM((N_SSM_BLOCKS, ROWS_A, LANES), F32),
            pltpu.VMEM((N_SSM_BLOCKS, ROWS_A, LANES), F32),
            pltpu.VMEM((D_MODEL, 3 * BRANCH), BF16),
            pltpu.VMEM((ROWS_A, 3 * BRANCH), F32),
            pltpu.VMEM((ROWS_A, D_MODEL), BF16),
        ],
        compiler_params=pltpu.CompilerParams(
            dimension_semantics=("arbitrary",), vmem_limit_bytes=VMEM_LIMIT),
        name="prompt_mixer_a",
    )(x, x, p['g_mix_pre'], raw['w_in'], p['bblk'], p['cre'], p['cimn'], p['ar'], p['ai'],
      p['ssm_d'], p['w_glu'], p['b_glu'], p['pool_w'], p['pool_scale'],
      raw['w_in'], raw['w_in'], raw['w_up'], raw['w_out'], raw['w_ffn_in'], raw['w_ffn_out'])
    return outs[:6], dict(zip(cast_names, outs[6:]))


def _mem_kv_kernel(mem_ref, g_ref, wkv_ref, k_ref, v_ref):
    wkv = wkv_ref[...].astype(BF16)
    for b in range(BATCH):
        kv = _dot(_rms(mem_ref[b], g_ref[...]).astype(BF16), wkv)
        for hd in range(MEM_HEADS):
            rows = pl.ds(hd, N_MEM, stride=MEM_HEADS)
            k_ref[b, rows, :] = kv[:, hd * MEM_HEAD_DIM:(hd + 1) * MEM_HEAD_DIM]
            v_ref[b, rows, :] = kv[:, BRANCH + hd * MEM_HEAD_DIM:BRANCH + (hd + 1) * MEM_HEAD_DIM]


def _mem_kv(mem, g_mem, w_kv):
    out = jax.ShapeDtypeStruct((DEPTH, BATCH, KV_ROWS, MEM_HEAD_DIM), F32)
    out_spec = pl.BlockSpec((None, BATCH, KV_ROWS, MEM_HEAD_DIM), lambda l: (l, 0, 0, 0))
    return pl.pallas_call(
        _mem_kv_kernel,
        grid=(DEPTH,),
        in_specs=[
            _const_spec((BATCH, N_MEM, D_MODEL)),
            pl.BlockSpec((None, 1, D_MODEL), lambda l: (l, 0, 0)),
            pl.BlockSpec((None, D_MODEL, 2 * BRANCH), lambda l: (l, 0, 0)),
        ],
        out_specs=(out_spec, out_spec), out_shape=(out, out),
        compiler_params=pltpu.CompilerParams(
            dimension_semantics=("arbitrary",), vmem_limit_bytes=VMEM_LIMIT),
        name="mem_kv",
    )(mem, g_mem, w_kv)


def _memory_attention_stages(q_ref, k_ref, v_ref, result):
    scale = MEM_HEAD_DIM ** -0.5
    head_rows = [pl.ds(hd, N_MEM, stride=MEM_HEADS) for hd in range(MEM_HEADS)]
    head_cols = [slice(hd * MEM_HEAD_DIM, (hd + 1) * MEM_HEAD_DIM) for hd in range(MEM_HEADS)]
    scores = [lax.dot_general(q_ref[:, head_cols[hd]], k_ref[head_rows[hd], :].astype(BF16),
                              (((1,), (1,)), ((), ())), preferred_element_type=F32) * scale
              for hd in range(MEM_HEADS)]
    yield
    probs = []
    for s in scores:
        e = jnp.exp(s - jnp.max(s, axis=-1, keepdims=True))
        probs.append((e / jnp.sum(e, axis=-1, keepdims=True)).astype(BF16))
        yield
    outs = []
    for hd in range(MEM_HEADS):
        outs.append(_dot(probs[hd], v_ref[head_rows[hd], :].astype(BF16)).astype(BF16))
        yield
    result.append(jnp.concatenate(outs, axis=-1))


def _merge_ffn_kernel(with_attention, x_ref, ossm_ref, opool_ref, *refs):
    o_mem = []
    if with_attention:
        attention = _memory_attention_stages(*refs[:3], o_mem)
        refs = refs[3:]
    else:
        attention = iter(())
        o_mem.append(refs[0][...])
        refs = refs[1:]
    (gpre_ref, wgate_ref, wup_ref, wout_ref, gpost_ref, gfpre_ref, wfin_ref, wfout_ref,
     gfpost_ref, y_ref) = refs
    x = x_ref[...]
    h = _rms(x, gpre_ref[...]).astype(BF16)

    def branch_term(b, o_b, cols):
        gate_cols = slice(b * D_MODEL + cols.start, b * D_MODEL + cols.stop)
        gate = jax.nn.sigmoid(_dot(h, wgate_ref[:, gate_cols]))
        return gate * _dot(o_b, wup_ref[b * BRANCH:(b + 1) * BRANCH, cols])

    col_chunks = [slice(c, c + MERGE_COLS) for c in range(0, D_MODEL, MERGE_COLS)]
    next(attention, None)
    partial = []
    for b, o_b in enumerate((ossm_ref[...], opool_ref[...])):
        for ci, cols in enumerate(col_chunks):
            term = branch_term(b, o_b, cols)
            if b == 0:
                partial.append(term)
            else:
                partial[ci] = partial[ci] + term
            next(attention, None)
    for _ in attention:
        pass
    merged = jnp.concatenate(
        [partial[ci] + branch_term(N_BRANCH - 1, o_mem[0], cols)
         for ci, cols in enumerate(col_chunks)], axis=-1).astype(BF16)

    def tail_stages(rows):
        m2 = _dot(merged[rows, :], wout_ref[...])
        yield
        x1 = x[rows, :] + _rms(m2, gpost_ref[...])
        hf = _rms(x1, gfpre_ref[...]).astype(BF16)
        yield
        f = None
        for lo, hi in FF_CHUNKS:
            hg = _dot(hf, wfin_ref[:, lo:hi])
            hu = _dot(hf, wfin_ref[:, D_FF + lo:D_FF + hi])
            yield
            act = (jax.nn.silu(hg) * hu).astype(BF16)
            yield
            part = _dot(act, wfout_ref[lo:hi, :])
            f = part if f is None else f + part
            yield
        y_ref[rows, :] = x1 + _rms(f, gfpost_ref[...])

    n_groups = TAIL_GROUPS if x.shape[0] >= TAIL_GROUPS * PIECE_ROWS else 1
    group_rows = x.shape[0] // n_groups
    waiting = [tail_stages(slice(g * group_rows, (g + 1) * group_rows)) for g in range(n_groups)]
    running = []
    while waiting or running:
        if waiting:
            running.append(waiting.pop(0))
        for chain in list(running):
            if next(chain, StopIteration) is StopIteration:
                running.remove(chain)


def _merge_ffn(layer, x, o_ssm, o_pool, third, p, w):
    n_rows = x.shape[0]
    rows = min(ROWS_C, n_rows)
    row_spec = lambda w: pl.BlockSpec((rows, w), lambda i: (i, 0))
    with_attention = isinstance(third, tuple)
    if with_attention:
        tiles_per_seq = SEQ // rows
        kv_spec = pl.BlockSpec((None, None, KV_ROWS, MEM_HEAD_DIM),
                               lambda i: (layer, i // tiles_per_seq, 0, 0))
        third_specs = [row_spec(BRANCH), kv_spec, kv_spec]
    else:
        third = (third,)
        third_specs = [row_spec(BRANCH)]
    in_specs = [row_spec(D_MODEL), row_spec(BRANCH), row_spec(BRANCH)] + third_specs + [
        _layer_spec((1, D_MODEL), layer),
        _const_spec((D_MODEL, N_BRANCH * D_MODEL)),
        _const_spec((N_BRANCH * BRANCH, D_MODEL)),
        _const_spec((D_MODEL, D_MODEL)),
        _layer_spec((1, D_MODEL), layer),
        _layer_spec((1, D_MODEL), layer),
        _const_spec((D_MODEL, 2 * D_FF)),
        _const_spec((D_FF, D_MODEL)),
        _layer_spec((1, D_MODEL), layer),
    ]
    return pl.pallas_call(
        functools.partial(_merge_ffn_kernel, with_attention),
        grid=(n_rows // rows,),
        in_specs=in_specs, out_specs=row_spec(D_MODEL),
        out_shape=jax.ShapeDtypeStruct((n_rows, D_MODEL), F32),
        compiler_params=pltpu.CompilerParams(
            dimension_semantics=("parallel",), vmem_limit_bytes=VMEM_LIMIT),
        name="merge_ffn",
    )(x, o_ssm, o_pool, *third, p['g_mix_pre'], w['w_gate'], w['w_up'], w['w_out'],
      p['g_mix_post'], p['g_ffn_pre'], w['w_ffn_in'], w['w_ffn_out'], p['g_ffn_post'])


def _sample_mixer_kernel(x_ref, g_ref, win_ref, bblk_ref, cre_ref, cimn_ref, ar_ref, ai_ref,
                         d_ref, wglu_ref, bglu_ref, poolw_ref, pscale_ref,
                         h0re_ref, h0im_ref, hist_ref,
                         q_ref, ossm_ref, opool_ref, hre_ref, him_ref, nhist_ref,
                         bu_re_ref, bu_im_ref):
    h = _rms(x_ref[...], g_ref[...]).astype(BF16)
    proj = _dot(h, win_ref[...].astype(BF16))
    u_ssm = proj[:, :BRANCH]
    u_pool = proj[:, BRANCH:2 * BRANCH]
    q_ref[...] = proj[:, 2 * BRANCH:]

    ys = []
    for k in range(N_SSM_BLOCKS):
        sl = slice(k * BLOCK_STATE, (k + 1) * BLOCK_STATE)
        _ssm_input(u_ssm[:, k * LANES:(k + 1) * LANES], bblk_ref[k], bu_re_ref, bu_im_ref, sl)
        a_r = ar_ref[:, sl]
        a_i = ai_ref[:, sl]
        h0r = h0re_ref[:, sl]
        h0i = h0im_ref[:, sl]
        hre_ref[:, sl] = bu_re_ref[:, sl] + a_r * h0r - a_i * h0i
        him_ref[:, sl] = bu_im_ref[:, sl] + a_r * h0i + a_i * h0r
        ys.append(_ssm_readout(hre_ref, him_ref, sl, cre_ref[k], cimn_ref[k]))
    y = _ssm_gate(jnp.concatenate(ys, axis=-1), u_ssm, d_ref, wglu_ref, bglu_ref)
    ossm_ref[...] = y.astype(BF16)

    pooled = []
    for gi, w in enumerate(POOL_WINDOWS):
        sl = slice(gi * POOL_GROUP_WIDTH, (gi + 1) * POOL_GROUP_WIDTH)
        acc = u_pool[:, sl]
        for j in range(1, w):
            acc = acc + hist_ref[POOL_HIST - j, :, sl]
        cnt = float(min(PAST_LEN + 1, w))
        pooled.append(acc / cnt - u_pool[:, sl])
    opool_ref[...] = _pool_mix(pooled, poolw_ref, pscale_ref).astype(BF16)
    for j in range(POOL_HIST - 1):
        nhist_ref[j] = hist_ref[j + 1]
    nhist_ref[POOL_HIST - 1] = u_pool


def _sample_mixer(layer, x_s, p, w_in, h0_re, h0_im, hist_t):
    n = x_s.shape[0]
    full = lambda shape: pl.BlockSpec(shape, lambda i: (0,) * len(shape))
    in_specs = [
        full((n, D_MODEL)),
        _layer_spec((1, D_MODEL), layer),
        _layer_spec((D_MODEL, 3 * BRANCH), layer),
        _layer_spec((N_SSM_BLOCKS, LANES, 2 * BLOCK_STATE), layer),
        _layer_spec((N_SSM_BLOCKS, BLOCK_STATE, LANES), layer),
        _layer_spec((N_SSM_BLOCKS, BLOCK_STATE, LANES), layer),
        _layer_spec((1, N_STATE), layer),
        _layer_spec((1, N_STATE), layer),
        _layer_spec((1, BRANCH), layer),
        _layer_spec((BRANCH, BRANCH), layer),
        _layer_spec((1, BRANCH), layer),
        _layer_spec((len(POOL_WINDOWS), POOL_GROUP_WIDTH, POOL_GROUP_WIDTH), layer),
        _layer_spec((1, BRANCH), layer),
        _layer_spec((n, N_STATE), layer),
        _layer_spec((n, N_STATE), layer),
        _layer_spec((POOL_HIST, n, BRANCH), layer),
    ]
    out_shape = (
        jax.ShapeDtypeStruct((n, BRANCH), F32),
        jax.ShapeDtypeStruct((n, BRANCH), BF16),
        jax.ShapeDtypeStruct((n, BRANCH), BF16),
        jax.ShapeDtypeStruct((n, N_STATE), F32),
        jax.ShapeDtypeStruct((n, N_STATE), F32),
        jax.ShapeDtypeStruct((POOL_HIST, n, BRANCH), F32),
    )
    out_specs = tuple(full(s.shape) for s in out_shape)
    return pl.pallas_call(
        _sample_mixer_kernel,
        grid=(1,),
        in_specs=in_specs, out_specs=out_specs, out_shape=out_shape,
        scratch_shapes=[pltpu.VMEM((n, N_STATE), F32), pltpu.VMEM((n, N_STATE), F32)],
        compiler_params=pltpu.CompilerParams(
            dimension_semantics=("arbitrary",), vmem_limit_bytes=VMEM_LIMIT),
        name="sample_mixer",
    )(x_s, p['g_mix_pre'], w_in, p['bblk'], p['cre'], p['cimn'], p['ar'], p['ai'],
      p['ssm_d'], p['w_glu'], p['b_glu'], p['pool_w'], p['pool_scale'], h0_re, h0_im, hist_t)


def _sample_attention_kernel(q_ref, k_ref, v_ref, ones_ref, o_ref):
    tiles = KV_ROWS // SUBLANES
    scale = MEM_HEAD_DIM ** -0.5

    def one_sample(b, carry):
        q8 = q_ref[b] * scale
        prod = k_ref[b].reshape(tiles, SUBLANES, MEM_HEAD_DIM) * q8[None]
        prod = prod.reshape(KV_ROWS, MEM_HEAD_DIM).astype(BF16)
        s = _dot(prod, ones_ref[...]).reshape(tiles, SUBLANES, MEM_HEAD_DIM)
        m8 = jnp.max(s, axis=0)
        m8 = jnp.maximum(m8, pltpu.roll(m8, MEM_HEADS, 0))
        e = jnp.exp(s - m8[None])
        l8 = jnp.sum(e, axis=0)
        acc = jnp.sum(e * v_ref[b].reshape(tiles, SUBLANES, MEM_HEAD_DIM), axis=0)
        l8 = l8 + pltpu.roll(l8, MEM_HEADS, 0)
        acc = acc + pltpu.roll(acc, MEM_HEADS, 0)
        o_ref[b] = acc / l8
        return carry

    lax.fori_loop(0, SAMPLE_BLOCK, one_sample, 0, unroll=True)


def _sample_attention(layer, q8, k_cache, v_cache, lane_ones):
    n = q8.shape[0]
    kv_spec = pl.BlockSpec((None, SAMPLE_BLOCK, KV_ROWS, MEM_HEAD_DIM), lambda i: (layer, i, 0, 0))
    q_spec = pl.BlockSpec((SAMPLE_BLOCK, SUBLANES, MEM_HEAD_DIM), lambda i: (i, 0, 0))
    return pl.pallas_call(
        _sample_attention_kernel,
        grid=(n // SAMPLE_BLOCK,),
        in_specs=[q_spec, kv_spec, kv_spec, _const_spec((MEM_HEAD_DIM, MEM_HEAD_DIM))],
        out_specs=q_spec,
        out_shape=jax.ShapeDtypeStruct((n, SUBLANES, MEM_HEAD_DIM), F32),
        compiler_params=pltpu.CompilerParams(
            dimension_semantics=("parallel",), vmem_limit_bytes=VMEM_LIMIT),
        name="sample_attention",
    )(q8, k_cache, v_cache, lane_ones)


def kernel(x_prompt, x_sample, mem_prompt, cache_mem_k, cache_mem_v, state_ssm_re, state_ssm_im, state_pool, g_mix_pre, g_mix_post, g_ffn_pre, g_ffn_post, g_mem, w_in, w_kv, ssm_lam_re, ssm_lam_im, ssm_log_dt, ssm_b_re, ssm_b_im, ssm_c_re, ssm_c_im, ssm_d, ssm_w_glu, ssm_b_glu, pool_w, pool_scale, w_branch_up, w_out, w_ffn_in, w_ffn_out):
    ar, ai, bblk = _discretise(ssm_lam_re, ssm_lam_im, ssm_log_dt, ssm_b_re, ssm_b_im)
    vec = lambda a: a.reshape(DEPTH, 1, a.shape[-1])
    p = {
        'g_mix_pre': vec(g_mix_pre), 'g_mix_post': vec(g_mix_post),
        'g_ffn_pre': vec(g_ffn_pre), 'g_ffn_post': vec(g_ffn_post),
        'bblk': bblk, 'ar': ar, 'ai': ai,
        'cre': _c_blocks(ssm_c_re), 'cimn': _c_blocks(-ssm_c_im),
        'ssm_d': vec(ssm_d), 'w_glu': ssm_w_glu.astype(BF16), 'b_glu': vec(ssm_b_glu),
        'pool_w': pool_w.astype(BF16), 'pool_scale': vec(pool_scale),
    }
    raw = {
        'w_in': w_in, 'w_up': w_branch_up.reshape(DEPTH, N_BRANCH * BRANCH, D_MODEL),
        'w_out': w_out, 'w_ffn_in': w_ffn_in, 'w_ffn_out': w_ffn_out,
    }
    k_mem, v_mem = _mem_kv(mem_prompt, vec(g_mem), w_kv)

    lane_ones = jnp.ones((MEM_HEAD_DIM, MEM_HEAD_DIM), BF16)
    k_cache = cache_mem_k.reshape(DEPTH, DEC_BATCH, KV_ROWS, MEM_HEAD_DIM)
    v_cache = cache_mem_v.reshape(DEPTH, DEC_BATCH, KV_ROWS, MEM_HEAD_DIM)
    h0_re = state_ssm_re.reshape(DEPTH, DEC_BATCH, N_STATE)
    h0_im = state_ssm_im.reshape(DEPTH, DEC_BATCH, N_STATE)
    hist_t = jnp.swapaxes(state_pool, 1, 2)

    xp = x_prompt
    xs = x_sample.reshape(DEC_BATCH, D_MODEL)
    n_prompt = BATCH * SEQ
    re_p, im_p, pool_p, re_s, im_s, pool_s = [], [], [], [], [], []
    for layer in range(DEPTH):
        (q, o_ssm, o_pool, hre, him, hist), w_bf = _mixer_a(layer, xp, p, raw)
        xp = _merge_ffn(layer, xp.reshape(n_prompt, D_MODEL), o_ssm.reshape(n_prompt, BRANCH),
                        o_pool.reshape(n_prompt, BRANCH),
                        (q.reshape(n_prompt, BRANCH), k_mem, v_mem), p, w_bf)
        xp = xp.reshape(BATCH, SEQ, D_MODEL)
        re_p.append(hre.reshape(BATCH, SSM_GROUPS, SSM_STATE))
        im_p.append(him.reshape(BATCH, SSM_GROUPS, SSM_STATE))
        pool_p.append(jnp.swapaxes(hist.reshape(POOL_HIST, BATCH, BRANCH), 0, 1))

        qs, os_ssm, os_pool, hre, him, nhist = _sample_mixer(layer, xs, p, w_in, h0_re, h0_im,
                                                             hist_t)
        qs4 = qs.reshape(DEC_BATCH, MEM_HEADS, MEM_HEAD_DIM)
        os_mem = _sample_attention(layer, jnp.concatenate([qs4, qs4], axis=1),
                                   k_cache, v_cache, lane_ones)
        os_mem = os_mem[:, :MEM_HEADS].reshape(DEC_BATCH, BRANCH).astype(BF16)
        xs = _merge_ffn(layer, xs, os_ssm, os_pool, os_mem, p, w_bf)
        re_s.append(hre.reshape(DEC_BATCH, SSM_GROUPS, SSM_STATE))
        im_s.append(him.reshape(DEC_BATCH, SSM_GROUPS, SSM_STATE))
        pool_s.append(jnp.swapaxes(nhist, 0, 1))

    y_prompt = xp
    y_sample = xs.reshape(DEC_BATCH, 1, D_MODEL)
    kv_shape = (DEPTH, BATCH, N_MEM, MEM_HEADS, MEM_HEAD_DIM)
    return (y_prompt, y_sample,
            jnp.stack(re_p), jnp.stack(im_p), jnp.stack(pool_p),
            k_mem.reshape(kv_shape), v_mem.reshape(kv_shape),
            jnp.stack(re_s), jnp.stack(im_s), jnp.stack(pool_s))
```

```python
import functools

import jax
import jax.numpy as jnp
from jax import lax
from jax.experimental import pallas as pl
from jax.experimental.pallas import tpu as pltpu

D_MODEL = 1024
BATCH = 8
SEQ = 2048
DEPTH = 4
DEC_BATCH = 128
PAST_LEN = 16384

BRANCH = D_MODEL // 2
SSM_GROUP = 16
SSM_GROUPS = BRANCH // SSM_GROUP
SSM_STATE = 64
N_STATE = SSM_GROUPS * SSM_STATE
POOL_WINDOWS = (2, 4, 8, 16)
POOL_GROUP_WIDTH = BRANCH // len(POOL_WINDOWS)
POOL_HIST = max(POOL_WINDOWS) - 1
N_MEM = 256
MEM_HEADS = 4
MEM_HEAD_DIM = BRANCH // MEM_HEADS
KV_ROWS = N_MEM * MEM_HEADS
N_BRANCH = 3
GATE_HALF = N_BRANCH * D_MODEL // 2
D_FF = 2816
RMS_EPS = 1e-6

LANES = 128
SUBLANES = 8
GROUPS_PER_BLOCK = LANES // SSM_GROUP
N_SSM_BLOCKS = BRANCH // LANES
BLOCK_STATE = GROUPS_PER_BLOCK * SSM_STATE

ROWS_A = 512
STEPS_A = ROWS_A // BATCH
ROWS_C = 512
MERGE_COLS = 256
TAIL_GROUPS = 2
PIECE_ROWS = 128
SAMPLES_PER_STEP = DEC_BATCH // (SEQ // STEPS_A)
FF_CHUNKS = ((0, 1536), (1536, 2816))
VMEM_LIMIT = 56 * 1024 * 1024

BF16 = jnp.bfloat16
F32 = jnp.float32


def _rms(x, g):
    ms = jnp.mean(x * x, axis=-1, keepdims=True)
    return x * lax.rsqrt(ms + RMS_EPS) * g


def _dot(a, b):
    return jnp.dot(a, b, preferred_element_type=F32)


def _const_spec(shape):
    nd = len(shape)
    return pl.BlockSpec(shape, lambda *_: (0,) * nd, pipeline_mode=pl.Buffered(1))


def _layer_spec(shape, layer):
    nd = len(shape)
    return pl.BlockSpec((None,) + tuple(shape), lambda *_: (layer,) + (0,) * nd,
                        pipeline_mode=pl.Buffered(1))


def _discretise_kernel(lr_ref, li_ref, ldt_ref, br_ref, bi_ref,
                       ar_ref, ai_ref, bbr_ref, bbi_ref):
    lr = lr_ref[...]
    li = li_ref[...]
    dt = jnp.exp(ldt_ref[...])
    zr = lr * dt
    zi = li * dt
    mag = jnp.exp(zr)
    ar = mag * jnp.cos(zi)
    ai = mag * jnp.sin(zi)
    den = lr * lr + li * li
    fr = ((ar - 1.0) * lr + ai * li) / den
    fi = (ai * lr - (ar - 1.0) * li) / den
    br = br_ref[...]
    bi = bi_ref[...]
    ar_ref[...] = ar
    ai_ref[...] = ai
    bbr_ref[...] = fr * br - fi * bi
    bbi_ref[...] = fr * bi + fi * br


def _discretise(lam_re, lam_im, log_dt, b_re, b_im):
    rows = DEPTH * SSM_GROUPS
    width = SSM_GROUP * SSM_STATE

    def tile_p(a):
        return jnp.tile(a.reshape(rows, 1, SSM_STATE), (1, SSM_GROUP, 1)).reshape(rows, width)

    def b_t(a):
        return jnp.swapaxes(a, -1, -2).reshape(rows, width)

    ldt = jnp.broadcast_to(log_dt.reshape(rows, 1), (rows, width))
    out = jax.ShapeDtypeStruct((rows, width), F32)
    ar, ai, bbr, bbi = pl.pallas_call(
        _discretise_kernel, out_shape=(out, out, out, out), name="s5_discretise",
    )(tile_p(lam_re), tile_p(lam_im), ldt, b_t(b_re), b_t(b_im))
    ar = ar[:, :SSM_STATE].reshape(DEPTH, 1, N_STATE)
    ai = ai[:, :SSM_STATE].reshape(DEPTH, 1, N_STATE)
    shape5 = (DEPTH, N_SSM_BLOCKS, GROUPS_PER_BLOCK, SSM_GROUP, SSM_STATE)
    bblk = jnp.concatenate([_group_block_diag(bbr.reshape(shape5)),
                            _group_block_diag(bbi.reshape(shape5))], axis=-1)
    return ar, ai, bblk


def _group_block_diag(m):
    depth, blocks, groups, n_r, n_c = m.shape
    same = jnp.eye(groups, dtype=bool)[None, None, :, None, :, None]
    out = jnp.where(same, m[:, :, :, :, None, :], 0.0).astype(BF16)
    return out.reshape(depth, blocks, groups * n_r, groups * n_c)


def _c_blocks(c):
    shape5 = (DEPTH, N_SSM_BLOCKS, GROUPS_PER_BLOCK, SSM_GROUP, SSM_STATE)
    return _group_block_diag(jnp.swapaxes(c.reshape(shape5), -1, -2))


def _ssm_input(u_block, bblk, bu_re_ref, bu_im_ref, sl):
    bu = _dot(u_block.astype(BF16), bblk)
    bu_re_ref[:, sl] = bu[:, :BLOCK_STATE]
    bu_im_ref[:, sl] = bu[:, BLOCK_STATE:]


def _ssm_readout(h_re_ref, h_im_ref, sl, cre, cimn):
    return _dot(h_re_ref[:, sl].astype(BF16), cre) + _dot(h_im_ref[:, sl].astype(BF16), cimn)


def _ssm_gate(y, u_ssm, d_ref, wglu_ref, bglu_ref):
    y = jax.nn.gelu(y + d_ref[...] * u_ssm)
    return y * jax.nn.sigmoid(_dot(y.astype(BF16), wglu_ref[...]) + bglu_ref[...])


def _pool_mix(pooled, poolw_ref, pscale_ref):
    outs = []
    for gi in range(len(POOL_WINDOWS)):
        sl = slice(gi * POOL_GROUP_WIDTH, (gi + 1) * POOL_GROUP_WIDTH)
        outs.append(_dot(pooled[gi].astype(BF16), poolw_ref[gi]) * pscale_ref[:, sl])
    return jnp.concatenate(outs, axis=-1)


def _sample_attend(q8, k, v, ones_ref):
    tiles = KV_ROWS // SUBLANES
    q8 = q8 * MEM_HEAD_DIM ** -0.5
    prod = k.reshape(tiles, SUBLANES, MEM_HEAD_DIM) * q8[None]
    prod = prod.reshape(KV_ROWS, MEM_HEAD_DIM).astype(BF16)
    s = _dot(prod, ones_ref[...]).reshape(tiles, SUBLANES, MEM_HEAD_DIM)
    m8 = jnp.max(s, axis=0)
    m8 = jnp.maximum(m8, pltpu.roll(m8, MEM_HEADS, 0))
    e = jnp.exp(s - m8[None])
    l8 = jnp.sum(e, axis=0)
    acc = jnp.sum(e * v.reshape(tiles, SUBLANES, MEM_HEAD_DIM), axis=0)
    l8 = l8 + pltpu.roll(l8, MEM_HEADS, 0)
    acc = acc + pltpu.roll(acc, MEM_HEADS, 0)
    return acc / l8


def _mixer_a_kernel(xfirst_ref, xnext_ref, g_ref, win_ref, bblk_ref, cre_ref, cimn_ref,
                    ar_ref, ai_ref, d_ref, wglu_ref, bglu_ref, poolw_ref, pscale_ref,
                    wgate_lo_ref, wgate_hi_ref, wup_ref, wout_ref, wfin_ref, wfout_ref,
                    sq_ref, sk_ref, sv_ref, ones_ref,
                    q_ref, ossm_ref, opool_ref, hre_ref, him_ref, hist_ref,
                    wgate_bf_ref, wup_bf_ref, wout_bf_ref, wfin_bf_ref, wfout_bf_ref, so_ref,
                    bu_re_ref, bu_im_ref, pool_buf, ussm_tm, ossm_tm, opool_tm, win_bf,
                    proj_scr, hnext_scr):
    i = pl.program_id(0)
    hist_rows = POOL_HIST * BATCH

    wgate_bf_ref[:, :GATE_HALF] = wgate_lo_ref[...].astype(BF16)
    wgate_bf_ref[:, GATE_HALF:] = wgate_hi_ref[...].astype(BF16)
    wup_bf_ref[...] = wup_ref[...].astype(BF16)
    wout_bf_ref[...] = wout_ref[...].astype(BF16)
    wfin_bf_ref[...] = wfin_ref[...].astype(BF16)
    wfout_bf_ref[...] = wfout_ref[...].astype(BF16)

    def normed(x_blk_ref):
        return _rms(x_blk_ref[...].reshape(ROWS_A, D_MODEL), g_ref[...]).astype(BF16)

    @pl.when(i == 0)
    def _():
        hre_ref[...] = jnp.zeros_like(hre_ref)
        him_ref[...] = jnp.zeros_like(him_ref)
        pool_buf[:, 0:hist_rows, :] = jnp.zeros((N_SSM_BLOCKS, hist_rows, LANES), F32)
        win_bf[...] = win_ref[...].astype(BF16)
        proj_scr[...] = _dot(normed(xfirst_ref), win_bf[...])

    for b in range(BATCH):
        rows = slice(b * STEPS_A, (b + 1) * STEPS_A)
        q_ref[b] = proj_scr[rows, 2 * BRANCH:].astype(BF16)
        for k in range(N_SSM_BLOCKS):
            ussm_tm[k, pl.ds(b, STEPS_A, stride=BATCH), :] = (
                proj_scr[rows, k * LANES:(k + 1) * LANES])
            pool_buf[k, pl.ds(hist_rows + b, STEPS_A, stride=BATCH), :] = (
                proj_scr[rows, BRANCH + k * LANES:BRANCH + (k + 1) * LANES])
    u_ssm = jnp.concatenate([ussm_tm[k] for k in range(N_SSM_BLOCKS)], axis=-1)
    hnext_scr[...] = normed(xnext_ref)
    next_cols = [slice(c, c + MERGE_COLS) for c in range(0, 3 * BRANCH, MERGE_COLS)]

    def project_next_chunk():
        if next_cols:
            cols = next_cols.pop(0)
            proj_scr[:, cols] = _dot(hnext_scr[...], win_bf[:, cols])

    ys = []
    for k in range(N_SSM_BLOCKS):
        sl = slice(k * BLOCK_STATE, (k + 1) * BLOCK_STATE)
        _ssm_input(ussm_tm[k], bblk_ref[k], bu_re_ref, bu_im_ref, sl)
        project_next_chunk()
        a_r = jnp.broadcast_to(ar_ref[:, sl], (BATCH, BLOCK_STATE))
        a_i = jnp.broadcast_to(ai_ref[:, sl], (BATCH, BLOCK_STATE))
        hr = hre_ref[:, sl]
        hi = him_ref[:, sl]
        for t in range(STEPS_A):
            rows = slice(t * BATCH, (t + 1) * BATCH)
            hr, hi = (a_r * hr - a_i * hi + bu_re_ref[rows, sl],
                      a_r * hi + a_i * hr + bu_im_ref[rows, sl])
            bu_re_ref[rows, sl] = hr
            bu_im_ref[rows, sl] = hi
        hre_ref[:, sl] = hr
        him_ref[:, sl] = hi
        ys.append(_ssm_readout(bu_re_ref, bu_im_ref, sl, cre_ref[k], cimn_ref[k]))
        for smp in range(k, SAMPLES_PER_STEP, N_SSM_BLOCKS):
            so_ref[smp] = _sample_attend(sq_ref[smp], sk_ref[smp], sv_ref[smp], ones_ref)
    project_next_chunk()
    o_ssm = _ssm_gate(jnp.concatenate(ys, axis=-1), u_ssm, d_ref, wglu_ref, bglu_ref)

    project_next_chunk()
    t_pos = i * STEPS_A + lax.broadcasted_iota(jnp.int32, (ROWS_A, POOL_GROUP_WIDTH), 0) // BATCH
    pooled = []
    for gi, w in enumerate(POOL_WINDOWS):
        u_g = pool_buf[gi, hist_rows:hist_rows + ROWS_A, :]
        acc = u_g
        for j in range(1, w):
            start = hist_rows - j * BATCH
            acc = acc + pool_buf[gi, start:start + ROWS_A, :]
        cnt = jnp.minimum(t_pos + 1, w).astype(F32)
        pooled.append(acc / cnt - u_g)
    o_pool = _pool_mix(pooled, poolw_ref, pscale_ref)
    pool_buf[:, 0:hist_rows, :] = pool_buf[:, ROWS_A:ROWS_A + hist_rows, :]

    for k in range(N_SSM_BLOCKS):
        ossm_tm[k] = o_ssm[:, k * LANES:(k + 1) * LANES]
        opool_tm[k] = o_pool[:, k * LANES:(k + 1) * LANES]
    for b in range(BATCH):
        rows = pl.ds(b, STEPS_A, stride=BATCH)
        ossm_ref[b] = jnp.concatenate(
            [ossm_tm[k, rows, :] for k in range(N_SSM_BLOCKS)], axis=-1).astype(BF16)
        opool_ref[b] = jnp.concatenate(
            [opool_tm[k, rows, :] for k in range(N_SSM_BLOCKS)], axis=-1).astype(BF16)

    @pl.when(i == pl.num_programs(0) - 1)
    def _():
        for k in range(N_SSM_BLOCKS):
            hist_ref[:, k * LANES:(k + 1) * LANES] = pool_buf[k, 0:hist_rows, :]


def _mixer_a(layer, x, p, raw, sample):
    hist_rows = POOL_HIST * BATCH
    n_steps = SEQ // STEPS_A
    seq_spec = lambda w: pl.BlockSpec((BATCH, STEPS_A, w), lambda i: (0, i, 0))
    cast_plan = {
        'w_gate': (D_MODEL // n_steps, N_BRANCH * D_MODEL, 1),
        'w_up': (N_BRANCH * BRANCH // n_steps, D_MODEL, 1),
        'w_out': (D_MODEL // n_steps, D_MODEL, 1),
        'w_ffn_in': (D_MODEL // n_steps, 2 * D_FF, 1),
        'w_ffn_out': (2 * D_FF // n_steps, D_MODEL, 2),
    }

    def chunk_in(name, width=None, col=0):
        rows, full_width, per = cast_plan[name]
        return pl.BlockSpec((None, rows, width or full_width),
                            lambda i: (layer, i // per, col))

    def chunk_out(name):
        rows, width, per = cast_plan[name]
        return pl.BlockSpec((rows, width), lambda i: (i // per, 0))

    sample_spec = pl.BlockSpec((SAMPLES_PER_STEP, SUBLANES, MEM_HEAD_DIM), lambda i: (i, 0, 0))
    cache_spec = pl.BlockSpec((None, SAMPLES_PER_STEP, KV_ROWS, MEM_HEAD_DIM),
                              lambda i: (layer, i, 0, 0))
    cast_names = ('w_gate', 'w_up', 'w_out', 'w_ffn_in', 'w_ffn_out')
    cast_shapes = tuple(
        jax.ShapeDtypeStruct((cast_plan[n][0] * n_steps // cast_plan[n][2], cast_plan[n][1]), BF16)
        for n in cast_names)
    in_specs = [
        pl.BlockSpec((BATCH, STEPS_A, D_MODEL), lambda i: (0, 0, 0), pipeline_mode=pl.Buffered(1)),
        pl.BlockSpec((BATCH, STEPS_A, D_MODEL), lambda i: (0, jnp.minimum(i + 1, n_steps - 1), 0)),
        _layer_spec((1, D_MODEL), layer),
        _layer_spec((D_MODEL, 3 * BRANCH), layer),
        _layer_spec((N_SSM_BLOCKS, LANES, 2 * BLOCK_STATE), layer),
        _layer_spec((N_SSM_BLOCKS, BLOCK_STATE, LANES), layer),
        _layer_spec((N_SSM_BLOCKS, BLOCK_STATE, LANES), layer),
        _layer_spec((1, N_STATE), layer),
        _layer_spec((1, N_STATE), layer),
        _layer_spec((1, BRANCH), layer),
        _layer_spec((BRANCH, BRANCH), layer),
        _layer_spec((1, BRANCH), layer),
        _layer_spec((len(POOL_WINDOWS), POOL_GROUP_WIDTH, POOL_GROUP_WIDTH), layer),
        _layer_spec((1, BRANCH), layer),
        chunk_in('w_gate', GATE_HALF, 1), chunk_in('w_gate', GATE_HALF, 2),
        chunk_in('w_up'), chunk_in('w_out'), chunk_in('w_ffn_in'), chunk_in('w_ffn_out'),
        sample_spec, cache_spec, cache_spec, _const_spec((MEM_HEAD_DIM, MEM_HEAD_DIM)),
    ]
    branch_out = jax.ShapeDtypeStruct((BATCH, SEQ, BRANCH), BF16)
    out_shape = (
        branch_out, branch_out, branch_out,
        jax.ShapeDtypeStruct((BATCH, N_STATE), F32),
        jax.ShapeDtypeStruct((BATCH, N_STATE), F32),
        jax.ShapeDtypeStruct((hist_rows, BRANCH), F32),
    ) + cast_shapes + (jax.ShapeDtypeStruct((DEC_BATCH, SUBLANES, MEM_HEAD_DIM), F32),)
    out_specs = (
        seq_spec(BRANCH), seq_spec(BRANCH), seq_spec(BRANCH),
        pl.BlockSpec((BATCH, N_STATE), lambda i: (0, 0)),
        pl.BlockSpec((BATCH, N_STATE), lambda i: (0, 0)),
        pl.BlockSpec((hist_rows, BRANCH), lambda i: (0, 0)),
    ) + tuple(chunk_out(n) for n in cast_names) + (sample_spec,)
    outs = pl.pallas_call(
        _mixer_a_kernel,
        grid=(n_steps,),
        in_specs=in_specs, out_specs=out_specs, out_shape=out_shape,
        scratch_shapes=[
            pltpu.VMEM((ROWS_A, N_STATE), F32),
            pltpu.VMEM((ROWS_A, N_STATE), F32),
            pltpu.VMEM((N_SSM_BLOCKS, hist_rows + ROWS_A, LANES), F32),
            pltpu.VMEM((N_SSM_BLOCKS, ROWS_A, LANES), F32),
            pltpu.VMEM((N_SSM_BLOCKS, ROWS_A, LANES), F32),
            pltpu.VMEM((N_SSM_BLOCKS, ROWS_A, LANES), F32),
            pltpu.VMEM((D_MODEL, 3 * BRANCH), BF16),
            pltpu.VMEM((ROWS_A, 3 * BRANCH), F32),
            pltpu.VMEM((ROWS_A, D_MODEL), BF16),
        ],
        compiler_params=pltpu.CompilerParams(
            dimension_semantics=("arbitrary",), vmem_limit_bytes=VMEM_LIMIT),
        name="prompt_mixer_a",
    )(x, x, p['g_mix_pre'], raw['w_in'], p['bblk'], p['cre'], p['cimn'], p['ar'], p['ai'],
      p['ssm_d'], p['w_glu'], p['b_glu'], p['pool_w'], p['pool_scale'],
      raw['w_in'], raw['w_in'], raw['w_up'], raw['w_out'], raw['w_ffn_in'], raw['w_ffn_out'],
      *sample)
    return outs[:6], dict(zip(cast_names, outs[6:-1])), outs[-1]


def _mem_kv_kernel(mem_ref, g_ref, wkv_ref, k_ref, v_ref):
    wkv = wkv_ref[...].astype(BF16)
    for b in range(BATCH):
        kv = _dot(_rms(mem_ref[b], g_ref[...]).astype(BF16), wkv)
        for hd in range(MEM_HEADS):
            rows = pl.ds(hd, N_MEM, stride=MEM_HEADS)
            k_ref[b, rows, :] = kv[:, hd * MEM_HEAD_DIM:(hd + 1) * MEM_HEAD_DIM]
            v_ref[b, rows, :] = kv[:, BRANCH + hd * MEM_HEAD_DIM:BRANCH + (hd + 1) * MEM_HEAD_DIM]


def _mem_kv(mem, g_mem, w_kv):
    out = jax.ShapeDtypeStruct((DEPTH, BATCH, KV_ROWS, MEM_HEAD_DIM), F32)
    out_spec = pl.BlockSpec((None, BATCH, KV_ROWS, MEM_HEAD_DIM), lambda l: (l, 0, 0, 0))
    return pl.pallas_call(
        _mem_kv_kernel,
        grid=(DEPTH,),
        in_specs=[
            _const_spec((BATCH, N_MEM, D_MODEL)),
            pl.BlockSpec((None, 1, D_MODEL), lambda l: (l, 0, 0)),
            pl.BlockSpec((None, D_MODEL, 2 * BRANCH), lambda l: (l, 0, 0)),
        ],
        out_specs=(out_spec, out_spec), out_shape=(out, out),
        compiler_params=pltpu.CompilerParams(
            dimension_semantics=("arbitrary",), vmem_limit_bytes=VMEM_LIMIT),
        name="mem_kv",
    )(mem, g_mem, w_kv)


def _memory_attention_stages(q_ref, k_ref, v_ref, result):
    scale = MEM_HEAD_DIM ** -0.5
    head_rows = [pl.ds(hd, N_MEM, stride=MEM_HEADS) for hd in range(MEM_HEADS)]
    head_cols = [slice(hd * MEM_HEAD_DIM, (hd + 1) * MEM_HEAD_DIM) for hd in range(MEM_HEADS)]
    scores = [lax.dot_general(q_ref[:, head_cols[hd]], k_ref[head_rows[hd], :].astype(BF16),
                              (((1,), (1,)), ((), ())), preferred_element_type=F32) * scale
              for hd in range(MEM_HEADS)]
    yield
    probs = []
    for s in scores:
        e = jnp.exp(s - jnp.max(s, axis=-1, keepdims=True))
        probs.append((e / jnp.sum(e, axis=-1, keepdims=True)).astype(BF16))
        yield
    outs = []
    for hd in range(MEM_HEADS):
        outs.append(_dot(probs[hd], v_ref[head_rows[hd], :].astype(BF16)).astype(BF16))
        yield
    result.append(jnp.concatenate(outs, axis=-1))


def _merge_ffn_kernel(with_attention, x_ref, ossm_ref, opool_ref, *refs):
    o_mem = []
    if with_attention:
        attention = _memory_attention_stages(*refs[:3], o_mem)
        refs = refs[3:]
    else:
        attention = iter(())
        o_mem.append(refs[0][...])
        refs = refs[1:]
    (gpre_ref, wgate_ref, wup_ref, wout_ref, gpost_ref, gfpre_ref, wfin_ref, wfout_ref,
     gfpost_ref, y_ref) = refs
    x = x_ref[...]
    h = _rms(x, gpre_ref[...]).astype(BF16)

    def branch_term(b, o_b, cols):
        gate_cols = slice(b * D_MODEL + cols.start, b * D_MODEL + cols.stop)
        gate = jax.nn.sigmoid(_dot(h, wgate_ref[:, gate_cols]))
        return gate * _dot(o_b, wup_ref[b * BRANCH:(b + 1) * BRANCH, cols])

    col_chunks = [slice(c, c + MERGE_COLS) for c in range(0, D_MODEL, MERGE_COLS)]
    next(attention, None)
    partial = []
    for b, o_b in enumerate((ossm_ref[...], opool_ref[...])):
        for ci, cols in enumerate(col_chunks):
            term = branch_term(b, o_b, cols)
            if b == 0:
                partial.append(term)
            else:
                partial[ci] = partial[ci] + term
            next(attention, None)
    for _ in attention:
        pass
    merged = jnp.concatenate(
        [partial[ci] + branch_term(N_BRANCH - 1, o_mem[0], cols)
         for ci, cols in enumerate(col_chunks)], axis=-1).astype(BF16)

    def tail_stages(rows):
        m2 = _dot(merged[rows, :], wout_ref[...])
        yield
        x1 = x[rows, :] + _rms(m2, gpost_ref[...])
        hf = _rms(x1, gfpre_ref[...]).astype(BF16)
        yield
        f = None
        for lo, hi in FF_CHUNKS:
            hg = _dot(hf, wfin_ref[:, lo:hi])
            hu = _dot(hf, wfin_ref[:, D_FF + lo:D_FF + hi])
            yield
            act = (jax.nn.silu(hg) * hu).astype(BF16)
            yield
            part = _dot(act, wfout_ref[lo:hi, :])
            f = part if f is None else f + part
            yield
        y_ref[rows, :] = x1 + _rms(f, gfpost_ref[...])

    n_groups = TAIL_GROUPS if x.shape[0] >= TAIL_GROUPS * PIECE_ROWS else 1
    group_rows = x.shape[0] // n_groups
    waiting = [tail_stages(slice(g * group_rows, (g + 1) * group_rows)) for g in range(n_groups)]
    running = []
    while waiting or running:
        if waiting:
            running.append(waiting.pop(0))
        for chain in list(running):
            if next(chain, StopIteration) is StopIteration:
                running.remove(chain)


def _merge_ffn(layer, x, o_ssm, o_pool, third, p, w):
    n_rows = x.shape[0]
    rows = min(ROWS_C, n_rows)
    row_spec = lambda w: pl.BlockSpec((rows, w), lambda i: (i, 0))
    with_attention = isinstance(third, tuple)
    if with_attention:
        tiles_per_seq = SEQ // rows
        kv_spec = pl.BlockSpec((None, None, KV_ROWS, MEM_HEAD_DIM),
                               lambda i: (layer, i // tiles_per_seq, 0, 0))
        third_specs = [row_spec(BRANCH), kv_spec, kv_spec]
    else:
        third = (third,)
        third_specs = [row_spec(BRANCH)]
    in_specs = [row_spec(D_MODEL), row_spec(BRANCH), row_spec(BRANCH)] + third_specs + [
        _layer_spec((1, D_MODEL), layer),
        _const_spec((D_MODEL, N_BRANCH * D_MODEL)),
        _const_spec((N_BRANCH * BRANCH, D_MODEL)),
        _const_spec((D_MODEL, D_MODEL)),
        _layer_spec((1, D_MODEL), layer),
        _layer_spec((1, D_MODEL), layer),
        _const_spec((D_MODEL, 2 * D_FF)),
        _const_spec((D_FF, D_MODEL)),
        _layer_spec((1, D_MODEL), layer),
    ]
    return pl.pallas_call(
        functools.partial(_merge_ffn_kernel, with_attention),
        grid=(n_rows // rows,),
        in_specs=in_specs, out_specs=row_spec(D_MODEL),
        out_shape=jax.ShapeDtypeStruct((n_rows, D_MODEL), F32),
        compiler_params=pltpu.CompilerParams(
            dimension_semantics=("parallel",), vmem_limit_bytes=VMEM_LIMIT),
        name="merge_ffn",
    )(x, o_ssm, o_pool, *third, p['g_mix_pre'], w['w_gate'], w['w_up'], w['w_out'],
      p['g_mix_post'], p['g_ffn_pre'], w['w_ffn_in'], w['w_ffn_out'], p['g_ffn_post'])


def _sample_mixer_kernel(x_ref, g_ref, win_ref, bblk_ref, cre_ref, cimn_ref, ar_ref, ai_ref,
                         d_ref, wglu_ref, bglu_ref, poolw_ref, pscale_ref,
                         h0re_ref, h0im_ref, hist_ref,
                         q_ref, ossm_ref, opool_ref, hre_ref, him_ref, nhist_ref,
                         bu_re_ref, bu_im_ref):
    h = _rms(x_ref[...], g_ref[...]).astype(BF16)
    proj = _dot(h, win_ref[...].astype(BF16))
    u_ssm = proj[:, :BRANCH]
    u_pool = proj[:, BRANCH:2 * BRANCH]
    q_ref[...] = proj[:, 2 * BRANCH:]

    ys = []
    for k in range(N_SSM_BLOCKS):
        sl = slice(k * BLOCK_STATE, (k + 1) * BLOCK_STATE)
        _ssm_input(u_ssm[:, k * LANES:(k + 1) * LANES], bblk_ref[k], bu_re_ref, bu_im_ref, sl)
        a_r = ar_ref[:, sl]
        a_i = ai_ref[:, sl]
        h0r = h0re_ref[:, sl]
        h0i = h0im_ref[:, sl]
        hre_ref[:, sl] = bu_re_ref[:, sl] + a_r * h0r - a_i * h0i
        him_ref[:, sl] = bu_im_ref[:, sl] + a_r * h0i + a_i * h0r
        ys.append(_ssm_readout(hre_ref, him_ref, sl, cre_ref[k], cimn_ref[k]))
    y = _ssm_gate(jnp.concatenate(ys, axis=-1), u_ssm, d_ref, wglu_ref, bglu_ref)
    ossm_ref[...] = y.astype(BF16)

    pooled = []
    for gi, w in enumerate(POOL_WINDOWS):
        sl = slice(gi * POOL_GROUP_WIDTH, (gi + 1) * POOL_GROUP_WIDTH)
        acc = u_pool[:, sl]
        for j in range(1, w):
            acc = acc + hist_ref[POOL_HIST - j, :, sl]
        cnt = float(min(PAST_LEN + 1, w))
        pooled.append(acc / cnt - u_pool[:, sl])
    opool_ref[...] = _pool_mix(pooled, poolw_ref, pscale_ref).astype(BF16)
    for j in range(POOL_HIST - 1):
        nhist_ref[j] = hist_ref[j + 1]
    nhist_ref[POOL_HIST - 1] = u_pool


def _sample_mixer(layer, x_s, p, w_in, h0_re, h0_im, hist_t):
    n = x_s.shape[0]
    full = lambda shape: pl.BlockSpec(shape, lambda i: (0,) * len(shape))
    in_specs = [
        full((n, D_MODEL)),
        _layer_spec((1, D_MODEL), layer),
        _layer_spec((D_MODEL, 3 * BRANCH), layer),
        _layer_spec((N_SSM_BLOCKS, LANES, 2 * BLOCK_STATE), layer),
        _layer_spec((N_SSM_BLOCKS, BLOCK_STATE, LANES), layer),
        _layer_spec((N_SSM_BLOCKS, BLOCK_STATE, LANES), layer),
        _layer_spec((1, N_STATE), layer),
        _layer_spec((1, N_STATE), layer),
        _layer_spec((1, BRANCH), layer),
        _layer_spec((BRANCH, BRANCH), layer),
        _layer_spec((1, BRANCH), layer),
        _layer_spec((len(POOL_WINDOWS), POOL_GROUP_WIDTH, POOL_GROUP_WIDTH), layer),
        _layer_spec((1, BRANCH), layer),
        _layer_spec((n, N_STATE), layer),
        _layer_spec((n, N_STATE), layer),
        _layer_spec((POOL_HIST, n, BRANCH), layer),
    ]
    out_shape = (
        jax.ShapeDtypeStruct((n, BRANCH), F32),
        jax.ShapeDtypeStruct((n, BRANCH), BF16),
        jax.ShapeDtypeStruct((n, BRANCH), BF16),
        jax.ShapeDtypeStruct((n, N_STATE), F32),
        jax.ShapeDtypeStruct((n, N_STATE), F32),
        jax.ShapeDtypeStruct((POOL_HIST, n, BRANCH), F32),
    )
    out_specs = tuple(full(s.shape) for s in out_shape)
    return pl.pallas_call(
        _sample_mixer_kernel,
        grid=(1,),
        in_specs=in_specs, out_specs=out_specs, out_shape=out_shape,
        scratch_shapes=[pltpu.VMEM((n, N_STATE), F32), pltpu.VMEM((n, N_STATE), F32)],
        compiler_params=pltpu.CompilerParams(
            dimension_semantics=("arbitrary",), vmem_limit_bytes=VMEM_LIMIT),
        name="sample_mixer",
    )(x_s, p['g_mix_pre'], w_in, p['bblk'], p['cre'], p['cimn'], p['ar'], p['ai'],
      p['ssm_d'], p['w_glu'], p['b_glu'], p['pool_w'], p['pool_scale'], h0_re, h0_im, hist_t)


def kernel(x_prompt, x_sample, mem_prompt, cache_mem_k, cache_mem_v, state_ssm_re, state_ssm_im, state_pool, g_mix_pre, g_mix_post, g_ffn_pre, g_ffn_post, g_mem, w_in, w_kv, ssm_lam_re, ssm_lam_im, ssm_log_dt, ssm_b_re, ssm_b_im, ssm_c_re, ssm_c_im, ssm_d, ssm_w_glu, ssm_b_glu, pool_w, pool_scale, w_branch_up, w_out, w_ffn_in, w_ffn_out):
    ar, ai, bblk = _discretise(ssm_lam_re, ssm_lam_im, ssm_log_dt, ssm_b_re, ssm_b_im)
    vec = lambda a: a.reshape(DEPTH, 1, a.shape[-1])
    p = {
        'g_mix_pre': vec(g_mix_pre), 'g_mix_post': vec(g_mix_post),
        'g_ffn_pre': vec(g_ffn_pre), 'g_ffn_post': vec(g_ffn_post),
        'bblk': bblk, 'ar': ar, 'ai': ai,
        'cre': _c_blocks(ssm_c_re), 'cimn': _c_blocks(-ssm_c_im),
        'ssm_d': vec(ssm_d), 'w_glu': ssm_w_glu.astype(BF16), 'b_glu': vec(ssm_b_glu),
        'pool_w': pool_w.astype(BF16), 'pool_scale': vec(pool_scale),
    }
    raw = {
        'w_in': w_in, 'w_up': w_branch_up.reshape(DEPTH, N_BRANCH * BRANCH, D_MODEL),
        'w_out': w_out, 'w_ffn_in': w_ffn_in, 'w_ffn_out': w_ffn_out,
    }
    k_mem, v_mem = _mem_kv(mem_prompt, vec(g_mem), w_kv)

    lane_ones = jnp.ones((MEM_HEAD_DIM, MEM_HEAD_DIM), BF16)
    k_cache = cache_mem_k.reshape(DEPTH, DEC_BATCH, KV_ROWS, MEM_HEAD_DIM)
    v_cache = cache_mem_v.reshape(DEPTH, DEC_BATCH, KV_ROWS, MEM_HEAD_DIM)
    h0_re = state_ssm_re.reshape(DEPTH, DEC_BATCH, N_STATE)
    h0_im = state_ssm_im.reshape(DEPTH, DEC_BATCH, N_STATE)
    hist_t = jnp.swapaxes(state_pool, 1, 2)

    xp = x_prompt
    xs = x_sample.reshape(DEC_BATCH, D_MODEL)
    n_prompt = BATCH * SEQ
    re_p, im_p, pool_p, re_s, im_s, pool_s = [], [], [], [], [], []
    for layer in range(DEPTH):
        qs, os_ssm, os_pool, hre_s, him_s, nhist = _sample_mixer(layer, xs, p, w_in, h0_re, h0_im,
                                                                 hist_t)
        qs4 = qs.reshape(DEC_BATCH, MEM_HEADS, MEM_HEAD_DIM)
        sample = (jnp.concatenate([qs4, qs4], axis=1), k_cache, v_cache, lane_ones)
        (q, o_ssm, o_pool, hre, him, hist), w_bf, os_mem = _mixer_a(layer, xp, p, raw, sample)
        xp = _merge_ffn(layer, xp.reshape(n_prompt, D_MODEL), o_ssm.reshape(n_prompt, BRANCH),
                        o_pool.reshape(n_prompt, BRANCH),
                        (q.reshape(n_prompt, BRANCH), k_mem, v_mem), p, w_bf)
        xp = xp.reshape(BATCH, SEQ, D_MODEL)
        re_p.append(hre.reshape(BATCH, SSM_GROUPS, SSM_STATE))
        im_p.append(him.reshape(BATCH, SSM_GROUPS, SSM_STATE))
        pool_p.append(jnp.swapaxes(hist.reshape(POOL_HIST, BATCH, BRANCH), 0, 1))

        os_mem = os_mem[:, :MEM_HEADS].reshape(DEC_BATCH, BRANCH).astype(BF16)
        xs = _merge_ffn(layer, xs, os_ssm, os_pool, os_mem, p, w_bf)
        re_s.append(hre_s.reshape(DEC_BATCH, SSM_GROUPS, SSM_STATE))
        im_s.append(him_s.reshape(DEC_BATCH, SSM_GROUPS, SSM_STATE))
        pool_s.append(jnp.swapaxes(nhist, 0, 1))

    y_prompt = xp
    y_sample = xs.reshape(DEC_BATCH, 1, D_MODEL)
    kv_shape = (DEPTH, BATCH, N_MEM, MEM_HEADS, MEM_HEAD_DIM)
    return (y_prompt, y_sample,
            jnp.stack(re_p), jnp.stack(im_p), jnp.stack(pool_p),
            k_mem.reshape(kv_shape), v_mem.reshape(kv_shape),
            jnp.stack(re_s), jnp.stack(im_s), jnp.stack(pool_s))
```

```python
import functools

import jax
import jax.numpy as jnp
from jax import lax
from jax.experimental import pallas as pl
from jax.experimental.pallas import tpu as pltpu

D_MODEL = 1024
BATCH = 8
SEQ = 2048
DEPTH = 4
DEC_BATCH = 128
PAST_LEN = 16384

BRANCH = D_MODEL // 2
SSM_GROUP = 16
SSM_GROUPS = BRANCH // SSM_GROUP
SSM_STATE = 64
N_STATE = SSM_GROUPS * SSM_STATE
POOL_WINDOWS = (2, 4, 8, 16)
POOL_GROUP_WIDTH = BRANCH // len(POOL_WINDOWS)
POOL_HIST = max(POOL_WINDOWS) - 1
N_MEM = 256
MEM_HEADS = 4
MEM_HEAD_DIM = BRANCH // MEM_HEADS
KV_ROWS = N_MEM * MEM_HEADS
N_BRANCH = 3
GATE_HALF = N_BRANCH * D_MODEL // 2
D_FF = 2816
RMS_EPS = 1e-6

LANES = 128
SUBLANES = 8
GROUPS_PER_BLOCK = LANES // SSM_GROUP
N_SSM_BLOCKS = BRANCH // LANES
BLOCK_STATE = GROUPS_PER_BLOCK * SSM_STATE

ROWS_A = 512
STEPS_A = ROWS_A // BATCH
ROWS_C = 512
MERGE_COLS = 256
TAIL_GROUPS = 2
PIECE_ROWS = 128
SAMPLES_PER_STEP = DEC_BATCH // (BATCH * SEQ // ROWS_C)
FF_CHUNKS = ((0, 1536), (1536, 2816))
VMEM_LIMIT = 56 * 1024 * 1024
VMEM_LIMIT_MERGE = 60 * 1024 * 1024

BF16 = jnp.bfloat16
F32 = jnp.float32


def _rms(x, g):
    ms = jnp.mean(x * x, axis=-1, keepdims=True)
    return x * lax.rsqrt(ms + RMS_EPS) * g


def _dot(a, b):
    return jnp.dot(a, b, preferred_element_type=F32)


def _const_spec(shape):
    nd = len(shape)
    return pl.BlockSpec(shape, lambda *_: (0,) * nd, pipeline_mode=pl.Buffered(1))


def _layer_spec(shape, layer):
    nd = len(shape)
    return pl.BlockSpec((None,) + tuple(shape), lambda *_: (layer,) + (0,) * nd,
                        pipeline_mode=pl.Buffered(1))


def _discretise_kernel(lr_ref, li_ref, ldt_ref, br_ref, bi_ref,
                       ar_ref, ai_ref, bbr_ref, bbi_ref):
    lr = lr_ref[...]
    li = li_ref[...]
    dt = jnp.exp(ldt_ref[...])
    zr = lr * dt
    zi = li * dt
    mag = jnp.exp(zr)
    ar = mag * jnp.cos(zi)
    ai = mag * jnp.sin(zi)
    den = lr * lr + li * li
    fr = ((ar - 1.0) * lr + ai * li) / den
    fi = (ai * lr - (ar - 1.0) * li) / den
    br = br_ref[...]
    bi = bi_ref[...]
    ar_ref[...] = ar
    ai_ref[...] = ai
    bbr_ref[...] = fr * br - fi * bi
    bbi_ref[...] = fr * bi + fi * br


def _discretise(lam_re, lam_im, log_dt, b_re, b_im):
    rows = DEPTH * SSM_GROUPS
    width = SSM_GROUP * SSM_STATE

    def tile_p(a):
        return jnp.tile(a.reshape(rows, 1, SSM_STATE), (1, SSM_GROUP, 1)).reshape(rows, width)

    def b_t(a):
        return jnp.swapaxes(a, -1, -2).reshape(rows, width)

    ldt = jnp.broadcast_to(log_dt.reshape(rows, 1), (rows, width))
    out = jax.ShapeDtypeStruct((rows, width), F32)
    ar, ai, bbr, bbi = pl.pallas_call(
        _discretise_kernel, out_shape=(out, out, out, out), name="s5_discretise",
    )(tile_p(lam_re), tile_p(lam_im), ldt, b_t(b_re), b_t(b_im))
    ar = ar[:, :SSM_STATE].reshape(DEPTH, 1, N_STATE)
    ai = ai[:, :SSM_STATE].reshape(DEPTH, 1, N_STATE)
    shape5 = (DEPTH, N_SSM_BLOCKS, GROUPS_PER_BLOCK, SSM_GROUP, SSM_STATE)
    bblk = jnp.concatenate([_group_block_diag(bbr.reshape(shape5)),
                            _group_block_diag(bbi.reshape(shape5))], axis=-1)
    return ar, ai, bblk


def _group_block_diag(m):
    depth, blocks, groups, n_r, n_c = m.shape
    same = jnp.eye(groups, dtype=bool)[None, None, :, None, :, None]
    out = jnp.where(same, m[:, :, :, :, None, :], 0.0).astype(BF16)
    return out.reshape(depth, blocks, groups * n_r, groups * n_c)


def _c_blocks(c):
    shape5 = (DEPTH, N_SSM_BLOCKS, GROUPS_PER_BLOCK, SSM_GROUP, SSM_STATE)
    return _group_block_diag(jnp.swapaxes(c.reshape(shape5), -1, -2))


def _ssm_input(u_block, bblk, bu_re_ref, bu_im_ref, sl):
    bu = _dot(u_block.astype(BF16), bblk)
    bu_re_ref[:, sl] = bu[:, :BLOCK_STATE]
    bu_im_ref[:, sl] = bu[:, BLOCK_STATE:]


def _ssm_readout(h_re_ref, h_im_ref, sl, cre, cimn):
    return _dot(h_re_ref[:, sl].astype(BF16), cre) + _dot(h_im_ref[:, sl].astype(BF16), cimn)


def _ssm_gate(y, u_ssm, d_ref, wglu_ref, bglu_ref):
    y = jax.nn.gelu(y + d_ref[...] * u_ssm)
    return y * jax.nn.sigmoid(_dot(y.astype(BF16), wglu_ref[...]) + bglu_ref[...])


def _pool_mix(pooled, poolw_ref, pscale_ref):
    outs = []
    for gi in range(len(POOL_WINDOWS)):
        sl = slice(gi * POOL_GROUP_WIDTH, (gi + 1) * POOL_GROUP_WIDTH)
        outs.append(_dot(pooled[gi].astype(BF16), poolw_ref[gi]) * pscale_ref[:, sl])
    return jnp.concatenate(outs, axis=-1)


def _sample_pair_stages(sq_ref, sk_ref, sv_ref, ones_ref, so_ref, first):
    tiles = KV_ROWS // SUBLANES
    width = 2 * MEM_HEAD_DIM
    pair = (first, first + 1)
    q8 = jnp.concatenate([sq_ref[n] for n in pair], axis=-1) * MEM_HEAD_DIM ** -0.5
    k = jnp.concatenate([sk_ref[n] for n in pair], axis=-1)
    prod = (k.reshape(tiles, SUBLANES, width) * q8[None]).reshape(KV_ROWS, width).astype(BF16)
    s = _dot(prod, ones_ref[...]).reshape(tiles, SUBLANES, width)
    yield
    v = jnp.concatenate([sv_ref[n] for n in pair], axis=-1).reshape(tiles, SUBLANES, width)
    m8 = jnp.max(s, axis=0)
    m8 = jnp.maximum(m8, pltpu.roll(m8, MEM_HEADS, 0))
    e = jnp.exp(s - m8[None])
    l8 = jnp.sum(e, axis=0)
    acc = jnp.sum(e * v, axis=0)
    l8 = l8 + pltpu.roll(l8, MEM_HEADS, 0)
    acc = acc + pltpu.roll(acc, MEM_HEADS, 0)
    out = acc / l8
    for j, n in enumerate(pair):
        so_ref[n] = out[:, j * MEM_HEAD_DIM:(j + 1) * MEM_HEAD_DIM]


def _mixer_a_kernel(xfirst_ref, xnext_ref, g_ref, win_ref, bblk_ref, cre_ref, cimn_ref,
                    ar_ref, ai_ref, d_ref, wglu_ref, bglu_ref, poolw_ref, pscale_ref,
                    wgate_lo_ref, wgate_hi_ref, wup_ref, wout_ref, wfin_ref, wfout_ref,
                    q_ref, ossm_ref, opool_ref, hre_ref, him_ref, hist_ref,
                    wgate_bf_ref, wup_bf_ref, wout_bf_ref, wfin_bf_ref, wfout_bf_ref,
                    bu_re_ref, bu_im_ref, pool_buf, ussm_tm, ossm_tm, opool_tm, win_bf,
                    proj_scr, hnext_scr):
    i = pl.program_id(0)
    hist_rows = POOL_HIST * BATCH

    wgate_bf_ref[:, :GATE_HALF] = wgate_lo_ref[...].astype(BF16)
    wgate_bf_ref[:, GATE_HALF:] = wgate_hi_ref[...].astype(BF16)
    wup_bf_ref[...] = wup_ref[...].astype(BF16)
    wout_bf_ref[...] = wout_ref[...].astype(BF16)
    wfin_bf_ref[...] = wfin_ref[...].astype(BF16)
    wfout_bf_ref[...] = wfout_ref[...].astype(BF16)

    def normed(x_blk_ref):
        return _rms(x_blk_ref[...].reshape(ROWS_A, D_MODEL), g_ref[...]).astype(BF16)

    @pl.when(i == 0)
    def _():
        hre_ref[...] = jnp.zeros_like(hre_ref)
        him_ref[...] = jnp.zeros_like(him_ref)
        pool_buf[:, 0:hist_rows, :] = jnp.zeros((N_SSM_BLOCKS, hist_rows, LANES), F32)
        win_bf[...] = win_ref[...].astype(BF16)
        proj_scr[...] = _dot(normed(xfirst_ref), win_bf[...])

    for b in range(BATCH):
        rows = slice(b * STEPS_A, (b + 1) * STEPS_A)
        q_ref[b] = proj_scr[rows, 2 * BRANCH:].astype(BF16)
        for k in range(N_SSM_BLOCKS):
            ussm_tm[k, pl.ds(b, STEPS_A, stride=BATCH), :] = (
                proj_scr[rows, k * LANES:(k + 1) * LANES])
            pool_buf[k, pl.ds(hist_rows + b, STEPS_A, stride=BATCH), :] = (
                proj_scr[rows, BRANCH + k * LANES:BRANCH + (k + 1) * LANES])
    u_ssm = jnp.concatenate([ussm_tm[k] for k in range(N_SSM_BLOCKS)], axis=-1)
    hnext_scr[...] = normed(xnext_ref)
    next_cols = [slice(c, c + MERGE_COLS) for c in range(0, 3 * BRANCH, MERGE_COLS)]

    def project_next_chunk():
        if next_cols:
            cols = next_cols.pop(0)
            proj_scr[:, cols] = _dot(hnext_scr[...], win_bf[:, cols])

    ys = []
    for k in range(N_SSM_BLOCKS):
        sl = slice(k * BLOCK_STATE, (k + 1) * BLOCK_STATE)
        _ssm_input(ussm_tm[k], bblk_ref[k], bu_re_ref, bu_im_ref, sl)
        project_next_chunk()
        a_r = jnp.broadcast_to(ar_ref[:, sl], (BATCH, BLOCK_STATE))
        a_i = jnp.broadcast_to(ai_ref[:, sl], (BATCH, BLOCK_STATE))
        hr = hre_ref[:, sl]
        hi = him_ref[:, sl]
        for t in range(STEPS_A):
            rows = slice(t * BATCH, (t + 1) * BATCH)
            hr, hi = (a_r * hr - a_i * hi + bu_re_ref[rows, sl],
                      a_r * hi + a_i * hr + bu_im_ref[rows, sl])
            bu_re_ref[rows, sl] = hr
            bu_im_ref[rows, sl] = hi
        hre_ref[:, sl] = hr
        him_ref[:, sl] = hi
        ys.append(_ssm_readout(bu_re_ref, bu_im_ref, sl, cre_ref[k], cimn_ref[k]))
    project_next_chunk()
    o_ssm = _ssm_gate(jnp.concatenate(ys, axis=-1), u_ssm, d_ref, wglu_ref, bglu_ref)

    project_next_chunk()
    t_pos = i * STEPS_A + lax.broadcasted_iota(jnp.int32, (ROWS_A, POOL_GROUP_WIDTH), 0) // BATCH
    pooled = []
    for gi, w in enumerate(POOL_WINDOWS):
        u_g = pool_buf[gi, hist_rows:hist_rows + ROWS_A, :]
        acc = u_g
        for j in range(1, w):
            start = hist_rows - j * BATCH
            acc = acc + pool_buf[gi, start:start + ROWS_A, :]
        cnt = jnp.minimum(t_pos + 1, w).astype(F32)
        pooled.append(acc / cnt - u_g)
    o_pool = _pool_mix(pooled, poolw_ref, pscale_ref)
    pool_buf[:, 0:hist_rows, :] = pool_buf[:, ROWS_A:ROWS_A + hist_rows, :]

    for k in range(N_SSM_BLOCKS):
        ossm_tm[k] = o_ssm[:, k * LANES:(k + 1) * LANES]
        opool_tm[k] = o_pool[:, k * LANES:(k + 1) * LANES]
    for b in range(BATCH):
        rows = pl.ds(b, STEPS_A, stride=BATCH)
        ossm_ref[b] = jnp.concatenate(
            [ossm_tm[k, rows, :] for k in range(N_SSM_BLOCKS)], axis=-1).astype(BF16)
        opool_ref[b] = jnp.concatenate(
            [opool_tm[k, rows, :] for k in range(N_SSM_BLOCKS)], axis=-1).astype(BF16)

    @pl.when(i == pl.num_programs(0) - 1)
    def _():
        for k in range(N_SSM_BLOCKS):
            hist_ref[:, k * LANES:(k + 1) * LANES] = pool_buf[k, 0:hist_rows, :]


def _mixer_a(layer, x, p, raw):
    hist_rows = POOL_HIST * BATCH
    n_steps = SEQ // STEPS_A
    seq_spec = lambda w: pl.BlockSpec((BATCH, STEPS_A, w), lambda i: (0, i, 0))
    cast_plan = {
        'w_gate': (D_MODEL // n_steps, N_BRANCH * D_MODEL, 1),
        'w_up': (N_BRANCH * BRANCH // n_steps, D_MODEL, 1),
        'w_out': (D_MODEL // n_steps, D_MODEL, 1),
        'w_ffn_in': (D_MODEL // n_steps, 2 * D_FF, 1),
        'w_ffn_out': (2 * D_FF // n_steps, D_MODEL, 2),
    }

    def chunk_in(name, width=None, col=0):
        rows, full_width, per = cast_plan[name]
        return pl.BlockSpec((None, rows, width or full_width),
                            lambda i: (layer, i // per, col))

    def chunk_out(name):
        rows, width, per = cast_plan[name]
        return pl.BlockSpec((rows, width), lambda i: (i // per, 0))

    cast_names = ('w_gate', 'w_up', 'w_out', 'w_ffn_in', 'w_ffn_out')
    cast_shapes = tuple(
        jax.ShapeDtypeStruct((cast_plan[n][0] * n_steps // cast_plan[n][2], cast_plan[n][1]), BF16)
        for n in cast_names)
    in_specs = [
        pl.BlockSpec((BATCH, STEPS_A, D_MODEL), lambda i: (0, 0, 0), pipeline_mode=pl.Buffered(1)),
        pl.BlockSpec((BATCH, STEPS_A, D_MODEL), lambda i: (0, jnp.minimum(i + 1, n_steps - 1), 0)),
        _layer_spec((1, D_MODEL), layer),
        _layer_spec((D_MODEL, 3 * BRANCH), layer),
        _layer_spec((N_SSM_BLOCKS, LANES, 2 * BLOCK_STATE), layer),
        _layer_spec((N_SSM_BLOCKS, BLOCK_STATE, LANES), layer),
        _layer_spec((N_SSM_BLOCKS, BLOCK_STATE, LANES), layer),
        _layer_spec((1, N_STATE), layer),
        _layer_spec((1, N_STATE), layer),
        _layer_spec((1, BRANCH), layer),
        _layer_spec((BRANCH, BRANCH), layer),
        _layer_spec((1, BRANCH), layer),
        _layer_spec((len(POOL_WINDOWS), POOL_GROUP_WIDTH, POOL_GROUP_WIDTH), layer),
        _layer_spec((1, BRANCH), layer),
        chunk_in('w_gate', GATE_HALF, 1), chunk_in('w_gate', GATE_HALF, 2),
        chunk_in('w_up'), chunk_in('w_out'), chunk_in('w_ffn_in'), chunk_in('w_ffn_out'),
    ]
    branch_out = jax.ShapeDtypeStruct((BATCH, SEQ, BRANCH), BF16)
    out_shape = (
        branch_out, branch_out, branch_out,
        jax.ShapeDtypeStruct((BATCH, N_STATE), F32),
        jax.ShapeDtypeStruct((BATCH, N_STATE), F32),
        jax.ShapeDtypeStruct((hist_rows, BRANCH), F32),
    ) + cast_shapes
    out_specs = (
        seq_spec(BRANCH), seq_spec(BRANCH), seq_spec(BRANCH),
        pl.BlockSpec((BATCH, N_STATE), lambda i: (0, 0)),
        pl.BlockSpec((BATCH, N_STATE), lambda i: (0, 0)),
        pl.BlockSpec((hist_rows, BRANCH), lambda i: (0, 0)),
    ) + tuple(chunk_out(n) for n in cast_names)
    outs = pl.pallas_call(
        _mixer_a_kernel,
        grid=(n_steps,),
        in_specs=in_specs, out_specs=out_specs, out_shape=out_shape,
        scratch_shapes=[
            pltpu.VMEM((ROWS_A, N_STATE), F32),
            pltpu.VMEM((ROWS_A, N_STATE), F32),
            pltpu.VMEM((N_SSM_BLOCKS, hist_rows + ROWS_A, LANES), F32),
            pltpu.VMEM((N_SSM_BLOCKS, ROWS_A, LANES), F32),
            pltpu.VMEM((N_SSM_BLOCKS, ROWS_A, LANES), F32),
            pltpu.VMEM((N_SSM_BLOCKS, ROWS_A, LANES), F32),
            pltpu.VMEM((D_MODEL, 3 * BRANCH), BF16),
            pltpu.VMEM((ROWS_A, 3 * BRANCH), F32),
            pltpu.VMEM((ROWS_A, D_MODEL), BF16),
        ],
        compiler_params=pltpu.CompilerParams(
            dimension_semantics=("arbitrary",), vmem_limit_bytes=VMEM_LIMIT),
        name="prompt_mixer_a",
    )(x, x, p['g_mix_pre'], raw['w_in'], p['bblk'], p['cre'], p['cimn'], p['ar'], p['ai'],
      p['ssm_d'], p['w_glu'], p['b_glu'], p['pool_w'], p['pool_scale'],
      raw['w_in'], raw['w_in'], raw['w_up'], raw['w_out'], raw['w_ffn_in'], raw['w_ffn_out'])
    return outs[:6], dict(zip(cast_names, outs[6:]))


def _mem_kv_kernel(mem_ref, g_ref, wkv_ref, k_ref, v_ref):
    wkv = wkv_ref[...].astype(BF16)
    for b in range(BATCH):
        kv = _dot(_rms(mem_ref[b], g_ref[...]).astype(BF16), wkv)
        for hd in range(MEM_HEADS):
            rows = pl.ds(hd, N_MEM, stride=MEM_HEADS)
            k_ref[b, rows, :] = kv[:, hd * MEM_HEAD_DIM:(hd + 1) * MEM_HEAD_DIM]
            v_ref[b, rows, :] = kv[:, BRANCH + hd * MEM_HEAD_DIM:BRANCH + (hd + 1) * MEM_HEAD_DIM]


def _mem_kv(mem, g_mem, w_kv):
    out = jax.ShapeDtypeStruct((DEPTH, BATCH, KV_ROWS, MEM_HEAD_DIM), F32)
    out_spec = pl.BlockSpec((None, BATCH, KV_ROWS, MEM_HEAD_DIM), lambda l: (l, 0, 0, 0))
    return pl.pallas_call(
        _mem_kv_kernel,
        grid=(DEPTH,),
        in_specs=[
            _const_spec((BATCH, N_MEM, D_MODEL)),
            pl.BlockSpec((None, 1, D_MODEL), lambda l: (l, 0, 0)),
            pl.BlockSpec((None, D_MODEL, 2 * BRANCH), lambda l: (l, 0, 0)),
        ],
        out_specs=(out_spec, out_spec), out_shape=(out, out),
        compiler_params=pltpu.CompilerParams(
            dimension_semantics=("arbitrary",), vmem_limit_bytes=VMEM_LIMIT),
        name="mem_kv",
    )(mem, g_mem, w_kv)


def _memory_attention_stages(q_ref, k_ref, v_ref, result):
    scale = MEM_HEAD_DIM ** -0.5
    head_rows = [pl.ds(hd, N_MEM, stride=MEM_HEADS) for hd in range(MEM_HEADS)]
    head_cols = [slice(hd * MEM_HEAD_DIM, (hd + 1) * MEM_HEAD_DIM) for hd in range(MEM_HEADS)]
    scores = [lax.dot_general(q_ref[:, head_cols[hd]], k_ref[head_rows[hd], :].astype(BF16),
                              (((1,), (1,)), ((), ())), preferred_element_type=F32) * scale
              for hd in range(MEM_HEADS)]
    yield
    probs = []
    for s in scores:
        e = jnp.exp(s - jnp.max(s, axis=-1, keepdims=True))
        probs.append((e / jnp.sum(e, axis=-1, keepdims=True)).astype(BF16))
        yield
    outs = []
    for hd in range(MEM_HEADS):
        outs.append(_dot(probs[hd], v_ref[head_rows[hd], :].astype(BF16)).astype(BF16))
        yield
    result.append(jnp.concatenate(outs, axis=-1))


def _merge_ffn_kernel(with_attention, x_ref, ossm_ref, opool_ref, *refs):
    o_mem = []
    side_jobs = []
    if with_attention:
        attention = _memory_attention_stages(*refs[:3], o_mem)
        sq_ref, sk_ref, sv_ref, ones_ref = refs[3:7]
        so_ref = refs[-1]
        side_jobs = [_sample_pair_stages(sq_ref, sk_ref, sv_ref, ones_ref, so_ref, first)
                     for first in range(0, SAMPLES_PER_STEP, 2)]
        refs = refs[7:-1]
    else:
        attention = iter(())
        o_mem.append(refs[0][...])
        refs = refs[1:]
    (gpre_ref, wgate_ref, wup_ref, wout_ref, gpost_ref, gfpre_ref, wfin_ref, wfout_ref,
     gfpost_ref, y_ref) = refs
    x = x_ref[...]
    h = _rms(x, gpre_ref[...]).astype(BF16)

    def branch_term(b, o_b, cols):
        gate_cols = slice(b * D_MODEL + cols.start, b * D_MODEL + cols.stop)
        gate = jax.nn.sigmoid(_dot(h, wgate_ref[:, gate_cols]))
        return gate * _dot(o_b, wup_ref[b * BRANCH:(b + 1) * BRANCH, cols])

    col_chunks = [slice(c, c + MERGE_COLS) for c in range(0, D_MODEL, MERGE_COLS)]
    next(attention, None)
    partial = []
    for b, o_b in enumerate((ossm_ref[...], opool_ref[...])):
        for ci, cols in enumerate(col_chunks):
            term = branch_term(b, o_b, cols)
            if b == 0:
                partial.append(term)
            else:
                partial[ci] = partial[ci] + term
            next(attention, None)
    for _ in attention:
        pass
    merged = jnp.concatenate(
        [partial[ci] + branch_term(N_BRANCH - 1, o_mem[0], cols)
         for ci, cols in enumerate(col_chunks)], axis=-1).astype(BF16)

    def tail_stages(rows):
        m2 = _dot(merged[rows, :], wout_ref[...])
        yield
        x1 = x[rows, :] + _rms(m2, gpost_ref[...])
        hf = _rms(x1, gfpre_ref[...]).astype(BF16)
        yield
        f = None
        for lo, hi in FF_CHUNKS:
            hg = _dot(hf, wfin_ref[:, lo:hi])
            hu = _dot(hf, wfin_ref[:, D_FF + lo:D_FF + hi])
            yield
            act = (jax.nn.silu(hg) * hu).astype(BF16)
            yield
            part = _dot(act, wfout_ref[lo:hi, :])
            f = part if f is None else f + part
            yield
        y_ref[rows, :] = x1 + _rms(f, gfpost_ref[...])

    n_groups = TAIL_GROUPS if x.shape[0] >= TAIL_GROUPS * PIECE_ROWS else 1
    group_rows = x.shape[0] // n_groups
    waiting = [tail_stages(slice(g * group_rows, (g + 1) * group_rows)) for g in range(n_groups)]
    waiting += side_jobs
    running = []
    while waiting or running:
        if waiting:
            running.append(waiting.pop(0))
        for chain in list(running):
            if next(chain, StopIteration) is StopIteration:
                running.remove(chain)


def _merge_ffn(layer, x, o_ssm, o_pool, third, p, w):
    n_rows = x.shape[0]
    rows = min(ROWS_C, n_rows)
    row_spec = lambda w: pl.BlockSpec((rows, w), lambda i: (i, 0))
    with_attention = isinstance(third, tuple)
    if with_attention:
        tiles_per_seq = SEQ // rows
        kv_spec = pl.BlockSpec((None, None, KV_ROWS, MEM_HEAD_DIM),
                               lambda i: (layer, i // tiles_per_seq, 0, 0))
        sample_spec = pl.BlockSpec((SAMPLES_PER_STEP, SUBLANES, MEM_HEAD_DIM), lambda i: (i, 0, 0))
        cache_spec = pl.BlockSpec((None, SAMPLES_PER_STEP, KV_ROWS, MEM_HEAD_DIM),
                                  lambda i: (layer, i, 0, 0))
        third_specs = [row_spec(BRANCH), kv_spec, kv_spec, sample_spec, cache_spec, cache_spec,
                       _const_spec((2 * MEM_HEAD_DIM, 2 * MEM_HEAD_DIM))]
    else:
        third = (third,)
        third_specs = [row_spec(BRANCH)]
    in_specs = [row_spec(D_MODEL), row_spec(BRANCH), row_spec(BRANCH)] + third_specs + [
        _layer_spec((1, D_MODEL), layer),
        _const_spec((D_MODEL, N_BRANCH * D_MODEL)),
        _const_spec((N_BRANCH * BRANCH, D_MODEL)),
        _const_spec((D_MODEL, D_MODEL)),
        _layer_spec((1, D_MODEL), layer),
        _layer_spec((1, D_MODEL), layer),
        _const_spec((D_MODEL, 2 * D_FF)),
        _const_spec((D_FF, D_MODEL)),
        _layer_spec((1, D_MODEL), layer),
    ]
    out_specs = row_spec(D_MODEL)
    out_shape = jax.ShapeDtypeStruct((n_rows, D_MODEL), F32)
    if with_attention:
        out_specs = (out_specs, sample_spec)
        out_shape = (out_shape, jax.ShapeDtypeStruct((DEC_BATCH, SUBLANES, MEM_HEAD_DIM), F32))
    return pl.pallas_call(
        functools.partial(_merge_ffn_kernel, with_attention),
        grid=(n_rows // rows,),
        in_specs=in_specs, out_specs=out_specs, out_shape=out_shape,
        compiler_params=pltpu.CompilerParams(
            dimension_semantics=("parallel",), vmem_limit_bytes=VMEM_LIMIT_MERGE),
        name="merge_ffn",
    )(x, o_ssm, o_pool, *third, p['g_mix_pre'], w['w_gate'], w['w_up'], w['w_out'],
      p['g_mix_post'], p['g_ffn_pre'], w['w_ffn_in'], w['w_ffn_out'], p['g_ffn_post'])


def _sample_mixer_kernel(x_ref, g_ref, win_ref, bblk_ref, cre_ref, cimn_ref, ar_ref, ai_ref,
                         d_ref, wglu_ref, bglu_ref, poolw_ref, pscale_ref,
                         h0re_ref, h0im_ref, hist_ref,
                         q_ref, ossm_ref, opool_ref, hre_ref, him_ref, nhist_ref,
                         bu_re_ref, bu_im_ref):
    h = _rms(x_ref[...], g_ref[...]).astype(BF16)
    proj = _dot(h, win_ref[...].astype(BF16))
    u_ssm = proj[:, :BRANCH]
    u_pool = proj[:, BRANCH:2 * BRANCH]
    q_ref[...] = proj[:, 2 * BRANCH:]

    ys = []
    for k in range(N_SSM_BLOCKS):
        sl = slice(k * BLOCK_STATE, (k + 1) * BLOCK_STATE)
        _ssm_input(u_ssm[:, k * LANES:(k + 1) * LANES], bblk_ref[k], bu_re_ref, bu_im_ref, sl)
        a_r = ar_ref[:, sl]
        a_i = ai_ref[:, sl]
        h0r = h0re_ref[:, sl]
        h0i = h0im_ref[:, sl]
        hre_ref[:, sl] = bu_re_ref[:, sl] + a_r * h0r - a_i * h0i
        him_ref[:, sl] = bu_im_ref[:, sl] + a_r * h0i + a_i * h0r
        ys.append(_ssm_readout(hre_ref, him_ref, sl, cre_ref[k], cimn_ref[k]))
    y = _ssm_gate(jnp.concatenate(ys, axis=-1), u_ssm, d_ref, wglu_ref, bglu_ref)
    ossm_ref[...] = y.astype(BF16)

    pooled = []
    for gi, w in enumerate(POOL_WINDOWS):
        sl = slice(gi * POOL_GROUP_WIDTH, (gi + 1) * POOL_GROUP_WIDTH)
        acc = u_pool[:, sl]
        for j in range(1, w):
            acc = acc + hist_ref[POOL_HIST - j, :, sl]
        cnt = float(min(PAST_LEN + 1, w))
        pooled.append(acc / cnt - u_pool[:, sl])
    opool_ref[...] = _pool_mix(pooled, poolw_ref, pscale_ref).astype(BF16)
    for j in range(POOL_HIST - 1):
        nhist_ref[j] = hist_ref[j + 1]
    nhist_ref[POOL_HIST - 1] = u_pool


def _sample_mixer(layer, x_s, p, w_in, h0_re, h0_im, hist_t):
    n = x_s.shape[0]
    full = lambda shape: pl.BlockSpec(shape, lambda i: (0,) * len(shape))
    in_specs = [
        full((n, D_MODEL)),
        _layer_spec((1, D_MODEL), layer),
        _layer_spec((D_MODEL, 3 * BRANCH), layer),
        _layer_spec((N_SSM_BLOCKS, LANES, 2 * BLOCK_STATE), layer),
        _layer_spec((N_SSM_BLOCKS, BLOCK_STATE, LANES), layer),
        _layer_spec((N_SSM_BLOCKS, BLOCK_STATE, LANES), layer),
        _layer_spec((1, N_STATE), layer),
        _layer_spec((1, N_STATE), layer),
        _layer_spec((1, BRANCH), layer),
        _layer_spec((BRANCH, BRANCH), layer),
        _layer_spec((1, BRANCH), layer),
        _layer_spec((len(POOL_WINDOWS), POOL_GROUP_WIDTH, POOL_GROUP_WIDTH), layer),
        _layer_spec((1, BRANCH), layer),
        _layer_spec((n, N_STATE), layer),
        _layer_spec((n, N_STATE), layer),
        _layer_spec((POOL_HIST, n, BRANCH), layer),
    ]
    out_shape = (
        jax.ShapeDtypeStruct((n, BRANCH), F32),
        jax.ShapeDtypeStruct((n, BRANCH), BF16),
        jax.ShapeDtypeStruct((n, BRANCH), BF16),
        jax.ShapeDtypeStruct((n, N_STATE), F32),
        jax.ShapeDtypeStruct((n, N_STATE), F32),
        jax.ShapeDtypeStruct((POOL_HIST, n, BRANCH), F32),
    )
    out_specs = tuple(full(s.shape) for s in out_shape)
    return pl.pallas_call(
        _sample_mixer_kernel,
        grid=(1,),
        in_specs=in_specs, out_specs=out_specs, out_shape=out_shape,
        scratch_shapes=[pltpu.VMEM((n, N_STATE), F32), pltpu.VMEM((n, N_STATE), F32)],
        compiler_params=pltpu.CompilerParams(
            dimension_semantics=("arbitrary",), vmem_limit_bytes=VMEM_LIMIT),
        name="sample_mixer",
    )(x_s, p['g_mix_pre'], w_in, p['bblk'], p['cre'], p['cimn'], p['ar'], p['ai'],
      p['ssm_d'], p['w_glu'], p['b_glu'], p['pool_w'], p['pool_scale'], h0_re, h0_im, hist_t)


def kernel(x_prompt, x_sample, mem_prompt, cache_mem_k, cache_mem_v, state_ssm_re, state_ssm_im, state_pool, g_mix_pre, g_mix_post, g_ffn_pre, g_ffn_post, g_mem, w_in, w_kv, ssm_lam_re, ssm_lam_im, ssm_log_dt, ssm_b_re, ssm_b_im, ssm_c_re, ssm_c_im, ssm_d, ssm_w_glu, ssm_b_glu, pool_w, pool_scale, w_branch_up, w_out, w_ffn_in, w_ffn_out):
    ar, ai, bblk = _discretise(ssm_lam_re, ssm_lam_im, ssm_log_dt, ssm_b_re, ssm_b_im)
    vec = lambda a: a.reshape(DEPTH, 1, a.shape[-1])
    p = {
        'g_mix_pre': vec(g_mix_pre), 'g_mix_post': vec(g_mix_post),
        'g_ffn_pre': vec(g_ffn_pre), 'g_ffn_post': vec(g_ffn_post),
        'bblk': bblk, 'ar': ar, 'ai': ai,
        'cre': _c_blocks(ssm_c_re), 'cimn': _c_blocks(-ssm_c_im),
        'ssm_d': vec(ssm_d), 'w_glu': ssm_w_glu.astype(BF16), 'b_glu': vec(ssm_b_glu),
        'pool_w': pool_w.astype(BF16), 'pool_scale': vec(pool_scale),
    }
    raw = {
        'w_in': w_in, 'w_up': w_branch_up.reshape(DEPTH, N_BRANCH * BRANCH, D_MODEL),
        'w_out': w_out, 'w_ffn_in': w_ffn_in, 'w_ffn_out': w_ffn_out,
    }
    k_mem, v_mem = _mem_kv(mem_prompt, vec(g_mem), w_kv)

    lane_seq = jnp.arange(2 * MEM_HEAD_DIM, dtype=jnp.int32) // MEM_HEAD_DIM
    pair_ones = (lane_seq[:, None] == lane_seq[None, :]).astype(BF16)
    k_cache = cache_mem_k.reshape(DEPTH, DEC_BATCH, KV_ROWS, MEM_HEAD_DIM)
    v_cache = cache_mem_v.reshape(DEPTH, DEC_BATCH, KV_ROWS, MEM_HEAD_DIM)
    h0_re = state_ssm_re.reshape(DEPTH, DEC_BATCH, N_STATE)
    h0_im = state_ssm_im.reshape(DEPTH, DEC_BATCH, N_STATE)
    hist_t = jnp.swapaxes(state_pool, 1, 2)

    xp = x_prompt
    xs = x_sample.reshape(DEC_BATCH, D_MODEL)
    n_prompt = BATCH * SEQ
    re_p, im_p, pool_p, re_s, im_s, pool_s = [], [], [], [], [], []
    for layer in range(DEPTH):
        qs, os_ssm, os_pool, hre_s, him_s, nhist = _sample_mixer(layer, xs, p, w_in, h0_re, h0_im,
                                                                 hist_t)
        qs4 = qs.reshape(DEC_BATCH, MEM_HEADS, MEM_HEAD_DIM)
        sample = (jnp.concatenate([qs4, qs4], axis=1), k_cache, v_cache, pair_ones)
        (q, o_ssm, o_pool, hre, him, hist), w_bf = _mixer_a(layer, xp, p, raw)
        xp, os_mem = _merge_ffn(layer, xp.reshape(n_prompt, D_MODEL),
                                o_ssm.reshape(n_prompt, BRANCH), o_pool.reshape(n_prompt, BRANCH),
                                (q.reshape(n_prompt, BRANCH), k_mem, v_mem) + sample, p, w_bf)
        xp = xp.reshape(BATCH, SEQ, D_MODEL)
        re_p.append(hre.reshape(BATCH, SSM_GROUPS, SSM_STATE))
        im_p.append(him.reshape(BATCH, SSM_GROUPS, SSM_STATE))
        pool_p.append(jnp.swapaxes(hist.reshape(POOL_HIST, BATCH, BRANCH), 0, 1))

        os_mem = os_mem[:, :MEM_HEADS].reshape(DEC_BATCH, BRANCH).astype(BF16)
        xs = _merge_ffn(layer, xs, os_ssm, os_pool, os_mem, p, w_bf)
        re_s.append(hre_s.reshape(DEC_BATCH, SSM_GROUPS, SSM_STATE))
        im_s.append(him_s.reshape(DEC_BATCH, SSM_GROUPS, SSM_STATE))
        pool_s.append(jnp.swapaxes(nhist, 0, 1))

    y_prompt = xp
    y_sample = xs.reshape(DEC_BATCH, 1, D_MODEL)
    kv_shape = (DEPTH, BATCH, N_MEM, MEM_HEADS, MEM_HEAD_DIM)
    return (y_prompt, y_sample,
            jnp.stack(re_p), jnp.stack(im_p), jnp.stack(pool_p),
            k_mem.reshape(kv_shape), v_mem.reshape(kv_shape),
            jnp.stack(re_s), jnp.stack(im_s), jnp.stack(pool_s))
```

```python
import jax
import jax.numpy as jnp
from jax import lax
from jax.experimental import pallas as pl
from jax.experimental.pallas import tpu as pltpu

D_MODEL = 1024
BATCH = 8
SEQ = 2048
DEPTH = 4
DEC_BATCH = 128
PAST_LEN = 16384

BRANCH = D_MODEL // 2
SSM_GROUP = 16
SSM_GROUPS = BRANCH // SSM_GROUP
SSM_STATE = 64
N_STATE = SSM_GROUPS * SSM_STATE
POOL_WINDOWS = (2, 4, 8, 16)
POOL_GROUP_WIDTH = BRANCH // len(POOL_WINDOWS)
POOL_HIST = max(POOL_WINDOWS) - 1
N_MEM = 256
MEM_HEADS = 4
MEM_HEAD_DIM = BRANCH // MEM_HEADS
KV_ROWS = N_MEM * MEM_HEADS
N_BRANCH = 3
GATE_HALF = N_BRANCH * D_MODEL // 2
D_FF = 2816
RMS_EPS = 1e-6

LANES = 128
SUBLANES = 8
GROUPS_PER_BLOCK = LANES // SSM_GROUP
N_SSM_BLOCKS = BRANCH // LANES
BLOCK_STATE = GROUPS_PER_BLOCK * SSM_STATE

ROWS_A = 512
STEPS_A = ROWS_A // BATCH
ROWS_C = 512
MERGE_COLS = 256
TAIL_GROUPS = 2
PIECE_ROWS = 128
SAMPLES_PER_STEP = DEC_BATCH // (BATCH * SEQ // ROWS_C)
FF_CHUNKS = ((0, 1536), (1536, 2816))
VMEM_LIMIT = 56 * 1024 * 1024
VMEM_LIMIT_MERGE = 60 * 1024 * 1024

BF16 = jnp.bfloat16
F32 = jnp.float32


def _rms(x, g):
    ms = jnp.mean(x * x, axis=-1, keepdims=True)
    return x * lax.rsqrt(ms + RMS_EPS) * g


def _dot(a, b):
    return jnp.dot(a, b, preferred_element_type=F32)


def _const_spec(shape):
    nd = len(shape)
    return pl.BlockSpec(shape, lambda *_: (0,) * nd, pipeline_mode=pl.Buffered(1))


def _layer_spec(shape, layer):
    nd = len(shape)
    return pl.BlockSpec((None,) + tuple(shape), lambda *_: (layer,) + (0,) * nd,
                        pipeline_mode=pl.Buffered(1))


def _discretise_kernel(lr_ref, li_ref, ldt_ref, br_ref, bi_ref,
                       ar_ref, ai_ref, bbr_ref, bbi_ref):
    lr = lr_ref[...]
    li = li_ref[...]
    dt = jnp.exp(ldt_ref[...])
    zr = lr * dt
    zi = li * dt
    mag = jnp.exp(zr)
    ar = mag * jnp.cos(zi)
    ai = mag * jnp.sin(zi)
    den = lr * lr + li * li
    fr = ((ar - 1.0) * lr + ai * li) / den
    fi = (ai * lr - (ar - 1.0) * li) / den
    br = br_ref[...]
    bi = bi_ref[...]
    ar_ref[...] = ar
    ai_ref[...] = ai
    bbr_ref[...] = fr * br - fi * bi
    bbi_ref[...] = fr * bi + fi * br


def _discretise(lam_re, lam_im, log_dt, b_re, b_im):
    rows = DEPTH * SSM_GROUPS
    width = SSM_GROUP * SSM_STATE

    def tile_p(a):
        return jnp.tile(a.reshape(rows, 1, SSM_STATE), (1, SSM_GROUP, 1)).reshape(rows, width)

    def b_t(a):
        return jnp.swapaxes(a, -1, -2).reshape(rows, width)

    ldt = jnp.broadcast_to(log_dt.reshape(rows, 1), (rows, width))
    out = jax.ShapeDtypeStruct((rows, width), F32)
    ar, ai, bbr, bbi = pl.pallas_call(
        _discretise_kernel, out_shape=(out, out, out, out), name="s5_discretise",
    )(tile_p(lam_re), tile_p(lam_im), ldt, b_t(b_re), b_t(b_im))
    ar = ar[:, :SSM_STATE].reshape(DEPTH, 1, N_STATE)
    ai = ai[:, :SSM_STATE].reshape(DEPTH, 1, N_STATE)
    shape5 = (DEPTH, N_SSM_BLOCKS, GROUPS_PER_BLOCK, SSM_GROUP, SSM_STATE)
    bblk = jnp.concatenate([_group_block_diag(bbr.reshape(shape5)),
                            _group_block_diag(bbi.reshape(shape5))], axis=-1)
    return ar, ai, bblk


def _group_block_diag(m):
    depth, blocks, groups, n_r, n_c = m.shape
    same = jnp.eye(groups, dtype=bool)[None, None, :, None, :, None]
    out = jnp.where(same, m[:, :, :, :, None, :], 0.0).astype(BF16)
    return out.reshape(depth, blocks, groups * n_r, groups * n_c)


def _c_blocks(c):
    shape5 = (DEPTH, N_SSM_BLOCKS, GROUPS_PER_BLOCK, SSM_GROUP, SSM_STATE)
    return _group_block_diag(jnp.swapaxes(c.reshape(shape5), -1, -2))


def _ssm_input(u_block, bblk, bu_re_ref, bu_im_ref, sl):
    bu = _dot(u_block.astype(BF16), bblk)
    bu_re_ref[:, sl] = bu[:, :BLOCK_STATE]
    bu_im_ref[:, sl] = bu[:, BLOCK_STATE:]


def _ssm_readout(h_re_ref, h_im_ref, sl, cre, cimn):
    return _dot(h_re_ref[:, sl].astype(BF16), cre) + _dot(h_im_ref[:, sl].astype(BF16), cimn)


def _ssm_gate(y, u_ssm, d_ref, wglu_ref, bglu_ref):
    y = jax.nn.gelu(y + d_ref[...] * u_ssm)
    return y * jax.nn.sigmoid(_dot(y.astype(BF16), wglu_ref[...]) + bglu_ref[...])


def _pool_mix(pooled, poolw_ref, pscale_ref):
    outs = []
    for gi in range(len(POOL_WINDOWS)):
        sl = slice(gi * POOL_GROUP_WIDTH, (gi + 1) * POOL_GROUP_WIDTH)
        outs.append(_dot(pooled[gi].astype(BF16), poolw_ref[gi]) * pscale_ref[:, sl])
    return jnp.concatenate(outs, axis=-1)


def _sample_pair_stages(sq_ref, sk_ref, sv_ref, ones_ref, so_all, first_row, first):
    tiles = KV_ROWS // SUBLANES
    width = 2 * MEM_HEAD_DIM
    pair = (first, first + 1)
    q8 = jnp.concatenate([sq_ref[n] for n in pair], axis=-1) * MEM_HEAD_DIM ** -0.5
    k = jnp.concatenate([sk_ref[n] for n in pair], axis=-1)
    prod = (k.reshape(tiles, SUBLANES, width) * q8[None]).reshape(KV_ROWS, width).astype(BF16)
    s = _dot(prod, ones_ref[...]).reshape(tiles, SUBLANES, width)
    yield
    v = jnp.concatenate([sv_ref[n] for n in pair], axis=-1).reshape(tiles, SUBLANES, width)
    m8 = jnp.max(s, axis=0)
    m8 = jnp.maximum(m8, pltpu.roll(m8, MEM_HEADS, 0))
    e = jnp.exp(s - m8[None])
    l8 = jnp.sum(e, axis=0)
    acc = jnp.sum(e * v, axis=0)
    l8 = l8 + pltpu.roll(l8, MEM_HEADS, 0)
    acc = acc + pltpu.roll(acc, MEM_HEADS, 0)
    out = acc / l8
    for j, n in enumerate(pair):
        rows = pl.ds(pl.multiple_of(first_row + n * SUBLANES, SUBLANES), SUBLANES)
        so_all[rows, :] = out[:, j * MEM_HEAD_DIM:(j + 1) * MEM_HEAD_DIM]


def _mixer_a_kernel(xfirst_ref, xnext_ref, g_ref, win_ref, bblk_ref, cre_ref, cimn_ref,
                    ar_ref, ai_ref, d_ref, wglu_ref, bglu_ref, poolw_ref, pscale_ref,
                    wgate_lo_ref, wgate_hi_ref, wup_ref, wout_ref, wfin_ref, wfout_ref,
                    q_ref, ossm_ref, opool_ref, hre_ref, him_ref, hist_ref,
                    wgate_bf_ref, wup_bf_ref, wout_bf_ref, wfin_bf_ref, wfout_bf_ref,
                    bu_re_ref, bu_im_ref, pool_buf, ussm_tm, ossm_tm, opool_tm, win_bf,
                    proj_scr, hnext_scr):
    i = pl.program_id(0)
    hist_rows = POOL_HIST * BATCH

    wgate_bf_ref[:, :GATE_HALF] = wgate_lo_ref[...].astype(BF16)
    wgate_bf_ref[:, GATE_HALF:] = wgate_hi_ref[...].astype(BF16)
    wup_bf_ref[...] = wup_ref[...].astype(BF16)
    wout_bf_ref[...] = wout_ref[...].astype(BF16)
    wfin_bf_ref[...] = wfin_ref[...].astype(BF16)
    wfout_bf_ref[...] = wfout_ref[...].astype(BF16)

    def normed(x_blk_ref):
        return _rms(x_blk_ref[...].reshape(ROWS_A, D_MODEL), g_ref[...]).astype(BF16)

    @pl.when(i == 0)
    def _():
        hre_ref[...] = jnp.zeros_like(hre_ref)
        him_ref[...] = jnp.zeros_like(him_ref)
        pool_buf[:, 0:hist_rows, :] = jnp.zeros((N_SSM_BLOCKS, hist_rows, LANES), F32)
        win_bf[...] = win_ref[...].astype(BF16)
        proj_scr[...] = _dot(normed(xfirst_ref), win_bf[...])

    for b in range(BATCH):
        rows = slice(b * STEPS_A, (b + 1) * STEPS_A)
        q_ref[b] = proj_scr[rows, 2 * BRANCH:].astype(BF16)
        for k in range(N_SSM_BLOCKS):
            ussm_tm[k, pl.ds(b, STEPS_A, stride=BATCH), :] = (
                proj_scr[rows, k * LANES:(k + 1) * LANES])
            pool_buf[k, pl.ds(hist_rows + b, STEPS_A, stride=BATCH), :] = (
                proj_scr[rows, BRANCH + k * LANES:BRANCH + (k + 1) * LANES])
    u_ssm = jnp.concatenate([ussm_tm[k] for k in range(N_SSM_BLOCKS)], axis=-1)
    hnext_scr[...] = normed(xnext_ref)
    next_cols = [slice(c, c + MERGE_COLS) for c in range(0, 3 * BRANCH, MERGE_COLS)]

    def project_next_chunk():
        if next_cols:
            cols = next_cols.pop(0)
            proj_scr[:, cols] = _dot(hnext_scr[...], win_bf[:, cols])

    ys = []
    for k in range(N_SSM_BLOCKS):
        sl = slice(k * BLOCK_STATE, (k + 1) * BLOCK_STATE)
        _ssm_input(ussm_tm[k], bblk_ref[k], bu_re_ref, bu_im_ref, sl)
        project_next_chunk()
        a_r = jnp.broadcast_to(ar_ref[:, sl], (BATCH, BLOCK_STATE))
        a_i = jnp.broadcast_to(ai_ref[:, sl], (BATCH, BLOCK_STATE))
        hr = hre_ref[:, sl]
        hi = him_ref[:, sl]
        for t in range(STEPS_A):
            rows = slice(t * BATCH, (t + 1) * BATCH)
            hr, hi = (a_r * hr - a_i * hi + bu_re_ref[rows, sl],
                      a_r * hi + a_i * hr + bu_im_ref[rows, sl])
            bu_re_ref[rows, sl] = hr
            bu_im_ref[rows, sl] = hi
        hre_ref[:, sl] = hr
        him_ref[:, sl] = hi
        ys.append(_ssm_readout(bu_re_ref, bu_im_ref, sl, cre_ref[k], cimn_ref[k]))
    project_next_chunk()
    o_ssm = _ssm_gate(jnp.concatenate(ys, axis=-1), u_ssm, d_ref, wglu_ref, bglu_ref)

    project_next_chunk()
    t_pos = i * STEPS_A + lax.broadcasted_iota(jnp.int32, (ROWS_A, POOL_GROUP_WIDTH), 0) // BATCH
    pooled = []
    for gi, w in enumerate(POOL_WINDOWS):
        u_g = pool_buf[gi, hist_rows:hist_rows + ROWS_A, :]
        acc = u_g
        for j in range(1, w):
            start = hist_rows - j * BATCH
            acc = acc + pool_buf[gi, start:start + ROWS_A, :]
        cnt = jnp.minimum(t_pos + 1, w).astype(F32)
        pooled.append(acc / cnt - u_g)
    o_pool = _pool_mix(pooled, poolw_ref, pscale_ref)
    pool_buf[:, 0:hist_rows, :] = pool_buf[:, ROWS_A:ROWS_A + hist_rows, :]

    for k in range(N_SSM_BLOCKS):
        ossm_tm[k] = o_ssm[:, k * LANES:(k + 1) * LANES]
        opool_tm[k] = o_pool[:, k * LANES:(k + 1) * LANES]
    for b in range(BATCH):
        rows = pl.ds(b, STEPS_A, stride=BATCH)
        ossm_ref[b] = jnp.concatenate(
            [ossm_tm[k, rows, :] for k in range(N_SSM_BLOCKS)], axis=-1).astype(BF16)
        opool_ref[b] = jnp.concatenate(
            [opool_tm[k, rows, :] for k in range(N_SSM_BLOCKS)], axis=-1).astype(BF16)

    @pl.when(i == pl.num_programs(0) - 1)
    def _():
        for k in range(N_SSM_BLOCKS):
            hist_ref[:, k * LANES:(k + 1) * LANES] = pool_buf[k, 0:hist_rows, :]


def _mixer_a(layer, x, p, raw):
    hist_rows = POOL_HIST * BATCH
    n_steps = SEQ // STEPS_A
    seq_spec = lambda w: pl.BlockSpec((BATCH, STEPS_A, w), lambda i: (0, i, 0))
    cast_plan = {
        'w_gate': (D_MODEL // n_steps, N_BRANCH * D_MODEL, 1),
        'w_up': (N_BRANCH * BRANCH // n_steps, D_MODEL, 1),
        'w_out': (D_MODEL // n_steps, D_MODEL, 1),
        'w_ffn_in': (D_MODEL // n_steps, 2 * D_FF, 1),
        'w_ffn_out': (2 * D_FF // n_steps, D_MODEL, 2),
    }

    def chunk_in(name, width=None, col=0):
        rows, full_width, per = cast_plan[name]
        return pl.BlockSpec((None, rows, width or full_width),
                            lambda i: (layer, i // per, col))

    def chunk_out(name):
        rows, width, per = cast_plan[name]
        return pl.BlockSpec((rows, width), lambda i: (i // per, 0))

    cast_names = ('w_gate', 'w_up', 'w_out', 'w_ffn_in', 'w_ffn_out')
    cast_shapes = tuple(
        jax.ShapeDtypeStruct((cast_plan[n][0] * n_steps // cast_plan[n][2], cast_plan[n][1]), BF16)
        for n in cast_names)
    in_specs = [
        pl.BlockSpec((BATCH, STEPS_A, D_MODEL), lambda i: (0, 0, 0), pipeline_mode=pl.Buffered(1)),
        pl.BlockSpec((BATCH, STEPS_A, D_MODEL), lambda i: (0, jnp.minimum(i + 1, n_steps - 1), 0)),
        _layer_spec((1, D_MODEL), layer),
        _layer_spec((D_MODEL, 3 * BRANCH), layer),
        _layer_spec((N_SSM_BLOCKS, LANES, 2 * BLOCK_STATE), layer),
        _layer_spec((N_SSM_BLOCKS, BLOCK_STATE, LANES), layer),
        _layer_spec((N_SSM_BLOCKS, BLOCK_STATE, LANES), layer),
        _layer_spec((1, N_STATE), layer),
        _layer_spec((1, N_STATE), layer),
        _layer_spec((1, BRANCH), layer),
        _layer_spec((BRANCH, BRANCH), layer),
        _layer_spec((1, BRANCH), layer),
        _layer_spec((len(POOL_WINDOWS), POOL_GROUP_WIDTH, POOL_GROUP_WIDTH), layer),
        _layer_spec((1, BRANCH), layer),
        chunk_in('w_gate', GATE_HALF, 1), chunk_in('w_gate', GATE_HALF, 2),
        chunk_in('w_up'), chunk_in('w_out'), chunk_in('w_ffn_in'), chunk_in('w_ffn_out'),
    ]
    branch_out = jax.ShapeDtypeStruct((BATCH, SEQ, BRANCH), BF16)
    out_shape = (
        branch_out, branch_out, branch_out,
        jax.ShapeDtypeStruct((BATCH, N_STATE), F32),
        jax.ShapeDtypeStruct((BATCH, N_STATE), F32),
        jax.ShapeDtypeStruct((hist_rows, BRANCH), F32),
    ) + cast_shapes
    out_specs = (
        seq_spec(BRANCH), seq_spec(BRANCH), seq_spec(BRANCH),
        pl.BlockSpec((BATCH, N_STATE), lambda i: (0, 0)),
        pl.BlockSpec((BATCH, N_STATE), lambda i: (0, 0)),
        pl.BlockSpec((hist_rows, BRANCH), lambda i: (0, 0)),
    ) + tuple(chunk_out(n) for n in cast_names)
    outs = pl.pallas_call(
        _mixer_a_kernel,
        grid=(n_steps,),
        in_specs=in_specs, out_specs=out_specs, out_shape=out_shape,
        scratch_shapes=[
            pltpu.VMEM((ROWS_A, N_STATE), F32),
            pltpu.VMEM((ROWS_A, N_STATE), F32),
            pltpu.VMEM((N_SSM_BLOCKS, hist_rows + ROWS_A, LANES), F32),
            pltpu.VMEM((N_SSM_BLOCKS, ROWS_A, LANES), F32),
            pltpu.VMEM((N_SSM_BLOCKS, ROWS_A, LANES), F32),
            pltpu.VMEM((N_SSM_BLOCKS, ROWS_A, LANES), F32),
            pltpu.VMEM((D_MODEL, 3 * BRANCH), BF16),
            pltpu.VMEM((ROWS_A, 3 * BRANCH), F32),
            pltpu.VMEM((ROWS_A, D_MODEL), BF16),
        ],
        compiler_params=pltpu.CompilerParams(
            dimension_semantics=("arbitrary",), vmem_limit_bytes=VMEM_LIMIT),
        name="prompt_mixer_a",
    )(x, x, p['g_mix_pre'], raw['w_in'], p['bblk'], p['cre'], p['cimn'], p['ar'], p['ai'],
      p['ssm_d'], p['w_glu'], p['b_glu'], p['pool_w'], p['pool_scale'],
      raw['w_in'], raw['w_in'], raw['w_up'], raw['w_out'], raw['w_ffn_in'], raw['w_ffn_out'])
    return outs[:6], dict(zip(cast_names, outs[6:]))


def _mem_kv_kernel(mem_ref, g_ref, wkv_ref, k_ref, v_ref):
    wkv = wkv_ref[...].astype(BF16)
    for b in range(BATCH):
        kv = _dot(_rms(mem_ref[b], g_ref[...]).astype(BF16), wkv)
        for hd in range(MEM_HEADS):
            rows = pl.ds(hd, N_MEM, stride=MEM_HEADS)
            k_ref[b, rows, :] = kv[:, hd * MEM_HEAD_DIM:(hd + 1) * MEM_HEAD_DIM]
            v_ref[b, rows, :] = kv[:, BRANCH + hd * MEM_HEAD_DIM:BRANCH + (hd + 1) * MEM_HEAD_DIM]


def _mem_kv(mem, g_mem, w_kv):
    out = jax.ShapeDtypeStruct((DEPTH, BATCH, KV_ROWS, MEM_HEAD_DIM), F32)
    out_spec = pl.BlockSpec((None, BATCH, KV_ROWS, MEM_HEAD_DIM), lambda l: (l, 0, 0, 0))
    return pl.pallas_call(
        _mem_kv_kernel,
        grid=(DEPTH,),
        in_specs=[
            _const_spec((BATCH, N_MEM, D_MODEL)),
            pl.BlockSpec((None, 1, D_MODEL), lambda l: (l, 0, 0)),
            pl.BlockSpec((None, D_MODEL, 2 * BRANCH), lambda l: (l, 0, 0)),
        ],
        out_specs=(out_spec, out_spec), out_shape=(out, out),
        compiler_params=pltpu.CompilerParams(
            dimension_semantics=("arbitrary",), vmem_limit_bytes=VMEM_LIMIT),
        name="mem_kv",
    )(mem, g_mem, w_kv)


def _memory_attention_stages(q_ref, k_ref, v_ref, result):
    scale = MEM_HEAD_DIM ** -0.5
    head_rows = [pl.ds(hd, N_MEM, stride=MEM_HEADS) for hd in range(MEM_HEADS)]
    head_cols = [slice(hd * MEM_HEAD_DIM, (hd + 1) * MEM_HEAD_DIM) for hd in range(MEM_HEADS)]
    scores = [lax.dot_general(q_ref[:, head_cols[hd]], k_ref[head_rows[hd], :].astype(BF16),
                              (((1,), (1,)), ((), ())), preferred_element_type=F32) * scale
              for hd in range(MEM_HEADS)]
    yield
    probs = []
    for s in scores:
        e = jnp.exp(s - jnp.max(s, axis=-1, keepdims=True))
        probs.append((e / jnp.sum(e, axis=-1, keepdims=True)).astype(BF16))
        yield
    outs = []
    for hd in range(MEM_HEADS):
        outs.append(_dot(probs[hd], v_ref[head_rows[hd], :].astype(BF16)).astype(BF16))
        yield
    result.append(jnp.concatenate(outs, axis=-1))


def _merge_rows(x, o_ssm, o_pool, attention, o_mem, side_jobs, store, weights):
    (gpre_ref, wgate_ref, wup_ref, wout_ref, gpost_ref, gfpre_ref, wfin_ref, wfout_ref,
     gfpost_ref) = weights
    h = _rms(x, gpre_ref[...]).astype(BF16)

    def branch_term(b, o_b, cols):
        gate_cols = slice(b * D_MODEL + cols.start, b * D_MODEL + cols.stop)
        gate = jax.nn.sigmoid(_dot(h, wgate_ref[:, gate_cols]))
        return gate * _dot(o_b, wup_ref[b * BRANCH:(b + 1) * BRANCH, cols])

    col_chunks = [slice(c, c + MERGE_COLS) for c in range(0, D_MODEL, MERGE_COLS)]
    next(attention, None)
    partial = []
    for b, o_b in enumerate((o_ssm, o_pool)):
        for ci, cols in enumerate(col_chunks):
            term = branch_term(b, o_b, cols)
            if b == 0:
                partial.append(term)
            else:
                partial[ci] = partial[ci] + term
            next(attention, None)
    for _ in attention:
        pass
    merged = jnp.concatenate(
        [partial[ci] + branch_term(N_BRANCH - 1, o_mem[0], cols)
         for ci, cols in enumerate(col_chunks)], axis=-1).astype(BF16)

    def tail_stages(rows):
        m2 = _dot(merged[rows, :], wout_ref[...])
        yield
        x1 = x[rows, :] + _rms(m2, gpost_ref[...])
        hf = _rms(x1, gfpre_ref[...]).astype(BF16)
        yield
        f = None
        for lo, hi in FF_CHUNKS:
            hg = _dot(hf, wfin_ref[:, lo:hi])
            hu = _dot(hf, wfin_ref[:, D_FF + lo:D_FF + hi])
            yield
            act = (jax.nn.silu(hg) * hu).astype(BF16)
            yield
            part = _dot(act, wfout_ref[lo:hi, :])
            f = part if f is None else f + part
            yield
        store(rows, x1 + _rms(f, gfpost_ref[...]))

    n_groups = TAIL_GROUPS if x.shape[0] >= TAIL_GROUPS * PIECE_ROWS else 1
    group_rows = x.shape[0] // n_groups
    waiting = [tail_stages(slice(g * group_rows, (g + 1) * group_rows)) for g in range(n_groups)]
    waiting += side_jobs
    running = []
    while waiting or running:
        if waiting:
            running.append(waiting.pop(0))
        for chain in list(running):
            if next(chain, StopIteration) is StopIteration:
                running.remove(chain)


def _merge_ffn_kernel(x_ref, ossm_ref, opool_ref, q_ref, k_ref, v_ref,
                      sq_ref, sk_ref, sv_ref, ones_ref, xs_ref, sossm_ref, sopool_ref,
                      gpre_ref, wgate_ref, wup_ref, wout_ref, gpost_ref, gfpre_ref, wfin_ref,
                      wfout_ref, gfpost_ref, y_ref, ys_ref, so_all):
    i = pl.program_id(0)
    weights = (gpre_ref, wgate_ref, wup_ref, wout_ref, gpost_ref, gfpre_ref, wfin_ref,
               wfout_ref, gfpost_ref)
    o_mem = []
    attention = _memory_attention_stages(q_ref, k_ref, v_ref, o_mem)
    block_rows = SAMPLES_PER_STEP * SUBLANES
    first_row = pl.multiple_of(i * block_rows, block_rows)
    side_jobs = [_sample_pair_stages(sq_ref, sk_ref, sv_ref, ones_ref, so_all, first_row, first)
                 for first in range(0, SAMPLES_PER_STEP, 2)]

    def store_prompt(rows, y):
        y_ref[rows, :] = y

    _merge_rows(x_ref[...], ossm_ref[...], opool_ref[...], attention, o_mem, side_jobs,
                store_prompt, weights)

    @pl.when(i == pl.num_programs(0) - 1)
    def _():
        heads = [so_all[pl.ds(hd, DEC_BATCH, stride=SUBLANES), :] for hd in range(MEM_HEADS)]
        o_mem_sample = [jnp.concatenate(heads, axis=-1).astype(BF16)]

        def store_sample(rows, y):
            ys_ref[rows, :] = y

        _merge_rows(xs_ref[...], sossm_ref[...], sopool_ref[...], iter(()), o_mem_sample, [],
                    store_sample, weights)


def _merge_ffn(layer, x, o_ssm, o_pool, q, k_all, v_all, sample, p, w):
    n_rows = x.shape[0]
    n_sample = sample[4].shape[0]
    row_spec = lambda w: pl.BlockSpec((ROWS_C, w), lambda i: (i, 0))
    tiles_per_seq = SEQ // ROWS_C
    kv_spec = pl.BlockSpec((None, None, KV_ROWS, MEM_HEAD_DIM),
                           lambda i: (layer, i // tiles_per_seq, 0, 0))
    sample_spec = pl.BlockSpec((SAMPLES_PER_STEP, SUBLANES, MEM_HEAD_DIM), lambda i: (i, 0, 0))
    cache_spec = pl.BlockSpec((None, SAMPLES_PER_STEP, KV_ROWS, MEM_HEAD_DIM),
                              lambda i: (layer, i, 0, 0))
    in_specs = [
        row_spec(D_MODEL), row_spec(BRANCH), row_spec(BRANCH),
        row_spec(BRANCH), kv_spec, kv_spec,
        sample_spec, cache_spec, cache_spec, _const_spec((2 * MEM_HEAD_DIM, 2 * MEM_HEAD_DIM)),
        _const_spec((n_sample, D_MODEL)), _const_spec((n_sample, BRANCH)),
        _const_spec((n_sample, BRANCH)),
        _layer_spec((1, D_MODEL), layer),
        _const_spec((D_MODEL, N_BRANCH * D_MODEL)),
        _const_spec((N_BRANCH * BRANCH, D_MODEL)),
        _const_spec((D_MODEL, D_MODEL)),
        _layer_spec((1, D_MODEL), layer),
        _layer_spec((1, D_MODEL), layer),
        _const_spec((D_MODEL, 2 * D_FF)),
        _const_spec((D_FF, D_MODEL)),
        _layer_spec((1, D_MODEL), layer),
    ]
    return pl.pallas_call(
        _merge_ffn_kernel,
        grid=(n_rows // ROWS_C,),
        in_specs=in_specs,
        out_specs=(row_spec(D_MODEL), pl.BlockSpec((n_sample, D_MODEL), lambda i: (0, 0))),
        out_shape=(jax.ShapeDtypeStruct((n_rows, D_MODEL), F32),
                   jax.ShapeDtypeStruct((n_sample, D_MODEL), F32)),
        scratch_shapes=[pltpu.VMEM((n_sample * SUBLANES, MEM_HEAD_DIM), F32)],
        compiler_params=pltpu.CompilerParams(
            dimension_semantics=("arbitrary",), vmem_limit_bytes=VMEM_LIMIT_MERGE),
        name="merge_ffn",
    )(x, o_ssm, o_pool, q, k_all, v_all, *sample, p['g_mix_pre'], w['w_gate'], w['w_up'],
      w['w_out'], p['g_mix_post'], p['g_ffn_pre'], w['w_ffn_in'], w['w_ffn_out'], p['g_ffn_post'])


def _sample_mixer_kernel(x_ref, g_ref, win_ref, bblk_ref, cre_ref, cimn_ref, ar_ref, ai_ref,
                         d_ref, wglu_ref, bglu_ref, poolw_ref, pscale_ref,
                         h0re_ref, h0im_ref, hist_ref,
                         q_ref, ossm_ref, opool_ref, hre_ref, him_ref, nhist_ref,
                         bu_re_ref, bu_im_ref):
    h = _rms(x_ref[...], g_ref[...]).astype(BF16)
    proj = _dot(h, win_ref[...].astype(BF16))
    u_ssm = proj[:, :BRANCH]
    u_pool = proj[:, BRANCH:2 * BRANCH]
    q_ref[...] = proj[:, 2 * BRANCH:]

    ys = []
    for k in range(N_SSM_BLOCKS):
        sl = slice(k * BLOCK_STATE, (k + 1) * BLOCK_STATE)
        _ssm_input(u_ssm[:, k * LANES:(k + 1) * LANES], bblk_ref[k], bu_re_ref, bu_im_ref, sl)
        a_r = ar_ref[:, sl]
        a_i = ai_ref[:, sl]
        h0r = h0re_ref[:, sl]
        h0i = h0im_ref[:, sl]
        hre_ref[:, sl] = bu_re_ref[:, sl] + a_r * h0r - a_i * h0i
        him_ref[:, sl] = bu_im_ref[:, sl] + a_r * h0i + a_i * h0r
        ys.append(_ssm_readout(hre_ref, him_ref, sl, cre_ref[k], cimn_ref[k]))
    y = _ssm_gate(jnp.concatenate(ys, axis=-1), u_ssm, d_ref, wglu_ref, bglu_ref)
    ossm_ref[...] = y.astype(BF16)

    pooled = []
    for gi, w in enumerate(POOL_WINDOWS):
        sl = slice(gi * POOL_GROUP_WIDTH, (gi + 1) * POOL_GROUP_WIDTH)
        acc = u_pool[:, sl]
        for j in range(1, w):
            acc = acc + hist_ref[POOL_HIST - j, :, sl]
        cnt = float(min(PAST_LEN + 1, w))
        pooled.append(acc / cnt - u_pool[:, sl])
    opool_ref[...] = _pool_mix(pooled, poolw_ref, pscale_ref).astype(BF16)
    for j in range(POOL_HIST - 1):
        nhist_ref[j] = hist_ref[j + 1]
    nhist_ref[POOL_HIST - 1] = u_pool


def _sample_mixer(layer, x_s, p, w_in, h0_re, h0_im, hist_t):
    n = x_s.shape[0]
    full = lambda shape: pl.BlockSpec(shape, lambda i: (0,) * len(shape))
    in_specs = [
        full((n, D_MODEL)),
        _layer_spec((1, D_MODEL), layer),
        _layer_spec((D_MODEL, 3 * BRANCH), layer),
        _layer_spec((N_SSM_BLOCKS, LANES, 2 * BLOCK_STATE), layer),
        _layer_spec((N_SSM_BLOCKS, BLOCK_STATE, LANES), layer),
        _layer_spec((N_SSM_BLOCKS, BLOCK_STATE, LANES), layer),
        _layer_spec((1, N_STATE), layer),
        _layer_spec((1, N_STATE), layer),
        _layer_spec((1, BRANCH), layer),
        _layer_spec((BRANCH, BRANCH), layer),
        _layer_spec((1, BRANCH), layer),
        _layer_spec((len(POOL_WINDOWS), POOL_GROUP_WIDTH, POOL_GROUP_WIDTH), layer),
        _layer_spec((1, BRANCH), layer),
        _layer_spec((n, N_STATE), layer),
        _layer_spec((n, N_STATE), layer),
        _layer_spec((POOL_HIST, n, BRANCH), layer),
    ]
    out_shape = (
        jax.ShapeDtypeStruct((n, BRANCH), F32),
        jax.ShapeDtypeStruct((n, BRANCH), BF16),
        jax.ShapeDtypeStruct((n, BRANCH), BF16),
        jax.ShapeDtypeStruct((n, N_STATE), F32),
        jax.ShapeDtypeStruct((n, N_STATE), F32),
        jax.ShapeDtypeStruct((POOL_HIST, n, BRANCH), F32),
    )
    out_specs = tuple(full(s.shape) for s in out_shape)
    return pl.pallas_call(
        _sample_mixer_kernel,
        grid=(1,),
        in_specs=in_specs, out_specs=out_specs, out_shape=out_shape,
        scratch_shapes=[pltpu.VMEM((n, N_STATE), F32), pltpu.VMEM((n, N_STATE), F32)],
        compiler_params=pltpu.CompilerParams(
            dimension_semantics=("arbitrary",), vmem_limit_bytes=VMEM_LIMIT),
        name="sample_mixer",
    )(x_s, p['g_mix_pre'], w_in, p['bblk'], p['cre'], p['cimn'], p['ar'], p['ai'],
      p['ssm_d'], p['w_glu'], p['b_glu'], p['pool_w'], p['pool_scale'], h0_re, h0_im, hist_t)


def kernel(x_prompt, x_sample, mem_prompt, cache_mem_k, cache_mem_v, state_ssm_re, state_ssm_im, state_pool, g_mix_pre, g_mix_post, g_ffn_pre, g_ffn_post, g_mem, w_in, w_kv, ssm_lam_re, ssm_lam_im, ssm_log_dt, ssm_b_re, ssm_b_im, ssm_c_re, ssm_c_im, ssm_d, ssm_w_glu, ssm_b_glu, pool_w, pool_scale, w_branch_up, w_out, w_ffn_in, w_ffn_out):
    ar, ai, bblk = _discretise(ssm_lam_re, ssm_lam_im, ssm_log_dt, ssm_b_re, ssm_b_im)
    vec = lambda a: a.reshape(DEPTH, 1, a.shape[-1])
    p = {
        'g_mix_pre': vec(g_mix_pre), 'g_mix_post': vec(g_mix_post),
        'g_ffn_pre': vec(g_ffn_pre), 'g_ffn_post': vec(g_ffn_post),
        'bblk': bblk, 'ar': ar, 'ai': ai,
        'cre': _c_blocks(ssm_c_re), 'cimn': _c_blocks(-ssm_c_im),
        'ssm_d': vec(ssm_d), 'w_glu': ssm_w_glu.astype(BF16), 'b_glu': vec(ssm_b_glu),
        'pool_w': pool_w.astype(BF16), 'pool_scale': vec(pool_scale),
    }
    raw = {
        'w_in': w_in, 'w_up': w_branch_up.reshape(DEPTH, N_BRANCH * BRANCH, D_MODEL),
        'w_out': w_out, 'w_ffn_in': w_ffn_in, 'w_ffn_out': w_ffn_out,
    }
    k_mem, v_mem = _mem_kv(mem_prompt, vec(g_mem), w_kv)

    lane_seq = jnp.arange(2 * MEM_HEAD_DIM, dtype=jnp.int32) // MEM_HEAD_DIM
    pair_ones = (lane_seq[:, None] == lane_seq[None, :]).astype(BF16)
    k_cache = cache_mem_k.reshape(DEPTH, DEC_BATCH, KV_ROWS, MEM_HEAD_DIM)
    v_cache = cache_mem_v.reshape(DEPTH, DEC_BATCH, KV_ROWS, MEM_HEAD_DIM)
    h0_re = state_ssm_re.reshape(DEPTH, DEC_BATCH, N_STATE)
    h0_im = state_ssm_im.reshape(DEPTH, DEC_BATCH, N_STATE)
    hist_t = jnp.swapaxes(state_pool, 1, 2)

    xp = x_prompt
    xs = x_sample.reshape(DEC_BATCH, D_MODEL)
    n_prompt = BATCH * SEQ
    re_p, im_p, pool_p, re_s, im_s, pool_s = [], [], [], [], [], []
    for layer in range(DEPTH):
        qs, os_ssm, os_pool, hre_s, him_s, nhist = _sample_mixer(layer, xs, p, w_in, h0_re, h0_im,
                                                                 hist_t)
        qs4 = qs.reshape(DEC_BATCH, MEM_HEADS, MEM_HEAD_DIM)
        sample = (jnp.concatenate([qs4, qs4], axis=1), k_cache, v_cache, pair_ones,
                  xs, os_ssm, os_pool)
        (q, o_ssm, o_pool, hre, him, hist), w_bf = _mixer_a(layer, xp, p, raw)
        xp, xs = _merge_ffn(layer, xp.reshape(n_prompt, D_MODEL),
                            o_ssm.reshape(n_prompt, BRANCH), o_pool.reshape(n_prompt, BRANCH),
                            q.reshape(n_prompt, BRANCH), k_mem, v_mem, sample, p, w_bf)
        xp = xp.reshape(BATCH, SEQ, D_MODEL)
        re_p.append(hre.reshape(BATCH, SSM_GROUPS, SSM_STATE))
        im_p.append(him.reshape(BATCH, SSM_GROUPS, SSM_STATE))
        pool_p.append(jnp.swapaxes(hist.reshape(POOL_HIST, BATCH, BRANCH), 0, 1))

        re_s.append(hre_s.reshape(DEC_BATCH, SSM_GROUPS, SSM_STATE))
        im_s.append(him_s.reshape(DEC_BATCH, SSM_GROUPS, SSM_STATE))
        pool_s.append(jnp.swapaxes(nhist, 0, 1))

    y_prompt = xp
    y_sample = xs.reshape(DEC_BATCH, 1, D_MODEL)
    kv_shape = (DEPTH, BATCH, N_MEM, MEM_HEADS, MEM_HEAD_DIM)
    return (y_prompt, y_sample,
            jnp.stack(re_p), jnp.stack(im_p), jnp.stack(pool_p),
            k_mem.reshape(kv_shape), v_mem.reshape(kv_shape),
            jnp.stack(re_s), jnp.stack(im_s), jnp.stack(pool_s))
```

```python
import jax
import jax.numpy as jnp
from jax import lax
from jax.experimental import pallas as pl
from jax.experimental.pallas import tpu as pltpu

D_MODEL = 1024
BATCH = 8
SEQ = 2048
DEPTH = 4
DEC_BATCH = 128
PAST_LEN = 16384

BRANCH = D_MODEL // 2
SSM_GROUP = 16
SSM_GROUPS = BRANCH // SSM_GROUP
SSM_STATE = 64
N_STATE = SSM_GROUPS * SSM_STATE
POOL_WINDOWS = (2, 4, 8, 16)
POOL_GROUP_WIDTH = BRANCH // len(POOL_WINDOWS)
POOL_HIST = max(POOL_WINDOWS) - 1
N_MEM = 256
MEM_HEADS = 4
MEM_HEAD_DIM = BRANCH // MEM_HEADS
KV_ROWS = N_MEM * MEM_HEADS
N_BRANCH = 3
GATE_HALF = N_BRANCH * D_MODEL // 2
D_FF = 2816
RMS_EPS = 1e-6

LANES = 128
SUBLANES = 8
GROUPS_PER_BLOCK = LANES // SSM_GROUP
N_SSM_BLOCKS = BRANCH // LANES
BLOCK_STATE = GROUPS_PER_BLOCK * SSM_STATE

ROWS_A = 512
STEPS_A = ROWS_A // BATCH
ROWS_C = 512
MERGE_COLS = 256
TAIL_GROUPS = 2
PIECE_ROWS = 128
SAMPLES_PER_STEP = DEC_BATCH // (BATCH * SEQ // ROWS_C)
FF_CHUNKS = ((0, 1536), (1536, 2816))
VMEM_LIMIT = 56 * 1024 * 1024
VMEM_LIMIT_MERGE = 60 * 1024 * 1024

BF16 = jnp.bfloat16
F32 = jnp.float32


def _rms(x, g):
    ms = jnp.mean(x * x, axis=-1, keepdims=True)
    return x * lax.rsqrt(ms + RMS_EPS) * g


def _dot(a, b):
    return jnp.dot(a, b, preferred_element_type=F32)


def _const_spec(shape):
    nd = len(shape)
    return pl.BlockSpec(shape, lambda *_: (0,) * nd, pipeline_mode=pl.Buffered(1))


def _layer_spec(shape, layer):
    nd = len(shape)
    return pl.BlockSpec((None,) + tuple(shape), lambda *_: (layer,) + (0,) * nd,
                        pipeline_mode=pl.Buffered(1))


def _discretise_kernel(lr_ref, li_ref, ldt_ref, br_ref, bi_ref,
                       ar_ref, ai_ref, bbr_ref, bbi_ref):
    lr = lr_ref[...]
    li = li_ref[...]
    dt = jnp.exp(ldt_ref[...])
    zr = lr * dt
    zi = li * dt
    mag = jnp.exp(zr)
    ar = mag * jnp.cos(zi)
    ai = mag * jnp.sin(zi)
    den = lr * lr + li * li
    fr = ((ar - 1.0) * lr + ai * li) / den
    fi = (ai * lr - (ar - 1.0) * li) / den
    br = br_ref[...]
    bi = bi_ref[...]
    ar_ref[...] = ar
    ai_ref[...] = ai
    bbr_ref[...] = fr * br - fi * bi
    bbi_ref[...] = fr * bi + fi * br


def _discretise(lam_re, lam_im, log_dt, b_re, b_im):
    rows = DEPTH * SSM_GROUPS
    width = SSM_GROUP * SSM_STATE

    def tile_p(a):
        return jnp.tile(a.reshape(rows, 1, SSM_STATE), (1, SSM_GROUP, 1)).reshape(rows, width)

    def b_t(a):
        return jnp.swapaxes(a, -1, -2).reshape(rows, width)

    ldt = jnp.broadcast_to(log_dt.reshape(rows, 1), (rows, width))
    out = jax.ShapeDtypeStruct((rows, width), F32)
    ar, ai, bbr, bbi = pl.pallas_call(
        _discretise_kernel, out_shape=(out, out, out, out), name="s5_discretise",
    )(tile_p(lam_re), tile_p(lam_im), ldt, b_t(b_re), b_t(b_im))
    ar = ar[:, :SSM_STATE].reshape(DEPTH, 1, N_STATE)
    ai = ai[:, :SSM_STATE].reshape(DEPTH, 1, N_STATE)
    shape5 = (DEPTH, N_SSM_BLOCKS, GROUPS_PER_BLOCK, SSM_GROUP, SSM_STATE)
    bblk = jnp.concatenate([_group_block_diag(bbr.reshape(shape5)),
                            _group_block_diag(bbi.reshape(shape5))], axis=-1)
    return ar, ai, bblk


def _group_block_diag(m):
    depth, blocks, groups, n_r, n_c = m.shape
    same = jnp.eye(groups, dtype=bool)[None, None, :, None, :, None]
    out = jnp.where(same, m[:, :, :, :, None, :], 0.0).astype(BF16)
    return out.reshape(depth, blocks, groups * n_r, groups * n_c)


def _c_blocks(c):
    shape5 = (DEPTH, N_SSM_BLOCKS, GROUPS_PER_BLOCK, SSM_GROUP, SSM_STATE)
    return _group_block_diag(jnp.swapaxes(c.reshape(shape5), -1, -2))


def _ssm_input(u_block, bblk, bu_re_ref, bu_im_ref, sl):
    bu = _dot(u_block.astype(BF16), bblk)
    bu_re_ref[:, sl] = bu[:, :BLOCK_STATE]
    bu_im_ref[:, sl] = bu[:, BLOCK_STATE:]


def _ssm_readout(h_re_ref, h_im_ref, sl, cre, cimn):
    return _dot(h_re_ref[:, sl].astype(BF16), cre) + _dot(h_im_ref[:, sl].astype(BF16), cimn)


def _ssm_gate(y, u_ssm, d_ref, wglu_ref, bglu_ref):
    y = jax.nn.gelu(y + d_ref[...] * u_ssm)
    return y * jax.nn.sigmoid(_dot(y.astype(BF16), wglu_ref[...]) + bglu_ref[...])


def _pool_mix(pooled, poolw_ref, pscale_ref):
    outs = []
    for gi in range(len(POOL_WINDOWS)):
        sl = slice(gi * POOL_GROUP_WIDTH, (gi + 1) * POOL_GROUP_WIDTH)
        outs.append(_dot(pooled[gi].astype(BF16), poolw_ref[gi]) * pscale_ref[:, sl])
    return jnp.concatenate(outs, axis=-1)


def _sample_pair_stages(sq_ref, sk_ref, sv_ref, ones_ref, so_all, first_row, first):
    tiles = KV_ROWS // SUBLANES
    width = 2 * MEM_HEAD_DIM
    pair = (first, first + 1)
    q8 = jnp.concatenate([sq_ref[n] for n in pair], axis=-1) * MEM_HEAD_DIM ** -0.5
    k = jnp.concatenate([sk_ref[n] for n in pair], axis=-1)
    prod = (k.reshape(tiles, SUBLANES, width) * q8[None]).reshape(KV_ROWS, width).astype(BF16)
    s = _dot(prod, ones_ref[...]).reshape(tiles, SUBLANES, width)
    yield
    v = jnp.concatenate([sv_ref[n] for n in pair], axis=-1).reshape(tiles, SUBLANES, width)
    m8 = jnp.max(s, axis=0)
    m8 = jnp.maximum(m8, pltpu.roll(m8, MEM_HEADS, 0))
    e = jnp.exp(s - m8[None])
    l8 = jnp.sum(e, axis=0)
    acc = jnp.sum(e * v, axis=0)
    l8 = l8 + pltpu.roll(l8, MEM_HEADS, 0)
    acc = acc + pltpu.roll(acc, MEM_HEADS, 0)
    out = acc / l8
    for j, n in enumerate(pair):
        rows = pl.ds(pl.multiple_of(first_row + n * SUBLANES, SUBLANES), SUBLANES)
        so_all[rows, :] = out[:, j * MEM_HEAD_DIM:(j + 1) * MEM_HEAD_DIM]


def _mixer_a_kernel(xfirst_ref, xnext_ref, g_ref, win_ref, bblk_ref, cre_ref, cimn_ref,
                    ar_ref, ai_ref, d_ref, wglu_ref, bglu_ref, poolw_ref, pscale_ref,
                    wgate_lo_ref, wgate_hi_ref, wup_ref, wout_ref, wfin_ref, wfout_ref,
                    q_ref, ossm_ref, opool_ref, hre_ref, him_ref, hist_ref,
                    wgate_bf_ref, wup_bf_ref, wout_bf_ref, wfin_bf_ref, wfout_bf_ref,
                    bu_re_ref, bu_im_ref, pool_buf, ussm_tm, ossm_tm, opool_tm, win_bf,
                    proj_scr, hnext_scr):
    i = pl.program_id(0)
    hist_rows = POOL_HIST * BATCH

    wgate_bf_ref[:, :GATE_HALF] = wgate_lo_ref[...].astype(BF16)
    wgate_bf_ref[:, GATE_HALF:] = wgate_hi_ref[...].astype(BF16)
    wup_bf_ref[...] = wup_ref[...].astype(BF16)
    wout_bf_ref[...] = wout_ref[...].astype(BF16)
    wfin_bf_ref[...] = wfin_ref[...].astype(BF16)
    wfout_bf_ref[...] = wfout_ref[...].astype(BF16)

    def normed(x_blk_ref):
        return _rms(x_blk_ref[...].reshape(ROWS_A, D_MODEL), g_ref[...]).astype(BF16)

    @pl.when(i == 0)
    def _():
        hre_ref[...] = jnp.zeros_like(hre_ref)
        him_ref[...] = jnp.zeros_like(him_ref)
        pool_buf[:, 0:hist_rows, :] = jnp.zeros((N_SSM_BLOCKS, hist_rows, LANES), F32)
        win_bf[...] = win_ref[...].astype(BF16)
        proj_scr[...] = _dot(normed(xfirst_ref), win_bf[...])

    for b in range(BATCH):
        rows = slice(b * STEPS_A, (b + 1) * STEPS_A)
        q_ref[b] = proj_scr[rows, 2 * BRANCH:].astype(BF16)
        for k in range(N_SSM_BLOCKS):
            ussm_tm[k, pl.ds(b, STEPS_A, stride=BATCH), :] = (
                proj_scr[rows, k * LANES:(k + 1) * LANES])
            pool_buf[k, pl.ds(hist_rows + b, STEPS_A, stride=BATCH), :] = (
                proj_scr[rows, BRANCH + k * LANES:BRANCH + (k + 1) * LANES])
    u_ssm = jnp.concatenate([ussm_tm[k] for k in range(N_SSM_BLOCKS)], axis=-1)
    hnext_scr[...] = normed(xnext_ref)
    next_cols = [slice(c, c + MERGE_COLS) for c in range(0, 3 * BRANCH, MERGE_COLS)]

    def project_next_chunk():
        if next_cols:
            cols = next_cols.pop(0)
            proj_scr[:, cols] = _dot(hnext_scr[...], win_bf[:, cols])

    ys = []
    for k in range(N_SSM_BLOCKS):
        sl = slice(k * BLOCK_STATE, (k + 1) * BLOCK_STATE)
        _ssm_input(ussm_tm[k], bblk_ref[k], bu_re_ref, bu_im_ref, sl)
        project_next_chunk()
        a_r = jnp.broadcast_to(ar_ref[:, sl], (BATCH, BLOCK_STATE))
        a_i = jnp.broadcast_to(ai_ref[:, sl], (BATCH, BLOCK_STATE))
        hr = hre_ref[:, sl]
        hi = him_ref[:, sl]
        for t in range(STEPS_A):
            rows = slice(t * BATCH, (t + 1) * BATCH)
            hr, hi = (a_r * hr - a_i * hi + bu_re_ref[rows, sl],
                      a_r * hi + a_i * hr + bu_im_ref[rows, sl])
            bu_re_ref[rows, sl] = hr
            bu_im_ref[rows, sl] = hi
        hre_ref[:, sl] = hr
        him_ref[:, sl] = hi
        ys.append(_ssm_readout(bu_re_ref, bu_im_ref, sl, cre_ref[k], cimn_ref[k]))
    project_next_chunk()
    o_ssm = _ssm_gate(jnp.concatenate(ys, axis=-1), u_ssm, d_ref, wglu_ref, bglu_ref)

    project_next_chunk()
    t_pos = i * STEPS_A + lax.broadcasted_iota(jnp.int32, (ROWS_A, POOL_GROUP_WIDTH), 0) // BATCH
    pooled = []
    for gi, w in enumerate(POOL_WINDOWS):
        u_g = pool_buf[gi, hist_rows:hist_rows + ROWS_A, :]
        acc = pool_buf[gi, hist_rows - (w - 1) * BATCH:hist_rows + ROWS_A, :]
        shift = BATCH
        while acc.shape[0] > ROWS_A:
            acc = acc[shift:, :] + acc[:-shift, :]
            shift *= 2
        cnt = jnp.minimum(t_pos + 1, w).astype(F32)
        pooled.append(acc / cnt - u_g)
    o_pool = _pool_mix(pooled, poolw_ref, pscale_ref)
    pool_buf[:, 0:hist_rows, :] = pool_buf[:, ROWS_A:ROWS_A + hist_rows, :]

    for k in range(N_SSM_BLOCKS):
        ossm_tm[k] = o_ssm[:, k * LANES:(k + 1) * LANES]
        opool_tm[k] = o_pool[:, k * LANES:(k + 1) * LANES]
    for b in range(BATCH):
        rows = pl.ds(b, STEPS_A, stride=BATCH)
        ossm_ref[b] = jnp.concatenate(
            [ossm_tm[k, rows, :] for k in range(N_SSM_BLOCKS)], axis=-1).astype(BF16)
        opool_ref[b] = jnp.concatenate(
            [opool_tm[k, rows, :] for k in range(N_SSM_BLOCKS)], axis=-1).astype(BF16)

    @pl.when(i == pl.num_programs(0) - 1)
    def _():
        for k in range(N_SSM_BLOCKS):
            hist_ref[:, k * LANES:(k + 1) * LANES] = pool_buf[k, 0:hist_rows, :]


def _mixer_a(layer, x, p, raw):
    hist_rows = POOL_HIST * BATCH
    n_steps = SEQ // STEPS_A
    seq_spec = lambda w: pl.BlockSpec((BATCH, STEPS_A, w), lambda i: (0, i, 0))
    cast_plan = {
        'w_gate': (D_MODEL // n_steps, N_BRANCH * D_MODEL, 1),
        'w_up': (N_BRANCH * BRANCH // n_steps, D_MODEL, 1),
        'w_out': (D_MODEL // n_steps, D_MODEL, 1),
        'w_ffn_in': (D_MODEL // n_steps, 2 * D_FF, 1),
        'w_ffn_out': (2 * D_FF // n_steps, D_MODEL, 2),
    }

    def chunk_in(name, width=None, col=0):
        rows, full_width, per = cast_plan[name]
        return pl.BlockSpec((None, rows, width or full_width),
                            lambda i: (layer, i // per, col))

    def chunk_out(name):
        rows, width, per = cast_plan[name]
        return pl.BlockSpec((rows, width), lambda i: (i // per, 0))

    cast_names = ('w_gate', 'w_up', 'w_out', 'w_ffn_in', 'w_ffn_out')
    cast_shapes = tuple(
        jax.ShapeDtypeStruct((cast_plan[n][0] * n_steps // cast_plan[n][2], cast_plan[n][1]), BF16)
        for n in cast_names)
    in_specs = [
        pl.BlockSpec((BATCH, STEPS_A, D_MODEL), lambda i: (0, 0, 0), pipeline_mode=pl.Buffered(1)),
        pl.BlockSpec((BATCH, STEPS_A, D_MODEL), lambda i: (0, jnp.minimum(i + 1, n_steps - 1), 0)),
        _layer_spec((1, D_MODEL), layer),
        _layer_spec((D_MODEL, 3 * BRANCH), layer),
        _layer_spec((N_SSM_BLOCKS, LANES, 2 * BLOCK_STATE), layer),
        _layer_spec((N_SSM_BLOCKS, BLOCK_STATE, LANES), layer),
        _layer_spec((N_SSM_BLOCKS, BLOCK_STATE, LANES), layer),
        _layer_spec((1, N_STATE), layer),
        _layer_spec((1, N_STATE), layer),
        _layer_spec((1, BRANCH), layer),
        _layer_spec((BRANCH, BRANCH), layer),
        _layer_spec((1, BRANCH), layer),
        _layer_spec((len(POOL_WINDOWS), POOL_GROUP_WIDTH, POOL_GROUP_WIDTH), layer),
        _layer_spec((1, BRANCH), layer),
        chunk_in('w_gate', GATE_HALF, 1), chunk_in('w_gate', GATE_HALF, 2),
        chunk_in('w_up'), chunk_in('w_out'), chunk_in('w_ffn_in'), chunk_in('w_ffn_out'),
    ]
    branch_out = jax.ShapeDtypeStruct((BATCH, SEQ, BRANCH), BF16)
    out_shape = (
        branch_out, branch_out, branch_out,
        jax.ShapeDtypeStruct((BATCH, N_STATE), F32),
        jax.ShapeDtypeStruct((BATCH, N_STATE), F32),
        jax.ShapeDtypeStruct((hist_rows, BRANCH), F32),
    ) + cast_shapes
    out_specs = (
        seq_spec(BRANCH), seq_spec(BRANCH), seq_spec(BRANCH),
        pl.BlockSpec((BATCH, N_STATE), lambda i: (0, 0)),
        pl.BlockSpec((BATCH, N_STATE), lambda i: (0, 0)),
        pl.BlockSpec((hist_rows, BRANCH), lambda i: (0, 0)),
    ) + tuple(chunk_out(n) for n in cast_names)
    outs = pl.pallas_call(
        _mixer_a_kernel,
        grid=(n_steps,),
        in_specs=in_specs, out_specs=out_specs, out_shape=out_shape,
        scratch_shapes=[
            pltpu.VMEM((ROWS_A, N_STATE), F32),
            pltpu.VMEM((ROWS_A, N_STATE), F32),
            pltpu.VMEM((N_SSM_BLOCKS, hist_rows + ROWS_A, LANES), F32),
            pltpu.VMEM((N_SSM_BLOCKS, ROWS_A, LANES), F32),
            pltpu.VMEM((N_SSM_BLOCKS, ROWS_A, LANES), F32),
            pltpu.VMEM((N_SSM_BLOCKS, ROWS_A, LANES), F32),
            pltpu.VMEM((D_MODEL, 3 * BRANCH), BF16),
            pltpu.VMEM((ROWS_A, 3 * BRANCH), F32),
            pltpu.VMEM((ROWS_A, D_MODEL), BF16),
        ],
        compiler_params=pltpu.CompilerParams(
            dimension_semantics=("arbitrary",), vmem_limit_bytes=VMEM_LIMIT),
        name="prompt_mixer_a",
    )(x, x, p['g_mix_pre'], raw['w_in'], p['bblk'], p['cre'], p['cimn'], p['ar'], p['ai'],
      p['ssm_d'], p['w_glu'], p['b_glu'], p['pool_w'], p['pool_scale'],
      raw['w_in'], raw['w_in'], raw['w_up'], raw['w_out'], raw['w_ffn_in'], raw['w_ffn_out'])
    return outs[:6], dict(zip(cast_names, outs[6:]))


def _mem_kv_kernel(mem_ref, g_ref, wkv_ref, k_ref, v_ref):
    wkv = wkv_ref[...].astype(BF16)
    for b in range(BATCH):
        kv = _dot(_rms(mem_ref[b], g_ref[...]).astype(BF16), wkv)
        for hd in range(MEM_HEADS):
            rows = pl.ds(hd, N_MEM, stride=MEM_HEADS)
            k_ref[b, rows, :] = kv[:, hd * MEM_HEAD_DIM:(hd + 1) * MEM_HEAD_DIM]
            v_ref[b, rows, :] = kv[:, BRANCH + hd * MEM_HEAD_DIM:BRANCH + (hd + 1) * MEM_HEAD_DIM]


def _mem_kv(mem, g_mem, w_kv):
    out = jax.ShapeDtypeStruct((DEPTH, BATCH, KV_ROWS, MEM_HEAD_DIM), F32)
    out_spec = pl.BlockSpec((None, BATCH, KV_ROWS, MEM_HEAD_DIM), lambda l: (l, 0, 0, 0))
    return pl.pallas_call(
        _mem_kv_kernel,
        grid=(DEPTH,),
        in_specs=[
            _const_spec((BATCH, N_MEM, D_MODEL)),
            pl.BlockSpec((None, 1, D_MODEL), lambda l: (l, 0, 0)),
            pl.BlockSpec((None, D_MODEL, 2 * BRANCH), lambda l: (l, 0, 0)),
        ],
        out_specs=(out_spec, out_spec), out_shape=(out, out),
        compiler_params=pltpu.CompilerParams(
            dimension_semantics=("arbitrary",), vmem_limit_bytes=VMEM_LIMIT),
        name="mem_kv",
    )(mem, g_mem, w_kv)


def _memory_attention_stages(q_ref, k_ref, v_ref, result):
    scale = MEM_HEAD_DIM ** -0.5
    head_rows = [pl.ds(hd, N_MEM, stride=MEM_HEADS) for hd in range(MEM_HEADS)]
    head_cols = [slice(hd * MEM_HEAD_DIM, (hd + 1) * MEM_HEAD_DIM) for hd in range(MEM_HEADS)]
    scores = [lax.dot_general(q_ref[:, head_cols[hd]], k_ref[head_rows[hd], :].astype(BF16),
                              (((1,), (1,)), ((), ())), preferred_element_type=F32) * scale
              for hd in range(MEM_HEADS)]
    yield
    probs = []
    for s in scores:
        e = jnp.exp(s - jnp.max(s, axis=-1, keepdims=True))
        probs.append((e / jnp.sum(e, axis=-1, keepdims=True)).astype(BF16))
        yield
    outs = []
    for hd in range(MEM_HEADS):
        outs.append(_dot(probs[hd], v_ref[head_rows[hd], :].astype(BF16)).astype(BF16))
        yield
    result.append(jnp.concatenate(outs, axis=-1))


def _merge_rows(x, o_ssm, o_pool, attention, o_mem, side_jobs, store, weights):
    (gpre_ref, wgate_ref, wup_ref, wout_ref, gpost_ref, gfpre_ref, wfin_ref, wfout_ref,
     gfpost_ref) = weights
    h = _rms(x, gpre_ref[...]).astype(BF16)

    def branch_term(b, o_b, cols):
        gate_cols = slice(b * D_MODEL + cols.start, b * D_MODEL + cols.stop)
        gate = jax.nn.sigmoid(_dot(h, wgate_ref[:, gate_cols]))
        return gate * _dot(o_b, wup_ref[b * BRANCH:(b + 1) * BRANCH, cols])

    col_chunks = [slice(c, c + MERGE_COLS) for c in range(0, D_MODEL, MERGE_COLS)]
    next(attention, None)
    partial = []
    for b, o_b in enumerate((o_ssm, o_pool)):
        for ci, cols in enumerate(col_chunks):
            term = branch_term(b, o_b, cols)
            if b == 0:
                partial.append(term)
            else:
                partial[ci] = partial[ci] + term
            next(attention, None)
    for _ in attention:
        pass
    merged = jnp.concatenate(
        [partial[ci] + branch_term(N_BRANCH - 1, o_mem[0], cols)
         for ci, cols in enumerate(col_chunks)], axis=-1).astype(BF16)

    def tail_stages(rows):
        m2 = _dot(merged[rows, :], wout_ref[...])
        yield
        x1 = x[rows, :] + _rms(m2, gpost_ref[...])
        hf = _rms(x1, gfpre_ref[...]).astype(BF16)
        yield
        f = None
        for lo, hi in FF_CHUNKS:
            hg = _dot(hf, wfin_ref[:, lo:hi])
            hu = _dot(hf, wfin_ref[:, D_FF + lo:D_FF + hi])
            yield
            act = (jax.nn.silu(hg) * hu).astype(BF16)
            yield
            part = _dot(act, wfout_ref[lo:hi, :])
            f = part if f is None else f + part
            yield
        store(rows, x1 + _rms(f, gfpost_ref[...]))

    n_groups = TAIL_GROUPS if x.shape[0] >= TAIL_GROUPS * PIECE_ROWS else 1
    group_rows = x.shape[0] // n_groups
    waiting = [tail_stages(slice(g * group_rows, (g + 1) * group_rows)) for g in range(n_groups)]
    waiting += side_jobs
    running = []
    while waiting or running:
        if waiting:
            running.append(waiting.pop(0))
        for chain in list(running):
            if next(chain, StopIteration) is StopIteration:
                running.remove(chain)


def _merge_ffn_kernel(x_ref, ossm_ref, opool_ref, q_ref, k_ref, v_ref,
                      sq_ref, sk_ref, sv_ref, ones_ref, xs_ref, sossm_ref, sopool_ref,
                      gpre_ref, wgate_ref, wup_ref, wout_ref, gpost_ref, gfpre_ref, wfin_ref,
                      wfout_ref, gfpost_ref, y_ref, ys_ref, so_all):
    i = pl.program_id(0)
    weights = (gpre_ref, wgate_ref, wup_ref, wout_ref, gpost_ref, gfpre_ref, wfin_ref,
               wfout_ref, gfpost_ref)
    o_mem = []
    attention = _memory_attention_stages(q_ref, k_ref, v_ref, o_mem)
    block_rows = SAMPLES_PER_STEP * SUBLANES
    first_row = pl.multiple_of(i * block_rows, block_rows)
    side_jobs = [_sample_pair_stages(sq_ref, sk_ref, sv_ref, ones_ref, so_all, first_row, first)
                 for first in range(0, SAMPLES_PER_STEP, 2)]

    def store_prompt(rows, y):
        y_ref[rows, :] = y

    _merge_rows(x_ref[...], ossm_ref[...], opool_ref[...], attention, o_mem, side_jobs,
                store_prompt, weights)

    @pl.when(i == pl.num_programs(0) - 1)
    def _():
        heads = [so_all[pl.ds(hd, DEC_BATCH, stride=SUBLANES), :] for hd in range(MEM_HEADS)]
        o_mem_sample = [jnp.concatenate(heads, axis=-1).astype(BF16)]

        def store_sample(rows, y):
            ys_ref[rows, :] = y

        _merge_rows(xs_ref[...], sossm_ref[...], sopool_ref[...], iter(()), o_mem_sample, [],
                    store_sample, weights)


def _merge_ffn(layer, x, o_ssm, o_pool, q, k_all, v_all, sample, p, w):
    n_rows = x.shape[0]
    n_sample = sample[4].shape[0]
    row_spec = lambda w: pl.BlockSpec((ROWS_C, w), lambda i: (i, 0))
    tiles_per_seq = SEQ // ROWS_C
    kv_spec = pl.BlockSpec((None, None, KV_ROWS, MEM_HEAD_DIM),
                           lambda i: (layer, i // tiles_per_seq, 0, 0))
    sample_spec = pl.BlockSpec((SAMPLES_PER_STEP, SUBLANES, MEM_HEAD_DIM), lambda i: (i, 0, 0))
    cache_spec = pl.BlockSpec((None, SAMPLES_PER_STEP, KV_ROWS, MEM_HEAD_DIM),
                              lambda i: (layer, i, 0, 0))
    in_specs = [
        row_spec(D_MODEL), row_spec(BRANCH), row_spec(BRANCH),
        row_spec(BRANCH), kv_spec, kv_spec,
        sample_spec, cache_spec, cache_spec, _const_spec((2 * MEM_HEAD_DIM, 2 * MEM_HEAD_DIM)),
        _const_spec((n_sample, D_MODEL)), _const_spec((n_sample, BRANCH)),
        _const_spec((n_sample, BRANCH)),
        _layer_spec((1, D_MODEL), layer),
        _const_spec((D_MODEL, N_BRANCH * D_MODEL)),
        _const_spec((N_BRANCH * BRANCH, D_MODEL)),
        _const_spec((D_MODEL, D_MODEL)),
        _layer_spec((1, D_MODEL), layer),
        _layer_spec((1, D_MODEL), layer),
        _const_spec((D_MODEL, 2 * D_FF)),
        _const_spec((D_FF, D_MODEL)),
        _layer_spec((1, D_MODEL), layer),
    ]
    return pl.pallas_call(
        _merge_ffn_kernel,
        grid=(n_rows // ROWS_C,),
        in_specs=in_specs,
        out_specs=(row_spec(D_MODEL), pl.BlockSpec((n_sample, D_MODEL), lambda i: (0, 0))),
        out_shape=(jax.ShapeDtypeStruct((n_rows, D_MODEL), F32),
                   jax.ShapeDtypeStruct((n_sample, D_MODEL), F32)),
        scratch_shapes=[pltpu.VMEM((n_sample * SUBLANES, MEM_HEAD_DIM), F32)],
        compiler_params=pltpu.CompilerParams(
            dimension_semantics=("arbitrary",), vmem_limit_bytes=VMEM_LIMIT_MERGE),
        name="merge_ffn",
    )(x, o_ssm, o_pool, q, k_all, v_all, *sample, p['g_mix_pre'], w['w_gate'], w['w_up'],
      w['w_out'], p['g_mix_post'], p['g_ffn_pre'], w['w_ffn_in'], w['w_ffn_out'], p['g_ffn_post'])


def _sample_mixer_kernel(x_ref, g_ref, win_ref, bblk_ref, cre_ref, cimn_ref, ar_ref, ai_ref,
                         d_ref, wglu_ref, bglu_ref, poolw_ref, pscale_ref,
                         h0re_ref, h0im_ref, hist_ref,
                         q_ref, ossm_ref, opool_ref, hre_ref, him_ref, nhist_ref,
                         bu_re_ref, bu_im_ref):
    h = _rms(x_ref[...], g_ref[...]).astype(BF16)
    proj = _dot(h, win_ref[...].astype(BF16))
    u_ssm = proj[:, :BRANCH]
    u_pool = proj[:, BRANCH:2 * BRANCH]
    q_ref[...] = proj[:, 2 * BRANCH:]

    ys = []
    for k in range(N_SSM_BLOCKS):
        sl = slice(k * BLOCK_STATE, (k + 1) * BLOCK_STATE)
        _ssm_input(u_ssm[:, k * LANES:(k + 1) * LANES], bblk_ref[k], bu_re_ref, bu_im_ref, sl)
        a_r = ar_ref[:, sl]
        a_i = ai_ref[:, sl]
        h0r = h0re_ref[:, sl]
        h0i = h0im_ref[:, sl]
        hre_ref[:, sl] = bu_re_ref[:, sl] + a_r * h0r - a_i * h0i
        him_ref[:, sl] = bu_im_ref[:, sl] + a_r * h0i + a_i * h0r
        ys.append(_ssm_readout(hre_ref, him_ref, sl, cre_ref[k], cimn_ref[k]))
    y = _ssm_gate(jnp.concatenate(ys, axis=-1), u_ssm, d_ref, wglu_ref, bglu_ref)
    ossm_ref[...] = y.astype(BF16)

    pooled = []
    for gi, w in enumerate(POOL_WINDOWS):
        sl = slice(gi * POOL_GROUP_WIDTH, (gi + 1) * POOL_GROUP_WIDTH)
        acc = u_pool[:, sl]
        for j in range(1, w):
            acc = acc + hist_ref[POOL_HIST - j, :, sl]
        cnt = float(min(PAST_LEN + 1, w))
        pooled.append(acc / cnt - u_pool[:, sl])
    opool_ref[...] = _pool_mix(pooled, poolw_ref, pscale_ref).astype(BF16)
    for j in range(POOL_HIST - 1):
        nhist_ref[j] = hist_ref[j + 1]
    nhist_ref[POOL_HIST - 1] = u_pool


def _sample_mixer(layer, x_s, p, w_in, h0_re, h0_im, hist_t):
    n = x_s.shape[0]
    full = lambda shape: pl.BlockSpec(shape, lambda i: (0,) * len(shape))
    in_specs = [
        full((n, D_MODEL)),
        _layer_spec((1, D_MODEL), layer),
        _layer_spec((D_MODEL, 3 * BRANCH), layer),
        _layer_spec((N_SSM_BLOCKS, LANES, 2 * BLOCK_STATE), layer),
        _layer_spec((N_SSM_BLOCKS, BLOCK_STATE, LANES), layer),
        _layer_spec((N_SSM_BLOCKS, BLOCK_STATE, LANES), layer),
        _layer_spec((1, N_STATE), layer),
        _layer_spec((1, N_STATE), layer),
        _layer_spec((1, BRANCH), layer),
        _layer_spec((BRANCH, BRANCH), layer),
        _layer_spec((1, BRANCH), layer),
        _layer_spec((len(POOL_WINDOWS), POOL_GROUP_WIDTH, POOL_GROUP_WIDTH), layer),
        _layer_spec((1, BRANCH), layer),
        _layer_spec((n, N_STATE), layer),
        _layer_spec((n, N_STATE), layer),
        _layer_spec((POOL_HIST, n, BRANCH), layer),
    ]
    out_shape = (
        jax.ShapeDtypeStruct((n, BRANCH), F32),
        jax.ShapeDtypeStruct((n, BRANCH), BF16),
        jax.ShapeDtypeStruct((n, BRANCH), BF16),
        jax.ShapeDtypeStruct((n, N_STATE), F32),
        jax.ShapeDtypeStruct((n, N_STATE), F32),
        jax.ShapeDtypeStruct((POOL_HIST, n, BRANCH), F32),
    )
    out_specs = tuple(full(s.shape) for s in out_shape)
    return pl.pallas_call(
        _sample_mixer_kernel,
        grid=(1,),
        in_specs=in_specs, out_specs=out_specs, out_shape=out_shape,
        scratch_shapes=[pltpu.VMEM((n, N_STATE), F32), pltpu.VMEM((n, N_STATE), F32)],
        compiler_params=pltpu.CompilerParams(
            dimension_semantics=("arbitrary",), vmem_limit_bytes=VMEM_LIMIT),
        name="sample_mixer",
    )(x_s, p['g_mix_pre'], w_in, p['bblk'], p['cre'], p['cimn'], p['ar'], p['ai'],
      p['ssm_d'], p['w_glu'], p['b_glu'], p['pool_w'], p['pool_scale'], h0_re, h0_im, hist_t)


def kernel(x_prompt, x_sample, mem_prompt, cache_mem_k, cache_mem_v, state_ssm_re, state_ssm_im, state_pool, g_mix_pre, g_mix_post, g_ffn_pre, g_ffn_post, g_mem, w_in, w_kv, ssm_lam_re, ssm_lam_im, ssm_log_dt, ssm_b_re, ssm_b_im, ssm_c_re, ssm_c_im, ssm_d, ssm_w_glu, ssm_b_glu, pool_w, pool_scale, w_branch_up, w_out, w_ffn_in, w_ffn_out):
    ar, ai, bblk = _discretise(ssm_lam_re, ssm_lam_im, ssm_log_dt, ssm_b_re, ssm_b_im)
    vec = lambda a: a.reshape(DEPTH, 1, a.shape[-1])
    p = {
        'g_mix_pre': vec(g_mix_pre), 'g_mix_post': vec(g_mix_post),
        'g_ffn_pre': vec(g_ffn_pre), 'g_ffn_post': vec(g_ffn_post),
        'bblk': bblk, 'ar': ar, 'ai': ai,
        'cre': _c_blocks(ssm_c_re), 'cimn': _c_blocks(-ssm_c_im),
        'ssm_d': vec(ssm_d), 'w_glu': ssm_w_glu.astype(BF16), 'b_glu': vec(ssm_b_glu),
        'pool_w': pool_w.astype(BF16), 'pool_scale': vec(pool_scale),
    }
    raw = {
        'w_in': w_in, 'w_up': w_branch_up.reshape(DEPTH, N_BRANCH * BRANCH, D_MODEL),
        'w_out': w_out, 'w_ffn_in': w_ffn_in, 'w_ffn_out': w_ffn_out,
    }
    k_mem, v_mem = _mem_kv(mem_prompt, vec(g_mem), w_kv)

    lane_seq = jnp.arange(2 * MEM_HEAD_DIM, dtype=jnp.int32) // MEM_HEAD_DIM
    pair_ones = (lane_seq[:, None] == lane_seq[None, :]).astype(BF16)
    k_cache = cache_mem_k.reshape(DEPTH, DEC_BATCH, KV_ROWS, MEM_HEAD_DIM)
    v_cache = cache_mem_v.reshape(DEPTH, DEC_BATCH, KV_ROWS, MEM_HEAD_DIM)
    h0_re = state_ssm_re.reshape(DEPTH, DEC_BATCH, N_STATE)
    h0_im = state_ssm_im.reshape(DEPTH, DEC_BATCH, N_STATE)
    hist_t = jnp.swapaxes(state_pool, 1, 2)

    xp = x_prompt
    xs = x_sample.reshape(DEC_BATCH, D_MODEL)
    n_prompt = BATCH * SEQ
    re_p, im_p, pool_p, re_s, im_s, pool_s = [], [], [], [], [], []
    for layer in range(DEPTH):
        qs, os_ssm, os_pool, hre_s, him_s, nhist = _sample_mixer(layer, xs, p, w_in, h0_re, h0_im,
                                                                 hist_t)
        qs4 = qs.reshape(DEC_BATCH, MEM_HEADS, MEM_HEAD_DIM)
        sample = (jnp.concatenate([qs4, qs4], axis=1), k_cache, v_cache, pair_ones,
                  xs, os_ssm, os_pool)
        (q, o_ssm, o_pool, hre, him, hist), w_bf = _mixer_a(layer, xp, p, raw)
        xp, xs = _merge_ffn(layer, xp.reshape(n_prompt, D_MODEL),
                            o_ssm.reshape(n_prompt, BRANCH), o_pool.reshape(n_prompt, BRANCH),
                            q.reshape(n_prompt, BRANCH), k_mem, v_mem, sample, p, w_bf)
        xp = xp.reshape(BATCH, SEQ, D_MODEL)
        re_p.append(hre.reshape(BATCH, SSM_GROUPS, SSM_STATE))
        im_p.append(him.reshape(BATCH, SSM_GROUPS, SSM_STATE))
        pool_p.append(jnp.swapaxes(hist.reshape(POOL_HIST, BATCH, BRANCH), 0, 1))

        re_s.append(hre_s.reshape(DEC_BATCH, SSM_GROUPS, SSM_STATE))
        im_s.append(him_s.reshape(DEC_BATCH, SSM_GROUPS, SSM_STATE))
        pool_s.append(jnp.swapaxes(nhist, 0, 1))

    y_prompt = xp
    y_sample = xs.reshape(DEC_BATCH, 1, D_MODEL)
    kv_shape = (DEPTH, BATCH, N_MEM, MEM_HEADS, MEM_HEAD_DIM)
    return (y_prompt, y_sample,
            jnp.stack(re_p), jnp.stack(im_p), jnp.stack(pool_p),
            k_mem.reshape(kv_shape), v_mem.reshape(kv_shape),
            jnp.stack(re_s), jnp.stack(im_s), jnp.stack(pool_s))
```
